```python
import jax
import jax.numpy as jnp
from jax import lax
import numpy as np


D_MODEL = 1024
BATCH = 4
SEQ = 4096
DEPTH = 2

GRID_W = 64
CTX_LEN = 256
D_CONV = D_MODEL // 2
D_NA = D_MODEL - D_CONV
NA_HEADS = 8
HEAD_DIM = D_NA // NA_HEADS
CONV_WIDTH = 31
NA_KH = 8
NA_KW = 16
FNET_GROUPS = 4
D_FF = ((8 * D_MODEL // 3 + 255) // 256) * 256
N_MOD = 9
EPS = 1e-6

kernel_name = 'hybrid_conv_natten_fnet_dit_block'


def _rms(x, g):
    x32 = x.astype(jnp.float32)
    y = x32 * lax.rsqrt(jnp.mean(x32 * x32, axis=-1, keepdims=True) + EPS)
    return (y * g.astype(jnp.float32)).astype(x.dtype)


def _layernorm(x, g, b):
    x32 = x.astype(jnp.float32)
    mu = jnp.mean(x32, axis=-1, keepdims=True)
    var = jnp.mean(jnp.square(x32 - mu), axis=-1, keepdims=True)
    y = (x32 - mu) * lax.rsqrt(var + EPS)
    return (y * g.astype(jnp.float32) + b.astype(jnp.float32)).astype(x.dtype)


def _modulate(x, shift, scale):
    return x * (1 + scale) + shift


def _swiglu(x, w_in, w_out):
    gt, up = jnp.split(x @ w_in, 2, axis=-1)
    return (jax.nn.silu(gt) * up) @ w_out


def _ffn_sublayer(h, shift, scale, gate, g, w_in, w_out):
    return h + 0.5 * gate * _swiglu(_modulate(_rms(h, g), shift, scale), w_in, w_out)


def _conv_module(u, w, b, ln_g, ln_b):
    a, gt = jnp.split(u, 2, axis=-1)
    y = a * jax.nn.sigmoid(gt)
    y = lax.conv_general_dilated(
        y, w[:, None, :].astype(y.dtype), window_strides=(1,),
        padding=[(CONV_WIDTH // 2, CONV_WIDTH // 2)],
        dimension_numbers=('NWC', 'WIO', 'NWC'), feature_group_count=D_CONV) + b
    return jax.nn.silu(_layernorm(y, ln_g, ln_b))


def _split_heads(t):
    b, n, _ = t.shape
    return t.reshape(b, n, NA_HEADS, HEAD_DIM).transpose(0, 2, 1, 3)


def _neighbourhood_attention(q, k, v, kc, vc, rpb):
    b, h, L, dh = q.shape
    rows = L // GRID_W
    kh = min(NA_KH, rows)
    kw = NA_KW
    scale = dh ** -0.5
    qg = q.reshape(b, h, rows, GRID_W, dh)
    kg = k.reshape(b, h, rows, GRID_W, dh)
    vg = v.reshape(b, h, rows, GRID_W, dh)
    row_start = jnp.clip(jnp.arange(rows) - kh // 2, 0, rows - kh)
    col_start = jnp.clip(jnp.arange(GRID_W) - kw // 2, 0, GRID_W - kw)
    col_idx = col_start[:, None] + jnp.arange(kw)[None, :]
    col_off = col_idx - jnp.arange(GRID_W)[:, None]
    bias_col = rpb[:, :, col_off + NA_KW - 1]

    def row_block(args):
        q_r, r, rs = args
        k_r = lax.dynamic_slice_in_dim(kg, rs, kh, axis=2)
        v_r = lax.dynamic_slice_in_dim(vg, rs, kh, axis=2)
        k_win = k_r[:, :, :, col_idx]
        v_win = v_r[:, :, :, col_idx]
        s_loc = jnp.einsum('bhwd,bhawkd->bhwak', q_r, k_win)
        row_off = rs + jnp.arange(kh) - r
        bias = jnp.transpose(bias_col[:, row_off + NA_KH - 1], (0, 2, 1, 3))
        s_loc = (s_loc * scale + bias).reshape(b, h, GRID_W, kh * kw)
        s_ctx = jnp.einsum('bhwd,bhnd->bhwn', q_r, kc) * scale
        p = jax.nn.softmax(jnp.concatenate([s_loc, s_ctx], axis=-1).astype(jnp.float32), axis=-1).astype(q.dtype)
        p_loc = p[..., :kh * kw].reshape(b, h, GRID_W, kh, kw)
        p_ctx = p[..., kh * kw:]
        return (jnp.einsum('bhwak,bhawkd->bhwd', p_loc, v_win)
                + jnp.einsum('bhwn,bhnd->bhwd', p_ctx, vc))

    out = lax.map(row_block, (jnp.moveaxis(qg, 2, 0), jnp.arange(rows), row_start))
    return out.transpose(1, 0, 3, 2, 4).reshape(b, L, h * dh)


def _context_attention(qc, kc, vc):
    b, h, n, dh = qc.shape
    s = jnp.einsum('bhnd,bhmd->bhnm', qc, kc) * (dh ** -0.5)
    p = jax.nn.softmax(s.astype(jnp.float32), axis=-1).astype(qc.dtype)
    o = jnp.einsum('bhnm,bhmd->bhnd', p, vc)
    return o.transpose(0, 2, 1, 3).reshape(b, n, h * dh)


def _ab_mixer(hn, hcn, w_in, conv_w, conv_b, ln_g, ln_b, rpb, w_out):
    p = hn @ w_in
    pc = hcn @ w_in
    o1, o2, o3 = 2 * D_CONV, 2 * D_CONV + D_NA, 2 * D_CONV + 2 * D_NA
    conv_x = _conv_module(p[..., :o1], conv_w, conv_b, ln_g, ln_b)
    conv_c = _conv_module(pc[..., :o1], conv_w, conv_b, ln_g, ln_b)
    q, k, v = _split_heads(p[..., o1:o2]), _split_heads(p[..., o2:o3]), _split_heads(p[..., o3:])
    qc, kc, vc = _split_heads(pc[..., o1:o2]), _split_heads(pc[..., o2:o3]), _split_heads(pc[..., o3:])
    att_x = _neighbourhood_attention(q, k, v, kc, vc, rpb)
    att_c = _context_attention(qc, kc, vc)
    y = jnp.concatenate([conv_x, att_x], axis=-1) @ w_out
    yc = jnp.concatenate([conv_c, att_c], axis=-1) @ w_out
    return y, yc


def _fourier_mixer(hn, w, b):
    bsz, L, d = hn.shape
    g = hn.astype(jnp.float32).reshape(bsz, L, FNET_GROUPS, d // FNET_GROUPS)
    f = jnp.fft.fft2(g, axes=(1, 3), norm='ortho').real.astype(hn.dtype).reshape(bsz, L, d)
    return f @ w + b


def setup_inputs(seed: int = 0) -> dict:
    key = jax.random.key(seed)
    ks = jax.random.split(key, 20)
    n_even = (DEPTH + 1) // 2
    n_odd = DEPTH // 2
    nrm = jax.random.normal
    f32 = jnp.float32
    return {
        'x': nrm(ks[0], (BATCH, SEQ, D_MODEL), f32),
        'c': nrm(ks[1], (BATCH, D_MODEL), f32),
        'ctx': nrm(ks[2], (BATCH, CTX_LEN, D_MODEL), f32),
        'c_ctx': nrm(ks[3], (D_MODEL,), f32),
        'ada_w': 0.02 * nrm(ks[4], (DEPTH, D_MODEL, N_MOD * D_MODEL), f32),
        'ada_b': 0.01 * nrm(ks[5], (DEPTH, N_MOD * D_MODEL), f32),
        'norm_g': 1.0 + 0.01 * nrm(ks[6], (DEPTH, 3, D_MODEL), f32),
        'ffn_w_in': nrm(ks[7], (DEPTH, 2, D_MODEL, 2 * D_FF), f32) * D_MODEL ** -0.5,
        'ffn_w_out': nrm(ks[8], (DEPTH, 2, D_FF, D_MODEL), f32) * D_FF ** -0.5,
        'ab_w_in': nrm(ks[9], (n_even, D_MODEL, 2 * D_CONV + 3 * D_NA), f32) * D_MODEL ** -0.5,
        'conv_w': nrm(ks[10], (n_even, CONV_WIDTH, D_CONV), f32) * CONV_WIDTH ** -0.5,
        'conv_b': 0.01 * nrm(ks[11], (n_even, D_CONV), f32),
        'conv_ln_g': 1.0 + 0.01 * nrm(ks[12], (n_even, D_CONV), f32),
        'conv_ln_b': 0.01 * nrm(ks[13], (n_even, D_CONV), f32),
        'na_rpb': 0.1 * nrm(ks[14], (n_even, NA_HEADS, 2 * NA_KH - 1, 2 * NA_KW - 1), f32),
        'ab_w_out': nrm(ks[15], (n_even, D_CONV + D_NA, D_MODEL), f32) * (D_CONV + D_NA) ** -0.5,
        'fnet_w': nrm(ks[16], (n_odd, D_MODEL, D_MODEL), f32) * D_MODEL ** -0.5,
        'fnet_b': 0.01 * nrm(ks[17], (n_odd, D_MODEL), f32),
        'final_g': 1.0 + 0.01 * nrm(ks[18], (D_MODEL,), f32),
    }


def reference(x, c, ctx, c_ctx, ada_w, ada_b, norm_g, ffn_w_in, ffn_w_out, ab_w_in, conv_w, conv_b,
              conv_ln_g, conv_ln_b, na_rpb, ab_w_out, fnet_w, fnet_b, final_g):
    s_c = jax.nn.silu(c)
    s_cc = jax.nn.silu(c_ctx)
    h, hc = x, ctx
    for i in range(DEPTH):
        mod = (s_c @ ada_w[i] + ada_b[i]).reshape(c.shape[0], N_MOD, D_MODEL)[:, :, None, :]
        modc = (s_cc @ ada_w[i] + ada_b[i]).reshape(N_MOD, D_MODEL)
        m = [mod[:, k] for k in range(N_MOD)]
        mc = [modc[k] for k in range(N_MOD)]
        ctx_out = i < DEPTH - 1
        ctx_used = ctx_out or (i % 2 == 0)
        j = i // 2
        h = _ffn_sublayer(h, m[0], m[1], m[2], norm_g[i, 0], ffn_w_in[i, 0], ffn_w_out[i, 0])
        if ctx_used:
            hc = _ffn_sublayer(hc, mc[0], mc[1], mc[2], norm_g[i, 0], ffn_w_in[i, 0], ffn_w_out[i, 0])
        hn = _modulate(_rms(h, norm_g[i, 1]), m[3], m[4])
        if i % 2 == 0:
            hcn = _modulate(_rms(hc, norm_g[i, 1]), mc[3], mc[4])
            y, yc = _ab_mixer(hn, hcn, ab_w_in[j], conv_w[j], conv_b[j], conv_ln_g[j], conv_ln_b[j],
                              na_rpb[j], ab_w_out[j])
            h = h + m[5] * y
            if ctx_out:
                hc = hc + mc[5] * yc
        else:
            h = h + m[5] * _fourier_mixer(hn, fnet_w[j], fnet_b[j])
            if ctx_out:
                hcn = _modulate(_rms(hc, norm_g[i, 1]), mc[3], mc[4])
                hc = hc + mc[5] * _fourier_mixer(hcn, fnet_w[j], fnet_b[j])
        h = _ffn_sublayer(h, m[6], m[7], m[8], norm_g[i, 2], ffn_w_in[i, 1], ffn_w_out[i, 1])
        if ctx_out:
            hc = _ffn_sublayer(hc, mc[6], mc[7], mc[8], norm_g[i, 2], ffn_w_in[i, 1], ffn_w_out[i, 1])
    return _rms(h, final_g)
```

```python
import functools

import numpy as np
import jax
import jax.numpy as jnp
from jax import lax
from jax.experimental import pallas as pl
from jax.experimental.pallas import tpu as pltpu

D_MODEL = 1024
GRID_W = 64
D_CONV = 512
D_NA = 512
NA_HEADS = 8
HEAD_DIM = 64
CONV_WIDTH = 31
NA_KH = 8
NA_KW = 16
FNET_GROUPS = 4
D_FF = 2816
N_MOD = 9
EPS = 1e-6

BF16 = jnp.bfloat16
F32 = jnp.float32

VMEM_LIMIT = 56 * 1024 * 1024
TM = 1024
FF_CHUNK = 256
CONV_TL = 512
CONV_HALO = 16
NA_ROWS = 8
NA_QB = NA_ROWS * GRID_W
NA_KROWS = 16
NA_KB = NA_KROWS * GRID_W
NEG = -1e30
FNET_TK = 512


def _params(*sem):
    return pltpu.CompilerParams(dimension_semantics=sem, vmem_limit_bytes=VMEM_LIMIT)


def _resident(shape, index_map):
    return pl.BlockSpec(shape, index_map, pipeline_mode=pl.Buffered(1))


def _silu(x):
    return x * (1.0 / (1.0 + jnp.exp(-x)))


def _dot(a, b):
    return jnp.dot(a, b, preferred_element_type=F32)


def _rms(x, g):
    return x * lax.rsqrt(jnp.mean(x * x, axis=-1, keepdims=True) + EPS) * g


def _modnorm(x, g, shift, scale):
    return (_rms(x, g) * (1.0 + scale) + shift).astype(BF16)


ADA_TN = 1536


def _ada_body(cc_ref, w_ref, b_ref, o_ref):
    s = _silu(cc_ref[...]).astype(BF16)
    o_ref[0] = _dot(s, w_ref[0].astype(BF16)) + b_ref[0]


def _ada(cc, ada_w, ada_b):
    depth, _, n = ada_w.shape
    return pl.pallas_call(
        _ada_body,
        grid=(depth, n // ADA_TN),
        in_specs=[
            pl.BlockSpec((8, D_MODEL), lambda i, j: (0, 0)),
            pl.BlockSpec((1, D_MODEL, ADA_TN), lambda i, j: (i, 0, j)),
            pl.BlockSpec((1, 1, ADA_TN), lambda i, j: (i, 0, j)),
        ],
        out_specs=pl.BlockSpec((1, 8, ADA_TN), lambda i, j: (i, 0, j)),
        out_shape=jax.ShapeDtypeStruct((depth, 8, n), F32),
        compiler_params=_params("arbitrary", "arbitrary"),
        name="ada_mod",
    )(cc, ada_w, ada_b.reshape(depth, 1, n))


def _mod_spec(row0, k):
    return pl.BlockSpec((1, 1, D_MODEL), lambda b, m: (row0 + b, 0, k))


def _const_row_spec(row):
    return pl.BlockSpec((1, 1, D_MODEL), lambda b, m: (row, 0, 0))


def _ffn_body(x_ref, sh_ref, sc_ref, gt_ref, g_ref, win_ref, wout_ref, fg_ref, o_ref, mid_ref, *, final):
    x = x_ref[0]
    xb = _modnorm(x, g_ref[0], sh_ref[0], sc_ref[0])
    for j in range(D_FF // FF_CHUNK):
        lo = j * FF_CHUNK
        gate = _dot(xb, win_ref[0, :, lo:lo + FF_CHUNK])
        up = _dot(xb, win_ref[0, :, D_FF + lo:D_FF + lo + FF_CHUNK])
        mid_ref[:, lo:lo + FF_CHUNK] = (_silu(gate) * up).astype(BF16)
    y = _dot(mid_ref[...], wout_ref[0])
    h = x + (0.5 * gt_ref[0]) * y
    if final:
        h = _rms(h, fg_ref[0])
    o_ref[0] = h


def _ffn(h, mod, row0, k0, g_all, g_row, w_in, w_out, w_idx, final_g, final):
    nb, n, _ = h.shape
    tm = min(TM, n)
    return pl.pallas_call(
        functools.partial(_ffn_body, final=final),
        grid=(nb, n // tm),
        in_specs=[
            pl.BlockSpec((1, tm, D_MODEL), lambda b, m: (b, m, 0)),
            _mod_spec(row0, k0), _mod_spec(row0, k0 + 1), _mod_spec(row0, k0 + 2),
            _const_row_spec(g_row),
            _resident((1, D_MODEL, 2 * D_FF), lambda b, m: (w_idx, 0, 0)),
            _resident((1, D_FF, D_MODEL), lambda b, m: (w_idx, 0, 0)),
            _const_row_spec(0),
        ],
        out_specs=pl.BlockSpec((1, tm, D_MODEL), lambda b, m: (b, m, 0)),
        out_shape=jax.ShapeDtypeStruct(h.shape, F32),
        scratch_shapes=[pltpu.VMEM((tm, D_FF), BF16)],
        compiler_params=_params("arbitrary", "arbitrary"),
        name="ffn_final" if final else "ffn",
    )(h, mod, mod, mod, g_all, w_in, w_out, final_g)


def _proj_body(x_ref, sh_ref, sc_ref, g_ref, w_ref, u_ref, qkv_ref):
    xb = _modnorm(x_ref[0], g_ref[0], sh_ref[0], sc_ref[0])
    u_ref[0] = _dot(xb, w_ref[:, :2 * D_CONV])
    qkv_ref[0] = _dot(xb, w_ref[:, 2 * D_CONV:]).astype(BF16)


def _proj(h, mod, row0, k0, g_all, g_row, w):
    nb, n, _ = h.shape
    tm = min(TM, n)
    n_out = w.shape[1]
    return pl.pallas_call(
        _proj_body,
        grid=(nb, n // tm),
        in_specs=[
            pl.BlockSpec((1, tm, D_MODEL), lambda b, m: (b, m, 0)),
            _mod_spec(row0, k0), _mod_spec(row0, k0 + 1),
            _const_row_spec(g_row),
            _resident((D_MODEL, n_out), lambda b, m: (0, 0)),
        ],
        out_specs=[
            pl.BlockSpec((1, tm, 2 * D_CONV), lambda b, m: (b, m, 0)),
            pl.BlockSpec((1, tm, 3 * D_NA), lambda b, m: (b, m, 0)),
        ],
        out_shape=[
            jax.ShapeDtypeStruct((nb, n, 2 * D_CONV), F32),
            jax.ShapeDtypeStruct((nb, n, 3 * D_NA), BF16),
        ],
        compiler_params=_params("arbitrary", "arbitrary"),
        name="ab_proj",
    )(h, mod, mod, g_all, w)


def _glu(v):
    return v[:, :D_CONV] * (1.0 / (1.0 + jnp.exp(-v[:, D_CONV:])))


CONV_RC = 64


def _conv_body(cur_ref, prev_ref, next_ref, w_ref, b_ref, lg_ref, lb_ref, o_ref, y_ref):
    t = pl.program_id(1)
    nt = pl.num_programs(1)
    y_ref[CONV_HALO:CONV_HALO + CONV_TL, :] = _glu(cur_ref[0])
    y_ref[0:CONV_HALO, :] = jnp.where(t > 0, _glu(prev_ref[0]), 0.0)
    y_ref[CONV_HALO + CONV_TL:, :] = jnp.where(t < nt - 1, _glu(next_ref[0]), 0.0)
    off = CONV_HALO - CONV_WIDTH // 2
    for r in range(0, CONV_TL, CONV_RC):
        acc = jnp.zeros((CONV_RC, D_CONV), F32)
        for k in range(CONV_WIDTH):
            acc = acc + w_ref[k:k + 1, :] * y_ref[r + off + k:r + off + k + CONV_RC, :]
        acc = acc + b_ref[...]
        mu = jnp.mean(acc, axis=-1, keepdims=True)
        cen = acc - mu
        var = jnp.mean(cen * cen, axis=-1, keepdims=True)
        z = cen * lax.rsqrt(var + EPS) * lg_ref[...] + lb_ref[...]
        o_ref[0, r:r + CONV_RC, :] = _silu(z).astype(BF16)


def _conv(u, w, b, ln_g, ln_b):
    nb, n, _ = u.shape
    nt = n // CONV_TL
    hb = CONV_TL // CONV_HALO
    last = n // CONV_HALO - 1
    row = lambda v: v.reshape(1, D_CONV)
    return pl.pallas_call(
        _conv_body,
        grid=(nb, nt),
        in_specs=[
            pl.BlockSpec((1, CONV_TL, 2 * D_CONV), lambda b_, t: (b_, t, 0)),
            pl.BlockSpec((1, CONV_HALO, 2 * D_CONV), lambda b_, t: (b_, jnp.maximum(t * hb - 1, 0), 0)),
            pl.BlockSpec((1, CONV_HALO, 2 * D_CONV), lambda b_, t: (b_, jnp.minimum((t + 1) * hb, last), 0)),
            pl.BlockSpec((CONV_WIDTH + 1, D_CONV), lambda b_, t: (0, 0)),
            pl.BlockSpec((1, D_CONV), lambda b_, t: (0, 0)),
            pl.BlockSpec((1, D_CONV), lambda b_, t: (0, 0)),
            pl.BlockSpec((1, D_CONV), lambda b_, t: (0, 0)),
        ],
        out_specs=pl.BlockSpec((1, CONV_TL, D_CONV), lambda b_, t: (b_, t, 0)),
        out_shape=jax.ShapeDtypeStruct((nb, n, D_CONV), BF16),
        scratch_shapes=[pltpu.VMEM((CONV_TL + 2 * CONV_HALO, D_CONV), F32)],
        compiler_params=_params("arbitrary", "arbitrary"),
        name="conv_module",
    )(u, u, u, jnp.pad(w, ((0, 1), (0, 0))), row(b), row(ln_g), row(ln_b))


def _na_static():
    rows = GRID_W
    qc = np.arange(GRID_W)
    cs = np.clip(qc - NA_KW // 2, 0, GRID_W - NA_KW)
    kc = np.arange(GRID_W)
    col_valid = (kc[None, :] >= cs[:, None]) & (kc[None, :] < cs[:, None] + NA_KW)
    dc = kc[None, :] - qc[:, None] + NA_KW - 1
    oh_c = np.zeros((2 * NA_KW - 1, GRID_W, GRID_W), np.float32)
    qi, ki = np.nonzero(col_valid)
    oh_c[dc[qi, ki], qi, ki] = 1.0
    blocks = (0, 1, rows // NA_ROWS - 1)
    oh_r = np.zeros((3, NA_ROWS, NA_KROWS, 2 * NA_KH - 1), np.float32)
    row_valid = np.zeros((3, NA_ROWS, NA_KROWS), bool)
    for c, blk in enumerate(blocks):
        w0 = int(np.clip(NA_ROWS * blk - NA_KH // 2, 0, rows - NA_KROWS))
        for i in range(NA_ROWS):
            r = NA_ROWS * blk + i
            rs = int(np.clip(r - NA_KH // 2, 0, rows - NA_KH))
            for a in range(NA_KROWS):
                kr = w0 + a
                if rs <= kr < rs + NA_KH:
                    row_valid[c, i, a] = True
                    oh_r[c, i, a, kr - r + NA_KH - 1] = 1.0
    valid = row_valid[:, :, None, :, None] & col_valid[None, None, :, None, :]
    return oh_c, oh_r, valid


def _na_bias_table(rpb):
    oh_c, oh_r, valid = _na_static()
    hi = lax.Precision.HIGHEST
    t_col = jnp.einsum('hrd,dqk->hrqk', rpb, jnp.asarray(oh_c), precision=hi)
    tbl = jnp.einsum('ciar,hrqk->hciqak', jnp.asarray(oh_r), t_col, precision=hi)
    tbl = jnp.where(jnp.asarray(valid)[None], tbl, NEG)
    return tbl.reshape(NA_HEADS, 3, NA_QB, NA_KB)


def _na_body(q_ref, k_ref, v_ref, kc_ref, vc_ref, tbl_ref, o_ref):
    blk = pl.program_id(2)
    nblk = pl.num_programs(2)
    w0 = jnp.clip(NA_ROWS * blk - NA_KH // 2, 0, GRID_W - NA_KROWS)
    start = pl.multiple_of(w0 * GRID_W, 256)
    case = jnp.where(blk == 0, 0, jnp.where(blk == nblk - 1, 2, 1))
    q2 = q_ref[0] * jnp.asarray(HEAD_DIM ** -0.5, BF16)
    kw = k_ref[0, pl.ds(start, NA_KB), :]
    vw = v_ref[0, pl.ds(start, NA_KB), :]
    kc = kc_ref[0]
    vc = vc_ref[0]
    lane = lax.broadcasted_iota(jnp.int32, (1, 2 * HEAD_DIM), 1)
    nt = (((1,), (1,)), ((), ()))
    outs = []
    for j in range(2):
        in_head = (lane >= j * HEAD_DIM) & (lane < (j + 1) * HEAD_DIM)
        qm = jnp.where(in_head, q2, jnp.zeros_like(q2))
        s = lax.dot_general(qm, kw, nt, preferred_element_type=F32) + tbl_ref[j, case]
        sc = lax.dot_general(qm, kc, nt, preferred_element_type=F32)
        m = jnp.maximum(jnp.max(s, axis=-1, keepdims=True), jnp.max(sc, axis=-1, keepdims=True))
        p = jnp.exp(s - m)
        pc = jnp.exp(sc - m)
        den = jnp.sum(p, axis=-1, keepdims=True) + jnp.sum(pc, axis=-1, keepdims=True)
        o = _dot(p.astype(BF16), vw) + _dot(pc.astype(BF16), vc)
        outs.append(o * (1.0 / den))
    o_ref[0] = jnp.where(lane < HEAD_DIM, outs[0], outs[1]).astype(BF16)


def _natten(qkv, qkv_c, tbl):
    nb, n, _ = qkv.shape
    nctx = qkv_c.shape[1]
    hp = NA_HEADS // 2
    return pl.pallas_call(
        _na_body,
        grid=(hp, nb, n // NA_QB),
        in_specs=[
            pl.BlockSpec((1, NA_QB, 2 * HEAD_DIM), lambda h, b, i: (b, i, h)),
            pl.BlockSpec((1, n, 2 * HEAD_DIM), lambda h, b, i: (b, 0, hp + h)),
            pl.BlockSpec((1, n, 2 * HEAD_DIM), lambda h, b, i: (b, 0, 2 * hp + h)),
            pl.BlockSpec((1, nctx, 2 * HEAD_DIM), lambda h, b, i: (b, 0, hp + h)),
            pl.BlockSpec((1, nctx, 2 * HEAD_DIM), lambda h, b, i: (b, 0, 2 * hp + h)),
            pl.BlockSpec((2, 3, NA_QB, NA_KB), lambda h, b, i: (h, 0, 0, 0)),
        ],
        out_specs=pl.BlockSpec((1, NA_QB, 2 * HEAD_DIM), lambda h, b, i: (b, i, h)),
        out_shape=jax.ShapeDtypeStruct((nb, n, D_NA), BF16),
        compiler_params=_params("arbitrary", "arbitrary", "arbitrary"),
        name="natten",
    )(qkv, qkv, qkv, qkv_c, qkv_c, tbl)


def _out_body(h_ref, cx_ref, ax_ref, gt_ref, w_ref, o_ref):
    y = _dot(cx_ref[0], w_ref[:D_CONV, :]) + _dot(ax_ref[0], w_ref[D_CONV:, :])
    o_ref[0] = h_ref[0] + gt_ref[0] * y


def _out_proj(h, conv_x, att_x, mod, row0, k, w):
    nb, n, _ = h.shape
    return pl.pallas_call(
        _out_body,
        grid=(nb, n // TM),
        in_specs=[
            pl.BlockSpec((1, TM, D_MODEL), lambda b, m: (b, m, 0)),
            pl.BlockSpec((1, TM, D_CONV), lambda b, m: (b, m, 0)),
            pl.BlockSpec((1, TM, D_NA), lambda b, m: (b, m, 0)),
            _mod_spec(row0, k),
            _resident((D_CONV + D_NA, D_MODEL), lambda b, m: (0, 0)),
        ],
        out_specs=pl.BlockSpec((1, TM, D_MODEL), lambda b, m: (b, m, 0)),
        out_shape=jax.ShapeDtypeStruct(h.shape, F32),
        compiler_params=_params("arbitrary", "arbitrary"),
        name="ab_out",
    )(h, conv_x, att_x, mod, w)


def _dft_tables(n):
    idx = np.arange(n, dtype=np.int64)
    ang = 2.0 * np.pi * ((idx[:, None] * idx[None, :]) % n).astype(np.float64) / n
    scale = 1.0 / np.sqrt(n)
    return (np.cos(ang) * scale).astype(np.float32), (np.sin(ang) * scale).astype(np.float32)


def _fnet_a_body(x_ref, sh_ref, sc_ref, g_ref, cs_ref, a_ref, b_ref):
    xb = _modnorm(x_ref[0], g_ref[0], sh_ref[0], sc_ref[0])
    gw = D_MODEL // FNET_GROUPS
    for grp in range(FNET_GROUPS):
        ab = _dot(xb[:, grp * gw:(grp + 1) * gw], cs_ref[...])
        a_ref[0, :, grp * gw:(grp + 1) * gw] = ab[:, :gw].astype(BF16)
        b_ref[0, :, grp * gw:(grp + 1) * gw] = ab[:, gw:].astype(BF16)


def _fnet_a(h, mod, row0, k0, g_all, g_row, cs):
    nb, n, _ = h.shape
    gw = D_MODEL // FNET_GROUPS
    spec = pl.BlockSpec((1, TM, D_MODEL), lambda b, m: (b, m, 0))
    return pl.pallas_call(
        _fnet_a_body,
        grid=(nb, n // TM),
        in_specs=[spec, _mod_spec(row0, k0), _mod_spec(row0, k0 + 1), _const_row_spec(g_row),
                  _resident((gw, 2 * gw), lambda b, m: (0, 0))],
        out_specs=[spec, spec],
        out_shape=[jax.ShapeDtypeStruct(h.shape, BF16)] * 2,
        compiler_params=_params("arbitrary", "arbitrary"),
        name="fnet_channel_dft",
    )(h, mod, mod, g_all, cs)


def _fnet_b_body(h_ref, a_ref, b_ref, cl_ref, sl_ref, gt_ref, w_ref, bias_ref, o_ref):
    f = _dot(cl_ref[...], a_ref[0]) + _dot(sl_ref[...], b_ref[0])
    y = _dot(f.astype(BF16), w_ref[...]) + bias_ref[...]
    o_ref[0] = h_ref[0] + gt_ref[0] * y


def _fnet_b(h, a, bneg, cl, sl, mod, row0, k, w, bias):
    nb, n, _ = h.shape
    tile = pl.BlockSpec((1, FNET_TK, D_MODEL), lambda b, t: (b, t, 0))
    full = pl.BlockSpec((1, n, D_MODEL), lambda b, t: (b, 0, 0), pipeline_mode=pl.Buffered(1))
    tab = pl.BlockSpec((FNET_TK, n), lambda b, t: (t, 0))
    return pl.pallas_call(
        _fnet_b_body,
        grid=(nb, n // FNET_TK),
        in_specs=[tile, full, full, tab, tab, _mod_spec(row0, k),
                  _resident((D_MODEL, D_MODEL), lambda b, t: (0, 0)),
                  pl.BlockSpec((1, D_MODEL), lambda b, t: (0, 0))],
        out_specs=tile,
        out_shape=jax.ShapeDtypeStruct(h.shape, F32),
        compiler_params=_params("arbitrary", "arbitrary"),
        name="fnet_seq_dft",
    )(h, a, bneg, cl, sl, mod, w, bias)


def kernel(x, c, ctx, c_ctx, ada_w, ada_b, norm_g, ffn_w_in, ffn_w_out, ab_w_in, conv_w, conv_b,
           conv_ln_g, conv_ln_b, na_rpb, ab_w_out, fnet_w, fnet_b, final_g):
    nb, n, d = x.shape
    depth = ada_w.shape[0]
    nctx = ctx.shape[1]
    assert (d, depth, nb) == (D_MODEL, 2, 4) and n == GRID_W * GRID_W

    cc = jnp.concatenate([c, c_ctx[None], jnp.zeros((8 - nb - 1, d), F32)], axis=0)
    mod = _ada(cc, ada_w, ada_b).reshape(depth * 8, 1, N_MOD * d)
    ctx_row = nb

    g_all = norm_g.reshape(depth * 3, 1, d)
    fg = final_g.reshape(1, 1, d)
    w_in = ffn_w_in.astype(BF16).reshape(depth * 2, d, 2 * D_FF)
    w_out = ffn_w_out.astype(BF16).reshape(depth * 2, D_FF, d)

    h = _ffn(x, mod, 0, 0, g_all, 0, w_in, w_out, 0, fg, False)
    hc = _ffn(ctx.reshape(1, nb * nctx, d), mod, ctx_row, 0, g_all, 0, w_in, w_out, 0, fg, False)
    w_ab = ab_w_in[0].astype(BF16)
    u, qkv = _proj(h, mod, 0, 3, g_all, 1, w_ab)
    _, qkv_c = _proj(hc, mod, ctx_row, 3, g_all, 1, w_ab)
    conv_x = _conv(u, conv_w[0], conv_b[0], conv_ln_g[0], conv_ln_b[0])
    att_x = _natten(qkv, qkv_c.reshape(nb, nctx, 3 * D_NA), _na_bias_table(na_rpb[0]))
    h = _out_proj(h, conv_x, att_x, mod, 0, 5, ab_w_out[0].astype(BF16))
    h = _ffn(h, mod, 0, 6, g_all, 2, w_in, w_out, 1, fg, False)

    h = _ffn(h, mod, 8, 0, g_all, 3, w_in, w_out, 2, fg, False)
    gw = d // FNET_GROUPS
    cc_tab, sc_tab = _dft_tables(gw)
    cs = jnp.asarray(np.concatenate([cc_tab, -sc_tab], axis=1)).astype(BF16)
    cl_tab, sl_tab = _dft_tables(n)
    a, bneg = _fnet_a(h, mod, 8, 3, g_all, 4, cs)
    h = _fnet_b(h, a, bneg, jnp.asarray(cl_tab).astype(BF16), jnp.asarray(sl_tab).astype(BF16), mod, 8, 5,
                fnet_w[0].astype(BF16), fnet_b[0].reshape(1, d))
    return _ffn(h, mod, 8, 6, g_all, 5, w_in, w_out, 3, fg, True)
```

```python
import functools

import numpy as np
import jax
import jax.numpy as jnp
from jax import lax
from jax.experimental import pallas as pl
from jax.experimental.pallas import tpu as pltpu

D_MODEL = 1024
GRID_W = 64
D_CONV = 512
D_NA = 512
NA_HEADS = 8
HEAD_DIM = 64
CONV_WIDTH = 31
NA_KH = 8
NA_KW = 16
FNET_GROUPS = 4
D_FF = 2816
N_MOD = 9
EPS = 1e-6

BF16 = jnp.bfloat16
F32 = jnp.float32

VMEM_LIMIT = 56 * 1024 * 1024
TM = 1024
FF_CHUNK = 256
CONV_TL = 512
CONV_HALO = 16
NA_ROWS = 8
NA_QB = NA_ROWS * GRID_W
NA_KROWS = 16
NA_KB = NA_KROWS * GRID_W
NEG = -1e30
FNET_TK = 512


def _params(*sem):
    return pltpu.CompilerParams(dimension_semantics=sem, vmem_limit_bytes=VMEM_LIMIT)


def _resident(shape, index_map):
    return pl.BlockSpec(shape, index_map, pipeline_mode=pl.Buffered(1))


def _silu(x):
    return x * (1.0 / (1.0 + jnp.exp(-x)))


def _dot(a, b):
    return jnp.dot(a, b, preferred_element_type=F32)


def _rms(x, g):
    return x * lax.rsqrt(jnp.mean(x * x, axis=-1, keepdims=True) + EPS) * g


def _modnorm(x, g, shift, scale):
    return (_rms(x, g) * (1.0 + scale) + shift).astype(BF16)


ADA_TN = 768
ADA_SPLIT = 3


def _ada_body(cc_ref, *refs):
    w_refs, b_ref, o_ref = refs[:ADA_SPLIT], refs[ADA_SPLIT], refs[ADA_SPLIT + 1]
    s = _silu(cc_ref[...]).astype(BF16)
    for q, w_ref in enumerate(w_refs):
        cols = slice(q * ADA_TN, (q + 1) * ADA_TN)
        o_ref[0, :, cols] = _dot(s, w_ref[0].astype(BF16)) + b_ref[0, :, cols]


def _ada(cc, ada_w, ada_b):
    depth, _, n = ada_w.shape
    step = ADA_SPLIT * ADA_TN

    def w_spec(q):
        return pl.BlockSpec((1, D_MODEL, ADA_TN), lambda i, j: (i, 0, ADA_SPLIT * j + q))

    return pl.pallas_call(
        _ada_body,
        grid=(depth, n // step),
        in_specs=[pl.BlockSpec((8, D_MODEL), lambda i, j: (0, 0))]
        + [w_spec(q) for q in range(ADA_SPLIT)]
        + [pl.BlockSpec((1, 1, step), lambda i, j: (i, 0, j))],
        out_specs=pl.BlockSpec((1, 8, step), lambda i, j: (i, 0, j)),
        out_shape=jax.ShapeDtypeStruct((depth, 8, n), F32),
        compiler_params=_params("arbitrary", "arbitrary"),
        name="ada_mod",
    )(cc, *([ada_w] * ADA_SPLIT), ada_b.reshape(depth, 1, n))


def _mod_spec(row0, k):
    return pl.BlockSpec((1, 1, D_MODEL), lambda b, m: (row0 + b, 0, k))


def _const_row_spec(row):
    return pl.BlockSpec((1, 1, D_MODEL), lambda b, m: (row, 0, 0))


def _ffn_body(x_ref, sh_ref, sc_ref, gt_ref, g_ref, win_ref, wout_ref, fg_ref, o_ref, mid_ref, *, final):
    x = x_ref[0]
    xb = _modnorm(x, g_ref[0], sh_ref[0], sc_ref[0])
    for j in range(D_FF // FF_CHUNK):
        lo = j * FF_CHUNK
        gate = _dot(xb, win_ref[0, :, lo:lo + FF_CHUNK])
        up = _dot(xb, win_ref[0, :, D_FF + lo:D_FF + lo + FF_CHUNK])
        mid_ref[:, lo:lo + FF_CHUNK] = (_silu(gate) * up).astype(BF16)
    y = _dot(mid_ref[...], wout_ref[0])
    h = x + (0.5 * gt_ref[0]) * y
    if final:
        h = _rms(h, fg_ref[0])
    o_ref[0] = h


def _ffn(h, mod, row0, k0, g_all, g_row, w_in, w_out, w_idx, final_g, final):
    nb, n, _ = h.shape
    tm = min(TM, n)
    return pl.pallas_call(
        functools.partial(_ffn_body, final=final),
        grid=(nb, n // tm),
        in_specs=[
            pl.BlockSpec((1, tm, D_MODEL), lambda b, m: (b, m, 0)),
            _mod_spec(row0, k0), _mod_spec(row0, k0 + 1), _mod_spec(row0, k0 + 2),
            _const_row_spec(g_row),
            _resident((1, D_MODEL, 2 * D_FF), lambda b, m: (w_idx, 0, 0)),
            _resident((1, D_FF, D_MODEL), lambda b, m: (w_idx, 0, 0)),
            _const_row_spec(0),
        ],
        out_specs=pl.BlockSpec((1, tm, D_MODEL), lambda b, m: (b, m, 0)),
        out_shape=jax.ShapeDtypeStruct(h.shape, F32),
        scratch_shapes=[pltpu.VMEM((tm, D_FF), BF16)],
        compiler_params=_params("arbitrary", "arbitrary"),
        name="ffn_final" if final else "ffn",
    )(h, mod, mod, mod, g_all, w_in, w_out, final_g)


def _proj_body(x_ref, sh_ref, sc_ref, g_ref, w_ref, u_ref, qkv_ref):
    xb = _modnorm(x_ref[0], g_ref[0], sh_ref[0], sc_ref[0])
    u_ref[0] = _dot(xb, w_ref[:, :2 * D_CONV])
    qkv_ref[0] = _dot(xb, w_ref[:, 2 * D_CONV:]).astype(BF16)


def _proj(h, mod, row0, k0, g_all, g_row, w):
    nb, n, _ = h.shape
    tm = min(TM, n)
    n_out = w.shape[1]
    return pl.pallas_call(
        _proj_body,
        grid=(nb, n // tm),
        in_specs=[
            pl.BlockSpec((1, tm, D_MODEL), lambda b, m: (b, m, 0)),
            _mod_spec(row0, k0), _mod_spec(row0, k0 + 1),
            _const_row_spec(g_row),
            _resident((D_MODEL, n_out), lambda b, m: (0, 0)),
        ],
        out_specs=[
            pl.BlockSpec((1, tm, 2 * D_CONV), lambda b, m: (b, m, 0)),
            pl.BlockSpec((1, tm, 3 * D_NA), lambda b, m: (b, m, 0)),
        ],
        out_shape=[
            jax.ShapeDtypeStruct((nb, n, 2 * D_CONV), F32),
            jax.ShapeDtypeStruct((nb, n, 3 * D_NA), BF16),
        ],
        compiler_params=_params("arbitrary", "arbitrary"),
        name="ab_proj",
    )(h, mod, mod, g_all, w)


def _glu(v):
    return v[:, :D_CONV] * (1.0 / (1.0 + jnp.exp(-v[:, D_CONV:])))


CONV_RC = 64
CONV_N = CONV_TL + 2 * CONV_HALO
SUBLANES = 8


def _conv_body(cur_ref, prev_ref, next_ref, w_ref, b_ref, lg_ref, lb_ref, o_ref, y_ref):
    t = pl.program_id(1)
    nt = pl.num_programs(1)
    y_ref[0, CONV_HALO:CONV_HALO + CONV_TL, :] = _glu(cur_ref[0])
    y_ref[0, 0:CONV_HALO, :] = jnp.where(t > 0, _glu(prev_ref[0]), 0.0)
    y_ref[0, CONV_HALO + CONV_TL:, :] = jnp.where(t < nt - 1, _glu(next_ref[0]), 0.0)
    for s in range(1, SUBLANES):
        y_ref[s, 0:CONV_N - SUBLANES, :] = y_ref[0, s:s + CONV_N - SUBLANES, :]
    off = CONV_HALO - CONV_WIDTH // 2
    for r in range(0, CONV_TL, CONV_RC):
        acc = jnp.zeros((CONV_RC // SUBLANES, SUBLANES, D_CONV), F32)
        for k in range(CONV_WIDTH):
            m8, s = divmod(off + k, SUBLANES)
            lo = r + SUBLANES * m8
            yk = y_ref[s, lo:lo + CONV_RC, :].reshape(CONV_RC // SUBLANES, SUBLANES, D_CONV)
            acc = acc + w_ref[k][None] * yk
        acc = acc.reshape(CONV_RC, D_CONV) + b_ref[...]
        mu = jnp.mean(acc, axis=-1, keepdims=True)
        cen = acc - mu
        var = jnp.mean(cen * cen, axis=-1, keepdims=True)
        z = cen * lax.rsqrt(var + EPS) * lg_ref[...] + lb_ref[...]
        o_ref[0, r:r + CONV_RC, :] = _silu(z).astype(BF16)


def _conv(u, w, b, ln_g, ln_b):
    nb, n, _ = u.shape
    nt = n // CONV_TL
    hb = CONV_TL // CONV_HALO
    last = n // CONV_HALO - 1
    row = lambda v: v.reshape(1, D_CONV)
    w_rep = jnp.broadcast_to(w[:, None, :], (CONV_WIDTH, SUBLANES, D_CONV))
    return pl.pallas_call(
        _conv_body,
        grid=(nb, nt),
        in_specs=[
            pl.BlockSpec((1, CONV_TL, 2 * D_CONV), lambda b_, t: (b_, t, 0)),
            pl.BlockSpec((1, CONV_HALO, 2 * D_CONV), lambda b_, t: (b_, jnp.maximum(t * hb - 1, 0), 0)),
            pl.BlockSpec((1, CONV_HALO, 2 * D_CONV), lambda b_, t: (b_, jnp.minimum((t + 1) * hb, last), 0)),
            pl.BlockSpec((CONV_WIDTH, SUBLANES, D_CONV), lambda b_, t: (0, 0, 0)),
            pl.BlockSpec((1, D_CONV), lambda b_, t: (0, 0)),
            pl.BlockSpec((1, D_CONV), lambda b_, t: (0, 0)),
            pl.BlockSpec((1, D_CONV), lambda b_, t: (0, 0)),
        ],
        out_specs=pl.BlockSpec((1, CONV_TL, D_CONV), lambda b_, t: (b_, t, 0)),
        out_shape=jax.ShapeDtypeStruct((nb, n, D_CONV), BF16),
        scratch_shapes=[pltpu.VMEM((SUBLANES, CONV_N, D_CONV), F32)],
        compiler_params=_params("arbitrary", "arbitrary"),
        name="conv_module",
    )(u, u, u, w_rep, row(b), row(ln_g), row(ln_b))


NA_HG = 4
NA_DR_PAD = 8
NA_T2 = 2 * NA_KH - 1 + 2 * NA_DR_PAD - 1


def _na_bias_table(rpb):
    qc = np.arange(GRID_W)
    cs = np.clip(qc - NA_KW // 2, 0, GRID_W - NA_KW)
    kc = np.arange(GRID_W)
    col_valid = (kc[None, :] >= cs[:, None]) & (kc[None, :] < cs[:, None] + NA_KW)
    dc = kc[None, :] - qc[:, None] + NA_KW - 1
    oh_c = np.zeros((2 * NA_KW - 1, GRID_W, GRID_W), np.float32)
    qi, ki = np.nonzero(col_valid)
    oh_c[dc[qi, ki], qi, ki] = 1.0
    t_col = jnp.einsum('hrd,dqk->hrqk', rpb, jnp.asarray(oh_c), precision=lax.Precision.HIGHEST)
    t_col = jnp.where(jnp.asarray(col_valid)[None, None], t_col, NEG)
    t_pad = jnp.pad(t_col, ((0, 0), (NA_DR_PAD, NA_DR_PAD), (0, 0), (0, 0)))
    return jnp.concatenate([t_pad[:, :NA_T2], t_pad[:, 1:NA_T2 + 1]], axis=-1)


def _na_body(q_ref, k_ref, v_ref, kc_ref, vc_ref, t2_ref, o_ref):
    blk = pl.program_id(2)
    w0 = jnp.clip(NA_ROWS * blk - NA_KH // 2, 0, GRID_W - NA_KROWS)
    start = pl.multiple_of(w0 * GRID_W, 256)
    base = w0 - NA_ROWS * blk + NA_KH - 1 + NA_DR_PAD
    q2 = q_ref[0] * jnp.asarray(HEAD_DIM ** -0.5, BF16)
    kw = k_ref[0, pl.ds(start, NA_KB), :]
    vw = v_ref[0, pl.ds(start, NA_KB), :]
    kc = kc_ref[0]
    vc = vc_ref[0]
    lane = lax.broadcasted_iota(jnp.int32, (1, NA_HG * HEAD_DIM), 1)
    klane = lax.broadcasted_iota(jnp.int32, (1, NA_KB), 1)
    row_valid = []
    for i in range(NA_ROWS):
        a_lo = jnp.clip(NA_ROWS * blk + i - NA_KH // 2, 0, GRID_W - NA_KH) - w0
        row_valid.append((klane >= a_lo * GRID_W) & (klane < (a_lo + NA_KH) * GRID_W))
    nt = (((1,), (1,)), ((), ()))
    one = jnp.ones((), BF16)
    out = None
    for j in range(NA_HG):
        in_head = (lane >= j * HEAD_DIM) & (lane < (j + 1) * HEAD_DIM)
        qm = jnp.where(in_head, q2, jnp.zeros_like(q2))
        s = lax.dot_general(qm, kw, nt, preferred_element_type=F32)
        parts = []
        for i in range(NA_ROWS):
            bias = jnp.concatenate([t2_ref[j, base + 2 * p - i] for p in range(NA_KROWS // 2)], axis=-1)
            parts.append(jnp.where(row_valid[i], s[i * GRID_W:(i + 1) * GRID_W] + bias, NEG))
        s = jnp.concatenate(parts, axis=0)
        sc = lax.dot_general(qm, kc, nt, preferred_element_type=F32)
        m = jnp.maximum(jnp.max(s, axis=-1, keepdims=True), jnp.max(sc, axis=-1, keepdims=True))
        p = jnp.exp(s - m).astype(BF16)
        pc = jnp.exp(sc - m).astype(BF16)
        o = _dot(p, jnp.where(in_head, vw, one)) + _dot(pc, jnp.where(in_head, vc, one))
        o = o * (1.0 / pltpu.roll(o, HEAD_DIM, axis=1))
        out = o if out is None else jnp.where(in_head, o, out)
    o_ref[0] = out.astype(BF16)


def _natten(qkv, qkv_c, t2):
    nb, n, _ = qkv.shape
    nctx = qkv_c.shape[1]
    ng = NA_HEADS // NA_HG
    lanes = NA_HG * HEAD_DIM
    return pl.pallas_call(
        _na_body,
        grid=(ng, nb, n // NA_QB),
        in_specs=[
            pl.BlockSpec((1, NA_QB, lanes), lambda h, b, i: (b, i, h)),
            pl.BlockSpec((1, n, lanes), lambda h, b, i: (b, 0, ng + h)),
            pl.BlockSpec((1, n, lanes), lambda h, b, i: (b, 0, 2 * ng + h)),
            pl.BlockSpec((1, nctx, lanes), lambda h, b, i: (b, 0, ng + h)),
            pl.BlockSpec((1, nctx, lanes), lambda h, b, i: (b, 0, 2 * ng + h)),
            pl.BlockSpec((NA_HG, NA_T2, GRID_W, 2 * GRID_W), lambda h, b, i: (h, 0, 0, 0)),
        ],
        out_specs=pl.BlockSpec((1, NA_QB, lanes), lambda h, b, i: (b, i, h)),
        out_shape=jax.ShapeDtypeStruct((nb, n, D_NA), BF16),
        compiler_params=_params("arbitrary", "arbitrary", "arbitrary"),
        name="natten",
    )(qkv, qkv, qkv, qkv_c, qkv_c, t2)


def _out_body(h_ref, cx_ref, ax_ref, gt_ref, w_ref, o_ref):
    y = _dot(cx_ref[0], w_ref[:D_CONV, :]) + _dot(ax_ref[0], w_ref[D_CONV:, :])
    o_ref[0] = h_ref[0] + gt_ref[0] * y


def _out_proj(h, conv_x, att_x, mod, row0, k, w):
    nb, n, _ = h.shape
    return pl.pallas_call(
        _out_body,
        grid=(nb, n // TM),
        in_specs=[
            pl.BlockSpec((1, TM, D_MODEL), lambda b, m: (b, m, 0)),
            pl.BlockSpec((1, TM, D_CONV), lambda b, m: (b, m, 0)),
            pl.BlockSpec((1, TM, D_NA), lambda b, m: (b, m, 0)),
            _mod_spec(row0, k),
            _resident((D_CONV + D_NA, D_MODEL), lambda b, m: (0, 0)),
        ],
        out_specs=pl.BlockSpec((1, TM, D_MODEL), lambda b, m: (b, m, 0)),
        out_shape=jax.ShapeDtypeStruct(h.shape, F32),
        compiler_params=_params("arbitrary", "arbitrary"),
        name="ab_out",
    )(h, conv_x, att_x, mod, w)


def _dft_tables(n):
    idx = np.arange(n, dtype=np.int64)
    ang = 2.0 * np.pi * ((idx[:, None] * idx[None, :]) % n).astype(np.float64) / n
    scale = 1.0 / np.sqrt(n)
    return (np.cos(ang) * scale).astype(np.float32), (np.sin(ang) * scale).astype(np.float32)


def _fnet_a_body(x_ref, sh_ref, sc_ref, g_ref, cs_ref, a_ref, b_ref):
    xb = _modnorm(x_ref[0], g_ref[0], sh_ref[0], sc_ref[0])
    gw = D_MODEL // FNET_GROUPS
    for grp in range(FNET_GROUPS):
        ab = _dot(xb[:, grp * gw:(grp + 1) * gw], cs_ref[...])
        a_ref[0, :, grp * gw:(grp + 1) * gw] = ab[:, :gw].astype(BF16)
        b_ref[0, :, grp * gw:(grp + 1) * gw] = ab[:, gw:].astype(BF16)


def _fnet_a(h, mod, row0, k0, g_all, g_row, cs):
    nb, n, _ = h.shape
    gw = D_MODEL // FNET_GROUPS
    spec = pl.BlockSpec((1, TM, D_MODEL), lambda b, m: (b, m, 0))
    return pl.pallas_call(
        _fnet_a_body,
        grid=(nb, n // TM),
        in_specs=[spec, _mod_spec(row0, k0), _mod_spec(row0, k0 + 1), _const_row_spec(g_row),
                  _resident((gw, 2 * gw), lambda b, m: (0, 0))],
        out_specs=[spec, spec],
        out_shape=[jax.ShapeDtypeStruct(h.shape, BF16)] * 2,
        compiler_params=_params("arbitrary", "arbitrary"),
        name="fnet_channel_dft",
    )(h, mod, mod, g_all, cs)


def _fnet_b_body(h_ref, a_ref, b_ref, cl_ref, sl_ref, gt_ref, w_ref, bias_ref, o_ref):
    f = _dot(cl_ref[...], a_ref[0]) + _dot(sl_ref[...], b_ref[0])
    y = _dot(f.astype(BF16), w_ref[...]) + bias_ref[...]
    o_ref[0] = h_ref[0] + gt_ref[0] * y


def _fnet_b(h, a, bneg, cl, sl, mod, row0, k, w, bias):
    nb, n, _ = h.shape
    tile = pl.BlockSpec((1, FNET_TK, D_MODEL), lambda b, t: (b, t, 0))
    full = pl.BlockSpec((1, n, D_MODEL), lambda b, t: (b, 0, 0), pipeline_mode=pl.Buffered(1))
    tab = pl.BlockSpec((FNET_TK, n), lambda b, t: (t, 0))
    return pl.pallas_call(
        _fnet_b_body,
        grid=(nb, n // FNET_TK),
        in_specs=[tile, full, full, tab, tab, _mod_spec(row0, k),
                  _resident((D_MODEL, D_MODEL), lambda b, t: (0, 0)),
                  pl.BlockSpec((1, D_MODEL), lambda b, t: (0, 0))],
        out_specs=tile,
        out_shape=jax.ShapeDtypeStruct(h.shape, F32),
        compiler_params=_params("arbitrary", "arbitrary"),
        name="fnet_seq_dft",
    )(h, a, bneg, cl, sl, mod, w, bias)


def kernel(x, c, ctx, c_ctx, ada_w, ada_b, norm_g, ffn_w_in, ffn_w_out, ab_w_in, conv_w, conv_b,
           conv_ln_g, conv_ln_b, na_rpb, ab_w_out, fnet_w, fnet_b, final_g):
    nb, n, d = x.shape
    depth = ada_w.shape[0]
    nctx = ctx.shape[1]
    assert (d, depth, nb) == (D_MODEL, 2, 4) and n == GRID_W * GRID_W

    cc = jnp.concatenate([c, c_ctx[None], jnp.zeros((8 - nb - 1, d), F32)], axis=0)
    mod = _ada(cc, ada_w, ada_b).reshape(depth * 8, 1, N_MOD * d)
    ctx_row = nb

    g_all = norm_g.reshape(depth * 3, 1, d)
    fg = final_g.reshape(1, 1, d)
    w_in = ffn_w_in.astype(BF16).reshape(depth * 2, d, 2 * D_FF)
    w_out = ffn_w_out.astype(BF16).reshape(depth * 2, D_FF, d)

    h = _ffn(x, mod, 0, 0, g_all, 0, w_in, w_out, 0, fg, False)
    hc = _ffn(ctx.reshape(1, nb * nctx, d), mod, ctx_row, 0, g_all, 0, w_in, w_out, 0, fg, False)
    w_ab = ab_w_in[0].astype(BF16)
    u, qkv = _proj(h, mod, 0, 3, g_all, 1, w_ab)
    _, qkv_c = _proj(hc, mod, ctx_row, 3, g_all, 1, w_ab)
    conv_x = _conv(u, conv_w[0], conv_b[0], conv_ln_g[0], conv_ln_b[0])
    att_x = _natten(qkv, qkv_c.reshape(nb, nctx, 3 * D_NA), _na_bias_table(na_rpb[0]))
    h = _out_proj(h, conv_x, att_x, mod, 0, 5, ab_w_out[0].astype(BF16))
    h = _ffn(h, mod, 0, 6, g_all, 2, w_in, w_out, 1, fg, False)

    h = _ffn(h, mod, 8, 0, g_all, 3, w_in, w_out, 2, fg, False)
    gw = d // FNET_GROUPS
    cc_tab, sc_tab = _dft_tables(gw)
    cs = jnp.asarray(np.concatenate([cc_tab, -sc_tab], axis=1)).astype(BF16)
    cl_tab, sl_tab = _dft_tables(n)
    a, bneg = _fnet_a(h, mod, 8, 3, g_all, 4, cs)
    h = _fnet_b(h, a, bneg, jnp.asarray(cl_tab).astype(BF16), jnp.asarray(sl_tab).astype(BF16), mod, 8, 5,
                fnet_w[0].astype(BF16), fnet_b[0].reshape(1, d))
    return _ffn(h, mod, 8, 6, g_all, 5, w_in, w_out, 3, fg, True)
```

```python
import functools

import numpy as np
import jax
import jax.numpy as jnp
from jax import lax
from jax.experimental import pallas as pl
from jax.experimental.pallas import tpu as pltpu

D_MODEL = 1024
GRID_W = 64
D_CONV = 512
D_NA = 512
NA_HEADS = 8
HEAD_DIM = 64
CONV_WIDTH = 31
NA_KH = 8
NA_KW = 16
FNET_GROUPS = 4
D_FF = 2816
N_MOD = 9
EPS = 1e-6

BF16 = jnp.bfloat16
F32 = jnp.float32

VMEM_LIMIT = 56 * 1024 * 1024
TM = 1024
FF_CHUNK = 256
CONV_TL = 512
CONV_HALO = 16
NA_ROWS = 8
NA_QB = NA_ROWS * GRID_W
NA_KROWS = 16
NA_KB = NA_KROWS * GRID_W
NEG = -1e30


def _params(*sem):
    return pltpu.CompilerParams(dimension_semantics=sem, vmem_limit_bytes=VMEM_LIMIT)


def _resident(shape, index_map):
    return pl.BlockSpec(shape, index_map, pipeline_mode=pl.Buffered(1))


def _silu(x):
    return x * (1.0 / (1.0 + jnp.exp(-x)))


def _dot(a, b):
    return jnp.dot(a, b, preferred_element_type=F32)


def _rms(x, g):
    return x * lax.rsqrt(jnp.mean(x * x, axis=-1, keepdims=True) + EPS) * g


def _modnorm(x, g, shift, scale):
    return (_rms(x, g) * (1.0 + scale) + shift).astype(BF16)


ADA_TN = 768
ADA_SPLIT = 3


def _ada_body(cc_ref, *refs):
    w_refs, b_ref, o_ref = refs[:ADA_SPLIT], refs[ADA_SPLIT], refs[ADA_SPLIT + 1]
    s = _silu(cc_ref[...]).astype(BF16)
    for q, w_ref in enumerate(w_refs):
        cols = slice(q * ADA_TN, (q + 1) * ADA_TN)
        o_ref[0, :, cols] = _dot(s, w_ref[0].astype(BF16)) + b_ref[0, :, cols]


def _ada(cc, ada_w, ada_b):
    depth, _, n = ada_w.shape
    step = ADA_SPLIT * ADA_TN

    def w_spec(q):
        return pl.BlockSpec((1, D_MODEL, ADA_TN), lambda i, j: (i, 0, ADA_SPLIT * j + q))

    return pl.pallas_call(
        _ada_body,
        grid=(depth, n // step),
        in_specs=[pl.BlockSpec((8, D_MODEL), lambda i, j: (0, 0))]
        + [w_spec(q) for q in range(ADA_SPLIT)]
        + [pl.BlockSpec((1, 1, step), lambda i, j: (i, 0, j))],
        out_specs=pl.BlockSpec((1, 8, step), lambda i, j: (i, 0, j)),
        out_shape=jax.ShapeDtypeStruct((depth, 8, n), F32),
        compiler_params=_params("arbitrary", "arbitrary"),
        name="ada_mod",
    )(cc, *([ada_w] * ADA_SPLIT), ada_b.reshape(depth, 1, n))


def _mod_spec(row0, k):
    return pl.BlockSpec((1, 1, D_MODEL), lambda b, m: (row0 + b, 0, k))


def _const_row_spec(row):
    return pl.BlockSpec((1, 1, D_MODEL), lambda b, m: (row, 0, 0))


def _ffn_body(x_ref, sh_ref, sc_ref, gt_ref, g_ref, win_ref, wout_ref, fg_ref, o_ref, mid_ref, *, final):
    x = x_ref[0]
    xb = _modnorm(x, g_ref[0], sh_ref[0], sc_ref[0])
    for j in range(D_FF // FF_CHUNK):
        lo = j * FF_CHUNK
        gate = _dot(xb, win_ref[0, :, lo:lo + FF_CHUNK])
        up = _dot(xb, win_ref[0, :, D_FF + lo:D_FF + lo + FF_CHUNK])
        mid_ref[:, lo:lo + FF_CHUNK] = (_silu(gate) * up).astype(BF16)
    y = _dot(mid_ref[...], wout_ref[0])
    h = x + (0.5 * gt_ref[0]) * y
    if final:
        h = _rms(h, fg_ref[0])
    o_ref[0] = h


def _ffn(h, mod, row0, k0, g_all, g_row, w_in, w_out, w_idx, final_g, final):
    nb, n, _ = h.shape
    tm = min(TM, n)
    return pl.pallas_call(
        functools.partial(_ffn_body, final=final),
        grid=(nb, n // tm),
        in_specs=[
            pl.BlockSpec((1, tm, D_MODEL), lambda b, m: (b, m, 0)),
            _mod_spec(row0, k0), _mod_spec(row0, k0 + 1), _mod_spec(row0, k0 + 2),
            _const_row_spec(g_row),
            _resident((1, D_MODEL, 2 * D_FF), lambda b, m: (w_idx, 0, 0)),
            _resident((1, D_FF, D_MODEL), lambda b, m: (w_idx, 0, 0)),
            _const_row_spec(0),
        ],
        out_specs=pl.BlockSpec((1, tm, D_MODEL), lambda b, m: (b, m, 0)),
        out_shape=jax.ShapeDtypeStruct(h.shape, F32),
        scratch_shapes=[pltpu.VMEM((tm, D_FF), BF16)],
        compiler_params=_params("arbitrary", "arbitrary"),
        name="ffn_final" if final else "ffn",
    )(h, mod, mod, mod, g_all, w_in, w_out, final_g)


def _proj_body(x_ref, sh_ref, sc_ref, g_ref, w_ref, u_ref, qkv_ref):
    xb = _modnorm(x_ref[0], g_ref[0], sh_ref[0], sc_ref[0])
    u_ref[0] = _dot(xb, w_ref[:, :2 * D_CONV])
    qkv_ref[0] = _dot(xb, w_ref[:, 2 * D_CONV:]).astype(BF16)


def _proj(h, mod, row0, k0, g_all, g_row, w):
    nb, n, _ = h.shape
    tm = min(TM, n)
    n_out = w.shape[1]
    return pl.pallas_call(
        _proj_body,
        grid=(nb, n // tm),
        in_specs=[
            pl.BlockSpec((1, tm, D_MODEL), lambda b, m: (b, m, 0)),
            _mod_spec(row0, k0), _mod_spec(row0, k0 + 1),
            _const_row_spec(g_row),
            _resident((D_MODEL, n_out), lambda b, m: (0, 0)),
        ],
        out_specs=[
            pl.BlockSpec((1, tm, 2 * D_CONV), lambda b, m: (b, m, 0)),
            pl.BlockSpec((1, tm, 3 * D_NA), lambda b, m: (b, m, 0)),
        ],
        out_shape=[
            jax.ShapeDtypeStruct((nb, n, 2 * D_CONV), F32),
            jax.ShapeDtypeStruct((nb, n, 3 * D_NA), BF16),
        ],
        compiler_params=_params("arbitrary", "arbitrary"),
        name="ab_proj",
    )(h, mod, mod, g_all, w)


def _glu(v):
    return v[:, :D_CONV] * (1.0 / (1.0 + jnp.exp(-v[:, D_CONV:])))


CONV_RC = 64
CONV_N = CONV_TL + 2 * CONV_HALO
SUBLANES = 8


def _conv_body(cur_ref, prev_ref, next_ref, w_ref, b_ref, lg_ref, lb_ref, o_ref, y_ref):
    t = pl.program_id(1)
    nt = pl.num_programs(1)
    y_ref[0, CONV_HALO:CONV_HALO + CONV_TL, :] = _glu(cur_ref[0])
    y_ref[0, 0:CONV_HALO, :] = jnp.where(t > 0, _glu(prev_ref[0]), 0.0)
    y_ref[0, CONV_HALO + CONV_TL:, :] = jnp.where(t < nt - 1, _glu(next_ref[0]), 0.0)
    for s in range(1, SUBLANES):
        y_ref[s, 0:CONV_N - SUBLANES, :] = y_ref[0, s:s + CONV_N - SUBLANES, :]
    off = CONV_HALO - CONV_WIDTH // 2
    for r in range(0, CONV_TL, CONV_RC):
        acc = jnp.zeros((CONV_RC // SUBLANES, SUBLANES, D_CONV), F32)
        for k in range(CONV_WIDTH):
            m8, s = divmod(off + k, SUBLANES)
            lo = r + SUBLANES * m8
            yk = y_ref[s, lo:lo + CONV_RC, :].reshape(CONV_RC // SUBLANES, SUBLANES, D_CONV)
            acc = acc + w_ref[k][None] * yk
        acc = acc.reshape(CONV_RC, D_CONV) + b_ref[...]
        mu = jnp.mean(acc, axis=-1, keepdims=True)
        cen = acc - mu
        var = jnp.mean(cen * cen, axis=-1, keepdims=True)
        z = cen * lax.rsqrt(var + EPS) * lg_ref[...] + lb_ref[...]
        o_ref[0, r:r + CONV_RC, :] = _silu(z).astype(BF16)


def _conv(u, w, b, ln_g, ln_b):
    nb, n, _ = u.shape
    nt = n // CONV_TL
    hb = CONV_TL // CONV_HALO
    last = n // CONV_HALO - 1
    row = lambda v: v.reshape(1, D_CONV)
    w_rep = jnp.broadcast_to(w[:, None, :], (CONV_WIDTH, SUBLANES, D_CONV))
    return pl.pallas_call(
        _conv_body,
        grid=(nb, nt),
        in_specs=[
            pl.BlockSpec((1, CONV_TL, 2 * D_CONV), lambda b_, t: (b_, t, 0)),
            pl.BlockSpec((1, CONV_HALO, 2 * D_CONV), lambda b_, t: (b_, jnp.maximum(t * hb - 1, 0), 0)),
            pl.BlockSpec((1, CONV_HALO, 2 * D_CONV), lambda b_, t: (b_, jnp.minimum((t + 1) * hb, last), 0)),
            pl.BlockSpec((CONV_WIDTH, SUBLANES, D_CONV), lambda b_, t: (0, 0, 0)),
            pl.BlockSpec((1, D_CONV), lambda b_, t: (0, 0)),
            pl.BlockSpec((1, D_CONV), lambda b_, t: (0, 0)),
            pl.BlockSpec((1, D_CONV), lambda b_, t: (0, 0)),
        ],
        out_specs=pl.BlockSpec((1, CONV_TL, D_CONV), lambda b_, t: (b_, t, 0)),
        out_shape=jax.ShapeDtypeStruct((nb, n, D_CONV), BF16),
        scratch_shapes=[pltpu.VMEM((SUBLANES, CONV_N, D_CONV), F32)],
        compiler_params=_params("arbitrary", "arbitrary"),
        name="conv_module",
    )(u, u, u, w_rep, row(b), row(ln_g), row(ln_b))


NA_HG = 4
NA_DR_PAD = 8
NA_T2 = 2 * NA_KH - 1 + 2 * NA_DR_PAD - 1


def _na_bias_table(rpb):
    qc = np.arange(GRID_W)
    cs = np.clip(qc - NA_KW // 2, 0, GRID_W - NA_KW)
    kc = np.arange(GRID_W)
    col_valid = (kc[None, :] >= cs[:, None]) & (kc[None, :] < cs[:, None] + NA_KW)
    dc = kc[None, :] - qc[:, None] + NA_KW - 1
    oh_c = np.zeros((2 * NA_KW - 1, GRID_W, GRID_W), np.float32)
    qi, ki = np.nonzero(col_valid)
    oh_c[dc[qi, ki], qi, ki] = 1.0
    t_col = jnp.einsum('hrd,dqk->hrqk', rpb, jnp.asarray(oh_c), precision=lax.Precision.HIGHEST)
    t_col = jnp.where(jnp.asarray(col_valid)[None, None], t_col, NEG)
    t_pad = jnp.pad(t_col, ((0, 0), (NA_DR_PAD, NA_DR_PAD), (0, 0), (0, 0)))
    return jnp.concatenate([t_pad[:, :NA_T2], t_pad[:, 1:NA_T2 + 1]], axis=-1)


def _na_body(q_ref, k_ref, v_ref, kc_ref, vc_ref, t2_ref, o_ref):
    blk = pl.program_id(2)
    w0 = jnp.clip(NA_ROWS * blk - NA_KH // 2, 0, GRID_W - NA_KROWS)
    start = pl.multiple_of(w0 * GRID_W, 256)
    base = w0 - NA_ROWS * blk + NA_KH - 1 + NA_DR_PAD
    q2 = q_ref[0] * jnp.asarray(HEAD_DIM ** -0.5, BF16)
    kw = k_ref[0, pl.ds(start, NA_KB), :]
    vw = v_ref[0, pl.ds(start, NA_KB), :]
    kc = kc_ref[0]
    vc = vc_ref[0]
    lane = lax.broadcasted_iota(jnp.int32, (1, NA_HG * HEAD_DIM), 1)
    klane = lax.broadcasted_iota(jnp.int32, (1, NA_KB), 1)
    row_valid = []
    for i in range(NA_ROWS):
        a_lo = jnp.clip(NA_ROWS * blk + i - NA_KH // 2, 0, GRID_W - NA_KH) - w0
        row_valid.append((klane >= a_lo * GRID_W) & (klane < (a_lo + NA_KH) * GRID_W))
    nt = (((1,), (1,)), ((), ()))
    one = jnp.ones((), BF16)
    out = None
    for j in range(NA_HG):
        in_head = (lane >= j * HEAD_DIM) & (lane < (j + 1) * HEAD_DIM)
        qm = jnp.where(in_head, q2, jnp.zeros_like(q2))
        s = lax.dot_general(qm, kw, nt, preferred_element_type=F32)
        parts = []
        for i in range(NA_ROWS):
            bias = jnp.concatenate([t2_ref[j, base + 2 * p - i] for p in range(NA_KROWS // 2)], axis=-1)
            parts.append(jnp.where(row_valid[i], s[i * GRID_W:(i + 1) * GRID_W] + bias, NEG))
        s = jnp.concatenate(parts, axis=0)
        sc = lax.dot_general(qm, kc, nt, preferred_element_type=F32)
        m = jnp.maximum(jnp.max(s, axis=-1, keepdims=True), jnp.max(sc, axis=-1, keepdims=True))
        p = jnp.exp(s - m).astype(BF16)
        pc = jnp.exp(sc - m).astype(BF16)
        o = _dot(p, jnp.where(in_head, vw, one)) + _dot(pc, jnp.where(in_head, vc, one))
        o = o * (1.0 / pltpu.roll(o, HEAD_DIM, axis=1))
        out = o if out is None else jnp.where(in_head, o, out)
    o_ref[0] = out.astype(BF16)


def _natten(qkv, qkv_c, t2):
    nb, n, _ = qkv.shape
    nctx = qkv_c.shape[1]
    ng = NA_HEADS // NA_HG
    lanes = NA_HG * HEAD_DIM
    return pl.pallas_call(
        _na_body,
        grid=(ng, nb, n // NA_QB),
        in_specs=[
            pl.BlockSpec((1, NA_QB, lanes), lambda h, b, i: (b, i, h)),
            pl.BlockSpec((1, n, lanes), lambda h, b, i: (b, 0, ng + h)),
            pl.BlockSpec((1, n, lanes), lambda h, b, i: (b, 0, 2 * ng + h)),
            pl.BlockSpec((1, nctx, lanes), lambda h, b, i: (b, 0, ng + h)),
            pl.BlockSpec((1, nctx, lanes), lambda h, b, i: (b, 0, 2 * ng + h)),
            pl.BlockSpec((NA_HG, NA_T2, GRID_W, 2 * GRID_W), lambda h, b, i: (h, 0, 0, 0)),
        ],
        out_specs=pl.BlockSpec((1, NA_QB, lanes), lambda h, b, i: (b, i, h)),
        out_shape=jax.ShapeDtypeStruct((nb, n, D_NA), BF16),
        compiler_params=_params("arbitrary", "arbitrary", "arbitrary"),
        name="natten",
    )(qkv, qkv, qkv, qkv_c, qkv_c, t2)


def _out_body(h_ref, cx_ref, ax_ref, gt_ref, w_ref, o_ref):
    y = _dot(cx_ref[0], w_ref[:D_CONV, :]) + _dot(ax_ref[0], w_ref[D_CONV:, :])
    o_ref[0] = h_ref[0] + gt_ref[0] * y


def _out_proj(h, conv_x, att_x, mod, row0, k, w):
    nb, n, _ = h.shape
    return pl.pallas_call(
        _out_body,
        grid=(nb, n // TM),
        in_specs=[
            pl.BlockSpec((1, TM, D_MODEL), lambda b, m: (b, m, 0)),
            pl.BlockSpec((1, TM, D_CONV), lambda b, m: (b, m, 0)),
            pl.BlockSpec((1, TM, D_NA), lambda b, m: (b, m, 0)),
            _mod_spec(row0, k),
            _resident((D_CONV + D_NA, D_MODEL), lambda b, m: (0, 0)),
        ],
        out_specs=pl.BlockSpec((1, TM, D_MODEL), lambda b, m: (b, m, 0)),
        out_shape=jax.ShapeDtypeStruct(h.shape, F32),
        compiler_params=_params("arbitrary", "arbitrary"),
        name="ab_out",
    )(h, conv_x, att_x, mod, w)


SEQ = GRID_W * GRID_W
FN_R = SUBLANES
FN_M = SEQ // FN_R
FN_TA = 1024
FN_LANES = 256
FN_CH = 16
LANES = 128


def _dft_tables(n):
    idx = np.arange(n, dtype=np.int64)
    ang = 2.0 * np.pi * ((idx[:, None] * idx[None, :]) % n).astype(np.float64) / n
    scale = 1.0 / np.sqrt(n)
    return (np.cos(ang) * scale).astype(np.float32), (np.sin(ang) * scale).astype(np.float32)


def _seq_tables():
    k2 = np.arange(FN_M, dtype=np.int64)
    n2 = np.arange(FN_M, dtype=np.int64)
    out = np.zeros((FN_R, 2 * FN_M, 2 * FN_M), np.float32)
    for n1 in range(FN_R):
        num = (k2[:, None] * n2[None, :] * FN_R + n1 * k2[:, None]) % SEQ
        ang = 2.0 * np.pi * num.astype(np.float64) / SEQ
        c = np.cos(ang) / np.sqrt(SEQ)
        s = np.sin(ang) / np.sqrt(SEQ)
        out[n1, :FN_M, :FN_M] = c
        out[n1, :FN_M, FN_M:] = s
        out[n1, FN_M:, :FN_M] = -s
        out[n1, FN_M:, FN_M:] = c
    return out


def _fnet_a_body(*refs):
    nx = D_MODEL // LANES
    x_refs = refs[:nx]
    sh_ref, sc_ref, g_ref, cs_ref, a_ref, b_ref = refs[nx:]
    gw = D_MODEL // FNET_GROUPS
    for s in range(FN_R):
        xs = jnp.concatenate([x[0, pl.ds(s, FN_TA // FN_R, stride=FN_R), :] for x in x_refs], axis=-1)
        xb = _modnorm(xs, g_ref[0], sh_ref[0], sc_ref[0])
        for grp in range(FNET_GROUPS):
            ab = _dot(xb[:, grp * gw:(grp + 1) * gw], cs_ref[...])
            a_ref[0, s, :, grp * gw:(grp + 1) * gw] = ab[:, :gw].astype(BF16)
            b_ref[0, s, :, grp * gw:(grp + 1) * gw] = ab[:, gw:].astype(BF16)


def _fnet_a(h, mod, row0, k0, g_all, g_row, cs):
    nb, n, _ = h.shape
    gw = D_MODEL // FNET_GROUPS
    nx = D_MODEL // LANES
    out_spec = pl.BlockSpec((1, FN_R, FN_TA // FN_R, D_MODEL), lambda b, m: (b, 0, m, 0))
    return pl.pallas_call(
        _fnet_a_body,
        grid=(nb, n // FN_TA),
        in_specs=[pl.BlockSpec((1, FN_TA, LANES), functools.partial(lambda b, m, c: (b, m, c), c=c))
                  for c in range(nx)]
        + [_mod_spec(row0, k0), _mod_spec(row0, k0 + 1), _const_row_spec(g_row),
           _resident((gw, 2 * gw), lambda b, m: (0, 0))],
        out_specs=[out_spec, out_spec],
        out_shape=[jax.ShapeDtypeStruct((nb, FN_R, n // FN_R, D_MODEL), BF16)] * 2,
        compiler_params=_params("arbitrary", "arbitrary"),
        name="fnet_channel_dft",
    )(*([h] * nx), mod, mod, g_all, cs)


def _fnet_b_body(zr_ref, zi_ref, m_ref, o_ref, v_ref):
    for n1 in range(FN_R):
        z = jnp.concatenate([zr_ref[0, n1], zi_ref[0, n1]], axis=0)
        v_ref[n1] = _dot(m_ref[n1], z)

    rt = np.float32(np.sqrt(0.5))

    def chunk(i, carry):
        r0 = pl.multiple_of(i * FN_CH, FN_CH)
        for lt in range(FN_LANES // LANES):
            ls = slice(lt * LANES, (lt + 1) * LANES)
            re = [v_ref[n, pl.ds(r0, FN_CH), ls] for n in range(FN_R)]
            im = [v_ref[n, pl.ds(FN_M + r0, FN_CH), ls] for n in range(FN_R)]
            e0 = (re[0] + re[4]) + (re[2] + re[6])
            e2 = (re[0] + re[4]) - (re[2] + re[6])
            e1 = (re[0] - re[4]) + (im[2] - im[6])
            e3 = (re[0] - re[4]) - (im[2] - im[6])
            t0r, t0i = re[1] + re[5], im[1] + im[5]
            t1r, t1i = re[1] - re[5], im[1] - im[5]
            t2r, t2i = re[3] + re[7], im[3] + im[7]
            t3r, t3i = re[3] - re[7], im[3] - im[7]
            p0 = t0r + t2r
            p2 = t0i - t2i
            al = t1r - t3r
            be = t1i + t3i
            p1 = (al + be) * rt
            p3 = (be - al) * rt
            ys = (e0 + p0, e1 + p1, e2 + p2, e3 + p3, e0 - p0, e1 - p1, e2 - p2, e3 - p3)
            for k1 in range(FN_R):
                o_ref[0, pl.ds(k1 * FN_M + r0, FN_CH), ls] = ys[k1].astype(BF16)
        return carry

    lax.fori_loop(0, FN_M // FN_CH, chunk, 0)


def _fnet_b(zr, zi, mtab):
    assert FN_R == 8
    nb = zr.shape[0]
    z_spec = pl.BlockSpec((1, FN_R, FN_M, FN_LANES), lambda b, l: (b, 0, 0, l))
    return pl.pallas_call(
        _fnet_b_body,
        grid=(nb, D_MODEL // FN_LANES),
        in_specs=[z_spec, z_spec, _resident((FN_R, 2 * FN_M, 2 * FN_M), lambda b, l: (0, 0, 0))],
        out_specs=pl.BlockSpec((1, SEQ, FN_LANES), lambda b, l: (b, 0, l)),
        out_shape=jax.ShapeDtypeStruct((nb, SEQ, D_MODEL), BF16),
        scratch_shapes=[pltpu.VMEM((FN_R, 2 * FN_M, FN_LANES), F32)],
        compiler_params=_params("arbitrary", "arbitrary"),
        name="fnet_seq_dft",
    )(zr, zi, mtab)


def _fnet_out_body(h_ref, f_ref, gt_ref, w_ref, bias_ref, o_ref):
    o_ref[0] = h_ref[0] + gt_ref[0] * (_dot(f_ref[0], w_ref[...]) + bias_ref[...])


def _fnet_out(h, f, mod, row0, k, w, bias):
    nb, n, _ = h.shape
    tile = pl.BlockSpec((1, TM, D_MODEL), lambda b, m: (b, m, 0))
    return pl.pallas_call(
        _fnet_out_body,
        grid=(nb, n // TM),
        in_specs=[tile, tile, _mod_spec(row0, k), _resident((D_MODEL, D_MODEL), lambda b, m: (0, 0)),
                  pl.BlockSpec((1, D_MODEL), lambda b, m: (0, 0))],
        out_specs=tile,
        out_shape=jax.ShapeDtypeStruct(h.shape, F32),
        compiler_params=_params("arbitrary", "arbitrary"),
        name="fnet_out",
    )(h, f, mod, w, bias)


def kernel(x, c, ctx, c_ctx, ada_w, ada_b, norm_g, ffn_w_in, ffn_w_out, ab_w_in, conv_w, conv_b,
           conv_ln_g, conv_ln_b, na_rpb, ab_w_out, fnet_w, fnet_b, final_g):
    nb, n, d = x.shape
    depth = ada_w.shape[0]
    nctx = ctx.shape[1]
    assert (d, depth, nb) == (D_MODEL, 2, 4) and n == GRID_W * GRID_W

    cc = jnp.concatenate([c, c_ctx[None], jnp.zeros((8 - nb - 1, d), F32)], axis=0)
    mod = _ada(cc, ada_w, ada_b).reshape(depth * 8, 1, N_MOD * d)
    ctx_row = nb

    g_all = norm_g.reshape(depth * 3, 1, d)
    fg = final_g.reshape(1, 1, d)
    w_in = ffn_w_in.astype(BF16).reshape(depth * 2, d, 2 * D_FF)
    w_out = ffn_w_out.astype(BF16).reshape(depth * 2, D_FF, d)

    h = _ffn(x, mod, 0, 0, g_all, 0, w_in, w_out, 0, fg, False)
    hc = _ffn(ctx.reshape(1, nb * nctx, d), mod, ctx_row, 0, g_all, 0, w_in, w_out, 0, fg, False)
    w_ab = ab_w_in[0].astype(BF16)
    u, qkv = _proj(h, mod, 0, 3, g_all, 1, w_ab)
    _, qkv_c = _proj(hc, mod, ctx_row, 3, g_all, 1, w_ab)
    conv_x = _conv(u, conv_w[0], conv_b[0], conv_ln_g[0], conv_ln_b[0])
    att_x = _natten(qkv, qkv_c.reshape(nb, nctx, 3 * D_NA), _na_bias_table(na_rpb[0]))
    h = _out_proj(h, conv_x, att_x, mod, 0, 5, ab_w_out[0].astype(BF16))
    h = _ffn(h, mod, 0, 6, g_all, 2, w_in, w_out, 1, fg, False)

    h = _ffn(h, mod, 8, 0, g_all, 3, w_in, w_out, 2, fg, False)
    gw = d // FNET_GROUPS
    cc_tab, sc_tab = _dft_tables(gw)
    cs = jnp.asarray(np.concatenate([cc_tab, -sc_tab], axis=1)).astype(BF16)
    zr, zi = _fnet_a(h, mod, 8, 3, g_all, 4, cs)
    f = _fnet_b(zr, zi, jnp.asarray(_seq_tables()).astype(BF16))
    h = _fnet_out(h, f, mod, 8, 5, fnet_w[0].astype(BF16), fnet_b[0].reshape(1, d))
    return _ffn(h, mod, 8, 6, g_all, 5, w_in, w_out, 3, fg, True)
```

```python
import functools

import numpy as np
import jax
import jax.numpy as jnp
from jax import lax
from jax.experimental import pallas as pl
from jax.experimental.pallas import tpu as pltpu

D_MODEL = 1024
GRID_W = 64
D_CONV = 512
D_NA = 512
NA_HEADS = 8
HEAD_DIM = 64
CONV_WIDTH = 31
NA_KH = 8
NA_KW = 16
FNET_GROUPS = 4
D_FF = 2816
N_MOD = 9
EPS = 1e-6

BF16 = jnp.bfloat16
F32 = jnp.float32

VMEM_LIMIT = 56 * 1024 * 1024
TM = 1024
FF_CHUNK = 256
CONV_TL = 512
CONV_HALO = 16
NA_ROWS = 8
NA_QB = NA_ROWS * GRID_W
NA_KROWS = 16
NA_KB = NA_KROWS * GRID_W
NEG = -1e30


def _params(*sem):
    return pltpu.CompilerParams(dimension_semantics=sem, vmem_limit_bytes=VMEM_LIMIT)


def _resident(shape, index_map):
    return pl.BlockSpec(shape, index_map, pipeline_mode=pl.Buffered(1))


def _silu(x):
    return x * (1.0 / (1.0 + jnp.exp(-x)))


def _dot(a, b):
    return jnp.dot(a, b, preferred_element_type=F32)


def _rms(x, g):
    return x * lax.rsqrt(jnp.mean(x * x, axis=-1, keepdims=True) + EPS) * g


def _modnorm(x, g, shift, scale):
    return (_rms(x, g) * (1.0 + scale) + shift).astype(BF16)


ADA_TN = 768
ADA_SPLIT = 3


def _ada_body(cc_ref, *refs):
    w_refs, b_ref, o_ref = refs[:ADA_SPLIT], refs[ADA_SPLIT], refs[ADA_SPLIT + 1]
    s = _silu(cc_ref[...]).astype(BF16)
    for q, w_ref in enumerate(w_refs):
        cols = slice(q * ADA_TN, (q + 1) * ADA_TN)
        o_ref[0, :, cols] = _dot(s, w_ref[0].astype(BF16)) + b_ref[0, :, cols]


def _ada(cc, ada_w, ada_b):
    depth, _, n = ada_w.shape
    step = ADA_SPLIT * ADA_TN

    def w_spec(q):
        return pl.BlockSpec((1, D_MODEL, ADA_TN), lambda i, j: (i, 0, ADA_SPLIT * j + q))

    return pl.pallas_call(
        _ada_body,
        grid=(depth, n // step),
        in_specs=[pl.BlockSpec((8, D_MODEL), lambda i, j: (0, 0))]
        + [w_spec(q) for q in range(ADA_SPLIT)]
        + [pl.BlockSpec((1, 1, step), lambda i, j: (i, 0, j))],
        out_specs=pl.BlockSpec((1, 8, step), lambda i, j: (i, 0, j)),
        out_shape=jax.ShapeDtypeStruct((depth, 8, n), F32),
        compiler_params=_params("arbitrary", "arbitrary"),
        name="ada_mod",
    )(cc, *([ada_w] * ADA_SPLIT), ada_b.reshape(depth, 1, n))


def _mod_spec(row0, k):
    return pl.BlockSpec((1, 1, D_MODEL), lambda b, m: (row0 + b, 0, k))


def _const_row_spec(row):
    return pl.BlockSpec((1, 1, D_MODEL), lambda b, m: (row, 0, 0))


def _ffn_body(x_ref, sh_ref, sc_ref, gt_ref, g_ref, win_ref, wout_ref, fg_ref, *rest, final, n_mix, mix_bias):
    mix_refs, (o_ref, mid_ref) = rest[:-2], rest[-2:]
    x = x_ref[0]
    if n_mix:
        acts, mg_ref, mw_ref = mix_refs[:n_mix], mix_refs[n_mix], mix_refs[n_mix + 1]
        y0, r = None, 0
        for a_ref in acts:
            k = a_ref.shape[-1]
            t = _dot(a_ref[0], mw_ref[r:r + k, :])
            y0 = t if y0 is None else y0 + t
            r += k
        if mix_bias:
            y0 = y0 + mix_refs[n_mix + 2][...]
        x = x + mg_ref[0] * y0
    xb = _modnorm(x, g_ref[0], sh_ref[0], sc_ref[0])
    for j in range(D_FF // FF_CHUNK):
        lo = j * FF_CHUNK
        gate = _dot(xb, win_ref[0, :, lo:lo + FF_CHUNK])
        up = _dot(xb, win_ref[0, :, D_FF + lo:D_FF + lo + FF_CHUNK])
        mid_ref[:, lo:lo + FF_CHUNK] = (_silu(gate) * up).astype(BF16)
    y = _dot(mid_ref[...], wout_ref[0])
    h = x + (0.5 * gt_ref[0]) * y
    if final:
        h = _rms(h, fg_ref[0])
    o_ref[0] = h


def _ffn(h, mod, row0, k0, g_all, g_row, w_in, w_out, w_idx, final_g, final, mix=None):
    nb, n, _ = h.shape
    tm = min(TM, n)
    in_specs = [
        pl.BlockSpec((1, tm, D_MODEL), lambda b, m: (b, m, 0)),
        _mod_spec(row0, k0), _mod_spec(row0, k0 + 1), _mod_spec(row0, k0 + 2),
        _const_row_spec(g_row),
        _resident((1, D_MODEL, 2 * D_FF), lambda b, m: (w_idx, 0, 0)),
        _resident((1, D_FF, D_MODEL), lambda b, m: (w_idx, 0, 0)),
        _const_row_spec(0),
    ]
    args = [h, mod, mod, mod, g_all, w_in, w_out, final_g]
    n_mix, mix_bias = 0, False
    if mix is not None:
        acts, mw, mb, mk = mix
        n_mix, mix_bias = len(acts), mb is not None
        in_specs += [pl.BlockSpec((1, tm, a.shape[-1]), lambda b, m: (b, m, 0)) for a in acts]
        in_specs += [_mod_spec(row0, mk), _resident(mw.shape, lambda b, m: (0, 0))]
        args += [*acts, mod, mw]
        if mix_bias:
            in_specs.append(pl.BlockSpec((1, D_MODEL), lambda b, m: (0, 0)))
            args.append(mb)
    return pl.pallas_call(
        functools.partial(_ffn_body, final=final, n_mix=n_mix, mix_bias=mix_bias),
        grid=(nb, n // tm),
        in_specs=in_specs,
        out_specs=pl.BlockSpec((1, tm, D_MODEL), lambda b, m: (b, m, 0)),
        out_shape=jax.ShapeDtypeStruct(h.shape, F32),
        scratch_shapes=[pltpu.VMEM((tm, D_FF), BF16)],
        compiler_params=_params("arbitrary", "arbitrary"),
        name="ffn_final" if final else "ffn",
    )(*args)


def _proj_body(x_ref, sh_ref, sc_ref, g_ref, w_ref, u_ref, qkv_ref):
    xb = _modnorm(x_ref[0], g_ref[0], sh_ref[0], sc_ref[0])
    u_ref[0] = _dot(xb, w_ref[:, :2 * D_CONV])
    qkv_ref[0] = _dot(xb, w_ref[:, 2 * D_CONV:]).astype(BF16)


def _proj(h, mod, row0, k0, g_all, g_row, w):
    nb, n, _ = h.shape
    tm = min(TM, n)
    n_out = w.shape[1]
    return pl.pallas_call(
        _proj_body,
        grid=(nb, n // tm),
        in_specs=[
            pl.BlockSpec((1, tm, D_MODEL), lambda b, m: (b, m, 0)),
            _mod_spec(row0, k0), _mod_spec(row0, k0 + 1),
            _const_row_spec(g_row),
            _resident((D_MODEL, n_out), lambda b, m: (0, 0)),
        ],
        out_specs=[
            pl.BlockSpec((1, tm, 2 * D_CONV), lambda b, m: (b, m, 0)),
            pl.BlockSpec((1, tm, 3 * D_NA), lambda b, m: (b, m, 0)),
        ],
        out_shape=[
            jax.ShapeDtypeStruct((nb, n, 2 * D_CONV), F32),
            jax.ShapeDtypeStruct((nb, n, 3 * D_NA), BF16),
        ],
        compiler_params=_params("arbitrary", "arbitrary"),
        name="ab_proj",
    )(h, mod, mod, g_all, w)


def _glu(v):
    return v[:, :D_CONV] * (1.0 / (1.0 + jnp.exp(-v[:, D_CONV:])))


CONV_RC = 64
CONV_N = CONV_TL + 2 * CONV_HALO
SUBLANES = 8


def _conv_body(cur_ref, prev_ref, next_ref, w_ref, b_ref, lg_ref, lb_ref, o_ref, y_ref):
    t = pl.program_id(1)
    nt = pl.num_programs(1)
    y_ref[0, CONV_HALO:CONV_HALO + CONV_TL, :] = _glu(cur_ref[0])
    y_ref[0, 0:CONV_HALO, :] = jnp.where(t > 0, _glu(prev_ref[0]), 0.0)
    y_ref[0, CONV_HALO + CONV_TL:, :] = jnp.where(t < nt - 1, _glu(next_ref[0]), 0.0)
    for s in range(1, SUBLANES):
        y_ref[s, 0:CONV_N - SUBLANES, :] = y_ref[0, s:s + CONV_N - SUBLANES, :]
    off = CONV_HALO - CONV_WIDTH // 2
    for r in range(0, CONV_TL, CONV_RC):
        acc = jnp.zeros((CONV_RC // SUBLANES, SUBLANES, D_CONV), F32)
        for k in range(CONV_WIDTH):
            m8, s = divmod(off + k, SUBLANES)
            lo = r + SUBLANES * m8
            yk = y_ref[s, lo:lo + CONV_RC, :].reshape(CONV_RC // SUBLANES, SUBLANES, D_CONV)
            acc = acc + w_ref[k][None] * yk
        acc = acc.reshape(CONV_RC, D_CONV) + b_ref[...]
        mu = jnp.mean(acc, axis=-1, keepdims=True)
        cen = acc - mu
        var = jnp.mean(cen * cen, axis=-1, keepdims=True)
        z = cen * lax.rsqrt(var + EPS) * lg_ref[...] + lb_ref[...]
        o_ref[0, r:r + CONV_RC, :] = _silu(z).astype(BF16)


def _conv(u, w, b, ln_g, ln_b):
    nb, n, _ = u.shape
    nt = n // CONV_TL
    hb = CONV_TL // CONV_HALO
    last = n // CONV_HALO - 1
    row = lambda v: v.reshape(1, D_CONV)
    w_rep = jnp.broadcast_to(w[:, None, :], (CONV_WIDTH, SUBLANES, D_CONV))
    return pl.pallas_call(
        _conv_body,
        grid=(nb, nt),
        in_specs=[
            pl.BlockSpec((1, CONV_TL, 2 * D_CONV), lambda b_, t: (b_, t, 0)),
            pl.BlockSpec((1, CONV_HALO, 2 * D_CONV), lambda b_, t: (b_, jnp.maximum(t * hb - 1, 0), 0)),
            pl.BlockSpec((1, CONV_HALO, 2 * D_CONV), lambda b_, t: (b_, jnp.minimum((t + 1) * hb, last), 0)),
            pl.BlockSpec((CONV_WIDTH, SUBLANES, D_CONV), lambda b_, t: (0, 0, 0)),
            pl.BlockSpec((1, D_CONV), lambda b_, t: (0, 0)),
            pl.BlockSpec((1, D_CONV), lambda b_, t: (0, 0)),
            pl.BlockSpec((1, D_CONV), lambda b_, t: (0, 0)),
        ],
        out_specs=pl.BlockSpec((1, CONV_TL, D_CONV), lambda b_, t: (b_, t, 0)),
        out_shape=jax.ShapeDtypeStruct((nb, n, D_CONV), BF16),
        scratch_shapes=[pltpu.VMEM((SUBLANES, CONV_N, D_CONV), F32)],
        compiler_params=_params("arbitrary", "arbitrary"),
        name="conv_module",
    )(u, u, u, w_rep, row(b), row(ln_g), row(ln_b))


NA_HG = 4
NA_DR_PAD = 8
NA_T2 = 2 * NA_KH - 1 + 2 * NA_DR_PAD - 1


def _na_bias_table(rpb):
    qc = np.arange(GRID_W)
    cs = np.clip(qc - NA_KW // 2, 0, GRID_W - NA_KW)
    kc = np.arange(GRID_W)
    col_valid = (kc[None, :] >= cs[:, None]) & (kc[None, :] < cs[:, None] + NA_KW)
    dc = kc[None, :] - qc[:, None] + NA_KW - 1
    oh_c = np.zeros((2 * NA_KW - 1, GRID_W, GRID_W), np.float32)
    qi, ki = np.nonzero(col_valid)
    oh_c[dc[qi, ki], qi, ki] = 1.0
    t_col = jnp.einsum('hrd,dqk->hrqk', rpb, jnp.asarray(oh_c), precision=lax.Precision.HIGHEST)
    t_col = jnp.where(jnp.asarray(col_valid)[None, None], t_col, NEG)
    t_pad = jnp.pad(t_col, ((0, 0), (NA_DR_PAD, NA_DR_PAD), (0, 0), (0, 0)))
    return jnp.concatenate([t_pad[:, :NA_T2], t_pad[:, 1:NA_T2 + 1]], axis=-1)


def _na_body(q_ref, k_ref, v_ref, kc_ref, vc_ref, t2_ref, o_ref):
    blk = pl.program_id(2)
    w0 = jnp.clip(NA_ROWS * blk - NA_KH // 2, 0, GRID_W - NA_KROWS)
    start = pl.multiple_of(w0 * GRID_W, 256)
    base = w0 - NA_ROWS * blk + NA_KH - 1 + NA_DR_PAD
    q2 = q_ref[0] * jnp.asarray(HEAD_DIM ** -0.5, BF16)
    kw = k_ref[0, pl.ds(start, NA_KB), :]
    vw = v_ref[0, pl.ds(start, NA_KB), :]
    kc = kc_ref[0]
    vc = vc_ref[0]
    lane = lax.broadcasted_iota(jnp.int32, (1, NA_HG * HEAD_DIM), 1)
    klane = lax.broadcasted_iota(jnp.int32, (1, NA_KB), 1)
    row_valid = []
    for i in range(NA_ROWS):
        a_lo = jnp.clip(NA_ROWS * blk + i - NA_KH // 2, 0, GRID_W - NA_KH) - w0
        row_valid.append((klane >= a_lo * GRID_W) & (klane < (a_lo + NA_KH) * GRID_W))
    nt = (((1,), (1,)), ((), ()))
    one = jnp.ones((), BF16)
    out = None
    for j in range(NA_HG):
        in_head = (lane >= j * HEAD_DIM) & (lane < (j + 1) * HEAD_DIM)
        qm = jnp.where(in_head, q2, jnp.zeros_like(q2))
        s = lax.dot_general(qm, kw, nt, preferred_element_type=F32)
        parts = []
        for i in range(NA_ROWS):
            bias = jnp.concatenate([t2_ref[j, base + 2 * p - i] for p in range(NA_KROWS // 2)], axis=-1)
            parts.append(jnp.where(row_valid[i], s[i * GRID_W:(i + 1) * GRID_W] + bias, NEG))
        s = jnp.concatenate(parts, axis=0)
        sc = lax.dot_general(qm, kc, nt, preferred_element_type=F32)
        m = jnp.maximum(jnp.max(s, axis=-1, keepdims=True), jnp.max(sc, axis=-1, keepdims=True))
        p = jnp.exp(s - m).astype(BF16)
        pc = jnp.exp(sc - m).astype(BF16)
        o = _dot(p, jnp.where(in_head, vw, one)) + _dot(pc, jnp.where(in_head, vc, one))
        o = o * (1.0 / pltpu.roll(o, HEAD_DIM, axis=1))
        out = o if out is None else jnp.where(in_head, o, out)
    o_ref[0] = out.astype(BF16)


def _natten(qkv, qkv_c, t2):
    nb, n, _ = qkv.shape
    nctx = qkv_c.shape[1]
    ng = NA_HEADS // NA_HG
    lanes = NA_HG * HEAD_DIM
    return pl.pallas_call(
        _na_body,
        grid=(ng, nb, n // NA_QB),
        in_specs=[
            pl.BlockSpec((1, NA_QB, lanes), lambda h, b, i: (b, i, h)),
            pl.BlockSpec((1, n, lanes), lambda h, b, i: (b, 0, ng + h)),
            pl.BlockSpec((1, n, lanes), lambda h, b, i: (b, 0, 2 * ng + h)),
            pl.BlockSpec((1, nctx, lanes), lambda h, b, i: (b, 0, ng + h)),
            pl.BlockSpec((1, nctx, lanes), lambda h, b, i: (b, 0, 2 * ng + h)),
            pl.BlockSpec((NA_HG, NA_T2, GRID_W, 2 * GRID_W), lambda h, b, i: (h, 0, 0, 0)),
        ],
        out_specs=pl.BlockSpec((1, NA_QB, lanes), lambda h, b, i: (b, i, h)),
        out_shape=jax.ShapeDtypeStruct((nb, n, D_NA), BF16),
        compiler_params=_params("arbitrary", "arbitrary", "arbitrary"),
        name="natten",
    )(qkv, qkv, qkv, qkv_c, qkv_c, t2)


SEQ = GRID_W * GRID_W
FN_R = SUBLANES
FN_M = SEQ // FN_R
FN_TA = 1024
FN_LANES = 256
FN_CH = 16
LANES = 128


def _dft_tables(n):
    idx = np.arange(n, dtype=np.int64)
    ang = 2.0 * np.pi * ((idx[:, None] * idx[None, :]) % n).astype(np.float64) / n
    scale = 1.0 / np.sqrt(n)
    return (np.cos(ang) * scale).astype(np.float32), (np.sin(ang) * scale).astype(np.float32)


def _seq_tables():
    k2 = np.arange(FN_M, dtype=np.int64)
    n2 = np.arange(FN_M, dtype=np.int64)
    out = np.zeros((FN_R, 2 * FN_M, 2 * FN_M), np.float32)
    for n1 in range(FN_R):
        num = (k2[:, None] * n2[None, :] * FN_R + n1 * k2[:, None]) % SEQ
        ang = 2.0 * np.pi * num.astype(np.float64) / SEQ
        c = np.cos(ang) / np.sqrt(SEQ)
        s = np.sin(ang) / np.sqrt(SEQ)
        out[n1, :FN_M, :FN_M] = c
        out[n1, :FN_M, FN_M:] = s
        out[n1, FN_M:, :FN_M] = -s
        out[n1, FN_M:, FN_M:] = c
    return out


def _fnet_a_body(*refs):
    nx = D_MODEL // LANES
    x_refs = refs[:nx]
    sh_ref, sc_ref, g_ref, cs_ref, a_ref, b_ref = refs[nx:]
    gw = D_MODEL // FNET_GROUPS
    for s in range(FN_R):
        xs = jnp.concatenate([x[0, pl.ds(s, FN_TA // FN_R, stride=FN_R), :] for x in x_refs], axis=-1)
        xb = _modnorm(xs, g_ref[0], sh_ref[0], sc_ref[0])
        for grp in range(FNET_GROUPS):
            ab = _dot(xb[:, grp * gw:(grp + 1) * gw], cs_ref[...])
            a_ref[0, s, :, grp * gw:(grp + 1) * gw] = ab[:, :gw].astype(BF16)
            b_ref[0, s, :, grp * gw:(grp + 1) * gw] = ab[:, gw:].astype(BF16)


def _fnet_a(h, mod, row0, k0, g_all, g_row, cs):
    nb, n, _ = h.shape
    gw = D_MODEL // FNET_GROUPS
    nx = D_MODEL // LANES
    out_spec = pl.BlockSpec((1, FN_R, FN_TA // FN_R, D_MODEL), lambda b, m: (b, 0, m, 0))
    return pl.pallas_call(
        _fnet_a_body,
        grid=(nb, n // FN_TA),
        in_specs=[pl.BlockSpec((1, FN_TA, LANES), functools.partial(lambda b, m, c: (b, m, c), c=c))
                  for c in range(nx)]
        + [_mod_spec(row0, k0), _mod_spec(row0, k0 + 1), _const_row_spec(g_row),
           _resident((gw, 2 * gw), lambda b, m: (0, 0))],
        out_specs=[out_spec, out_spec],
        out_shape=[jax.ShapeDtypeStruct((nb, FN_R, n // FN_R, D_MODEL), BF16)] * 2,
        compiler_params=_params("arbitrary", "arbitrary"),
        name="fnet_channel_dft",
    )(*([h] * nx), mod, mod, g_all, cs)


def _fnet_b_body(zr_ref, zi_ref, m_ref, o_ref, v_ref):
    for n1 in range(FN_R):
        z = jnp.concatenate([zr_ref[0, n1], zi_ref[0, n1]], axis=0)
        v_ref[n1] = _dot(m_ref[n1], z)

    rt = np.float32(np.sqrt(0.5))

    def chunk(i, carry):
        r0 = pl.multiple_of(i * FN_CH, FN_CH)
        for lt in range(FN_LANES // LANES):
            ls = slice(lt * LANES, (lt + 1) * LANES)
            re = [v_ref[n, pl.ds(r0, FN_CH), ls] for n in range(FN_R)]
            im = [v_ref[n, pl.ds(FN_M + r0, FN_CH), ls] for n in range(FN_R)]
            e0 = (re[0] + re[4]) + (re[2] + re[6])
            e2 = (re[0] + re[4]) - (re[2] + re[6])
            e1 = (re[0] - re[4]) + (im[2] - im[6])
            e3 = (re[0] - re[4]) - (im[2] - im[6])
            t0r, t0i = re[1] + re[5], im[1] + im[5]
            t1r, t1i = re[1] - re[5], im[1] - im[5]
            t2r, t2i = re[3] + re[7], im[3] + im[7]
            t3r, t3i = re[3] - re[7], im[3] - im[7]
            p0 = t0r + t2r
            p2 = t0i - t2i
            al = t1r - t3r
            be = t1i + t3i
            p1 = (al + be) * rt
            p3 = (be - al) * rt
            ys = (e0 + p0, e1 + p1, e2 + p2, e3 + p3, e0 - p0, e1 - p1, e2 - p2, e3 - p3)
            for k1 in range(FN_R):
                o_ref[0, pl.ds(k1 * FN_M + r0, FN_CH), ls] = ys[k1].astype(BF16)
        return carry

    lax.fori_loop(0, FN_M // FN_CH, chunk, 0)


def _fnet_b(zr, zi, mtab):
    assert FN_R == 8
    nb = zr.shape[0]
    z_spec = pl.BlockSpec((1, FN_R, FN_M, FN_LANES), lambda b, l: (b, 0, 0, l))
    return pl.pallas_call(
        _fnet_b_body,
        grid=(nb, D_MODEL // FN_LANES),
        in_specs=[z_spec, z_spec, _resident((FN_R, 2 * FN_M, 2 * FN_M), lambda b, l: (0, 0, 0))],
        out_specs=pl.BlockSpec((1, SEQ, FN_LANES), lambda b, l: (b, 0, l)),
        out_shape=jax.ShapeDtypeStruct((nb, SEQ, D_MODEL), BF16),
        scratch_shapes=[pltpu.VMEM((FN_R, 2 * FN_M, FN_LANES), F32)],
        compiler_params=_params("arbitrary", "arbitrary"),
        name="fnet_seq_dft",
    )(zr, zi, mtab)


def kernel(x, c, ctx, c_ctx, ada_w, ada_b, norm_g, ffn_w_in, ffn_w_out, ab_w_in, conv_w, conv_b,
           conv_ln_g, conv_ln_b, na_rpb, ab_w_out, fnet_w, fnet_b, final_g):
    nb, n, d = x.shape
    depth = ada_w.shape[0]
    nctx = ctx.shape[1]
    assert (d, depth, nb) == (D_MODEL, 2, 4) and n == GRID_W * GRID_W

    cc = jnp.concatenate([c, c_ctx[None], jnp.zeros((8 - nb - 1, d), F32)], axis=0)
    mod = _ada(cc, ada_w, ada_b).reshape(depth * 8, 1, N_MOD * d)
    ctx_row = nb

    g_all = norm_g.reshape(depth * 3, 1, d)
    fg = final_g.reshape(1, 1, d)
    w_in = ffn_w_in.astype(BF16).reshape(depth * 2, d, 2 * D_FF)
    w_out = ffn_w_out.astype(BF16).reshape(depth * 2, D_FF, d)

    h = _ffn(x, mod, 0, 0, g_all, 0, w_in, w_out, 0, fg, False)
    hc = _ffn(ctx.reshape(1, nb * nctx, d), mod, ctx_row, 0, g_all, 0, w_in, w_out, 0, fg, False)
    w_ab = ab_w_in[0].astype(BF16)
    u, qkv = _proj(h, mod, 0, 3, g_all, 1, w_ab)
    _, qkv_c = _proj(hc, mod, ctx_row, 3, g_all, 1, w_ab)
    conv_x = _conv(u, conv_w[0], conv_b[0], conv_ln_g[0], conv_ln_b[0])
    att_x = _natten(qkv, qkv_c.reshape(nb, nctx, 3 * D_NA), _na_bias_table(na_rpb[0]))
    h = _ffn(h, mod, 0, 6, g_all, 2, w_in, w_out, 1, fg, False,
             mix=((conv_x, att_x), ab_w_out[0].astype(BF16), None, 5))

    h = _ffn(h, mod, 8, 0, g_all, 3, w_in, w_out, 2, fg, False)
    gw = d // FNET_GROUPS
    cc_tab, sc_tab = _dft_tables(gw)
    cs = jnp.asarray(np.concatenate([cc_tab, -sc_tab], axis=1)).astype(BF16)
    zr, zi = _fnet_a(h, mod, 8, 3, g_all, 4, cs)
    f = _fnet_b(zr, zi, jnp.asarray(_seq_tables()).astype(BF16))
    return _ffn(h, mod, 8, 6, g_all, 5, w_in, w_out, 3, fg, True,
                mix=((f,), fnet_w[0].astype(BF16), fnet_b[0].reshape(1, d), 5))
```

```python
import functools

import numpy as np
import jax
import jax.numpy as jnp
from jax import lax
from jax.experimental import pallas as pl
from jax.experimental.pallas import tpu as pltpu

D_MODEL = 1024
GRID_W = 64
D_CONV = 512
D_NA = 512
NA_HEADS = 8
HEAD_DIM = 64
CONV_WIDTH = 31
NA_KH = 8
NA_KW = 16
FNET_GROUPS = 4
D_FF = 2816
N_MOD = 9
EPS = 1e-6

BF16 = jnp.bfloat16
F32 = jnp.float32

VMEM_LIMIT = 56 * 1024 * 1024
FFN_VMEM_LIMIT = 60 * 1024 * 1024
TM = 1024
FF_CHUNK = 256
CONV_TL = 512
CONV_HALO = 16
NA_ROWS = 8
NA_QB = NA_ROWS * GRID_W
NA_KROWS = 16
NA_KB = NA_KROWS * GRID_W
NEG = -1e30


def _params(*sem):
    return pltpu.CompilerParams(dimension_semantics=sem, vmem_limit_bytes=VMEM_LIMIT)


def _resident(shape, index_map):
    return pl.BlockSpec(shape, index_map, pipeline_mode=pl.Buffered(1))


def _silu(x):
    return x * (1.0 / (1.0 + jnp.exp(-x)))


def _dot(a, b):
    return jnp.dot(a, b, preferred_element_type=F32)


def _rms(x, g):
    return x * lax.rsqrt(jnp.mean(x * x, axis=-1, keepdims=True) + EPS) * g


def _modnorm(x, g, shift, scale):
    return (_rms(x, g) * (1.0 + scale) + shift).astype(BF16)


ADA_TN = 768
ADA_SPLIT = 3


def _ada_body(cc_ref, *refs):
    w_refs, b_ref, o_ref = refs[:ADA_SPLIT], refs[ADA_SPLIT], refs[ADA_SPLIT + 1]
    s = _silu(cc_ref[...]).astype(BF16)
    for q, w_ref in enumerate(w_refs):
        cols = slice(q * ADA_TN, (q + 1) * ADA_TN)
        o_ref[0, :, cols] = _dot(s, w_ref[0].astype(BF16)) + b_ref[0, :, cols]


def _ada(cc, ada_w, ada_b):
    depth, _, n = ada_w.shape
    step = ADA_SPLIT * ADA_TN

    def w_spec(q):
        return pl.BlockSpec((1, D_MODEL, ADA_TN), lambda i, j: (i, 0, ADA_SPLIT * j + q))

    return pl.pallas_call(
        _ada_body,
        grid=(depth, n // step),
        in_specs=[pl.BlockSpec((8, D_MODEL), lambda i, j: (0, 0))]
        + [w_spec(q) for q in range(ADA_SPLIT)]
        + [pl.BlockSpec((1, 1, step), lambda i, j: (i, 0, j))],
        out_specs=pl.BlockSpec((1, 8, step), lambda i, j: (i, 0, j)),
        out_shape=jax.ShapeDtypeStruct((depth, 8, n), F32),
        compiler_params=_params("arbitrary", "arbitrary"),
        name="ada_mod",
    )(cc, *([ada_w] * ADA_SPLIT), ada_b.reshape(depth, 1, n))


def _mod_spec(row0, k):
    return pl.BlockSpec((1, 1, D_MODEL), lambda b, m: (row0 + b, 0, k))


def _const_row_spec(row):
    return pl.BlockSpec((1, 1, D_MODEL), lambda b, m: (row, 0, 0))


W_CH = 128
W_SLOTS = 2


def _weight_copies(win_hbm, wout_hbm, st_in, st_out, sem, w_idx, j, slot):
    lo = j * W_CH
    return (
        pltpu.make_async_copy(win_hbm.at[w_idx, :, pl.ds(lo, W_CH)], st_in.at[slot, 0], sem.at[slot, 0]),
        pltpu.make_async_copy(win_hbm.at[w_idx, :, pl.ds(D_FF + lo, W_CH)], st_in.at[slot, 1], sem.at[slot, 1]),
        pltpu.make_async_copy(wout_hbm.at[w_idx, pl.ds(lo, W_CH), :], st_out.at[slot], sem.at[slot, 2]),
    )


def _load_weights_bf16(win_hbm, wout_hbm, win_ref, wout_ref, st_in, st_out, sem, w_idx):
    n_ch = D_FF // W_CH
    copies = functools.partial(_weight_copies, win_hbm, wout_hbm, st_in, st_out, sem, w_idx)
    for j in range(W_SLOTS):
        for cp in copies(j, j):
            cp.start()
    for j in range(n_ch):
        slot = j % W_SLOTS
        for cp in copies(j, slot):
            cp.wait()
        lo = j * W_CH
        win_ref[:, lo:lo + W_CH] = st_in[slot, 0].astype(BF16)
        win_ref[:, D_FF + lo:D_FF + lo + W_CH] = st_in[slot, 1].astype(BF16)
        wout_ref[lo:lo + W_CH, :] = st_out[slot].astype(BF16)
        if j + W_SLOTS < n_ch:
            for cp in copies(j + W_SLOTS, slot):
                cp.start()


def _ffn_body(x_ref, sh_ref, sc_ref, gt_ref, g_ref, win_hbm, wout_hbm, fg_ref, *rest,
              final, n_mix, mix_bias, w_idx):
    mix_refs = rest[:-7]
    o_ref, mid_ref, win_ref, wout_ref, st_in, st_out, sem = rest[-7:]

    @pl.when((pl.program_id(0) == 0) & (pl.program_id(1) == 0))
    def _():
        _load_weights_bf16(win_hbm, wout_hbm, win_ref, wout_ref, st_in, st_out, sem, w_idx)

    x = x_ref[0]
    if n_mix:
        acts, mg_ref, mw_ref = mix_refs[:n_mix], mix_refs[n_mix], mix_refs[n_mix + 1]
        y0, r = None, 0
        for a_ref in acts:
            k = a_ref.shape[-1]
            t = _dot(a_ref[0], mw_ref[r:r + k, :])
            y0 = t if y0 is None else y0 + t
            r += k
        if mix_bias:
            y0 = y0 + mix_refs[n_mix + 2][...]
        x = x + mg_ref[0] * y0
    xb = _modnorm(x, g_ref[0], sh_ref[0], sc_ref[0])
    for j in range(D_FF // FF_CHUNK):
        lo = j * FF_CHUNK
        gate = _dot(xb, win_ref[:, lo:lo + FF_CHUNK])
        up = _dot(xb, win_ref[:, D_FF + lo:D_FF + lo + FF_CHUNK])
        mid_ref[:, lo:lo + FF_CHUNK] = (_silu(gate) * up).astype(BF16)
    y = _dot(mid_ref[...], wout_ref[...])
    h = x + (0.5 * gt_ref[0]) * y
    if final:
        h = _rms(h, fg_ref[0])
    o_ref[0] = h


def _ffn(h, mod, row0, k0, g_all, g_row, w_in, w_out, w_idx, final_g, final, mix=None):
    nb, n, _ = h.shape
    tm = min(TM, n)
    in_specs = [
        pl.BlockSpec((1, tm, D_MODEL), lambda b, m: (b, m, 0)),
        _mod_spec(row0, k0), _mod_spec(row0, k0 + 1), _mod_spec(row0, k0 + 2),
        _const_row_spec(g_row),
        pl.BlockSpec(memory_space=pl.ANY),
        pl.BlockSpec(memory_space=pl.ANY),
        _const_row_spec(0),
    ]
    args = [h, mod, mod, mod, g_all, w_in, w_out, final_g]
    n_mix, mix_bias = 0, False
    if mix is not None:
        acts, mw, mb, mk = mix
        n_mix, mix_bias = len(acts), mb is not None
        in_specs += [pl.BlockSpec((1, tm, a.shape[-1]), lambda b, m: (b, m, 0)) for a in acts]
        in_specs += [_mod_spec(row0, mk), _resident(mw.shape, lambda b, m: (0, 0))]
        args += [*acts, mod, mw]
        if mix_bias:
            in_specs.append(pl.BlockSpec((1, D_MODEL), lambda b, m: (0, 0)))
            args.append(mb)
    return pl.pallas_call(
        functools.partial(_ffn_body, final=final, n_mix=n_mix, mix_bias=mix_bias, w_idx=w_idx),
        grid=(nb, n // tm),
        in_specs=in_specs,
        out_specs=pl.BlockSpec((1, tm, D_MODEL), lambda b, m: (b, m, 0)),
        out_shape=jax.ShapeDtypeStruct(h.shape, F32),
        scratch_shapes=[
            pltpu.VMEM((tm, D_FF), BF16),
            pltpu.VMEM((D_MODEL, 2 * D_FF), BF16),
            pltpu.VMEM((D_FF, D_MODEL), BF16),
            pltpu.VMEM((W_SLOTS, 2, D_MODEL, W_CH), F32),
            pltpu.VMEM((W_SLOTS, W_CH, D_MODEL), F32),
            pltpu.SemaphoreType.DMA((W_SLOTS, 3)),
        ],
        compiler_params=pltpu.CompilerParams(dimension_semantics=("arbitrary", "arbitrary"),
                                             vmem_limit_bytes=FFN_VMEM_LIMIT),
        name="ffn_final" if final else "ffn",
    )(*args)


def _proj_body(x_ref, sh_ref, sc_ref, g_ref, w_ref, u_ref, qkv_ref):
    xb = _modnorm(x_ref[0], g_ref[0], sh_ref[0], sc_ref[0])
    u_ref[0] = _dot(xb, w_ref[:, :2 * D_CONV])
    qkv_ref[0] = _dot(xb, w_ref[:, 2 * D_CONV:]).astype(BF16)


def _proj(h, mod, row0, k0, g_all, g_row, w):
    nb, n, _ = h.shape
    tm = min(TM, n)
    n_out = w.shape[1]
    return pl.pallas_call(
        _proj_body,
        grid=(nb, n // tm),
        in_specs=[
            pl.BlockSpec((1, tm, D_MODEL), lambda b, m: (b, m, 0)),
            _mod_spec(row0, k0), _mod_spec(row0, k0 + 1),
            _const_row_spec(g_row),
            _resident((D_MODEL, n_out), lambda b, m: (0, 0)),
        ],
        out_specs=[
            pl.BlockSpec((1, tm, 2 * D_CONV), lambda b, m: (b, m, 0)),
            pl.BlockSpec((1, tm, 3 * D_NA), lambda b, m: (b, m, 0)),
        ],
        out_shape=[
            jax.ShapeDtypeStruct((nb, n, 2 * D_CONV), F32),
            jax.ShapeDtypeStruct((nb, n, 3 * D_NA), BF16),
        ],
        compiler_params=_params("arbitrary", "arbitrary"),
        name="ab_proj",
    )(h, mod, mod, g_all, w)


def _glu(v):
    return v[:, :D_CONV] * (1.0 / (1.0 + jnp.exp(-v[:, D_CONV:])))


CONV_RC = 64
CONV_N = CONV_TL + 2 * CONV_HALO
SUBLANES = 8


def _conv_body(cur_ref, prev_ref, next_ref, w_ref, b_ref, lg_ref, lb_ref, o_ref, y_ref):
    t = pl.program_id(1)
    nt = pl.num_programs(1)
    y_ref[0, CONV_HALO:CONV_HALO + CONV_TL, :] = _glu(cur_ref[0])
    y_ref[0, 0:CONV_HALO, :] = jnp.where(t > 0, _glu(prev_ref[0]), 0.0)
    y_ref[0, CONV_HALO + CONV_TL:, :] = jnp.where(t < nt - 1, _glu(next_ref[0]), 0.0)
    for s in range(1, SUBLANES):
        y_ref[s, 0:CONV_N - SUBLANES, :] = y_ref[0, s:s + CONV_N - SUBLANES, :]
    off = CONV_HALO - CONV_WIDTH // 2
    for r in range(0, CONV_TL, CONV_RC):
        acc = jnp.zeros((CONV_RC // SUBLANES, SUBLANES, D_CONV), F32)
        for k in range(CONV_WIDTH):
            m8, s = divmod(off + k, SUBLANES)
            lo = r + SUBLANES * m8
            yk = y_ref[s, lo:lo + CONV_RC, :].reshape(CONV_RC // SUBLANES, SUBLANES, D_CONV)
            acc = acc + w_ref[k][None] * yk
        acc = acc.reshape(CONV_RC, D_CONV) + b_ref[...]
        mu = jnp.mean(acc, axis=-1, keepdims=True)
        cen = acc - mu
        var = jnp.mean(cen * cen, axis=-1, keepdims=True)
        z = cen * lax.rsqrt(var + EPS) * lg_ref[...] + lb_ref[...]
        o_ref[0, r:r + CONV_RC, :] = _silu(z).astype(BF16)


def _conv(u, w, b, ln_g, ln_b):
    nb, n, _ = u.shape
    nt = n // CONV_TL
    hb = CONV_TL // CONV_HALO
    last = n // CONV_HALO - 1
    row = lambda v: v.reshape(1, D_CONV)
    w_rep = jnp.broadcast_to(w[:, None, :], (CONV_WIDTH, SUBLANES, D_CONV))
    return pl.pallas_call(
        _conv_body,
        grid=(nb, nt),
        in_specs=[
            pl.BlockSpec((1, CONV_TL, 2 * D_CONV), lambda b_, t: (b_, t, 0)),
            pl.BlockSpec((1, CONV_HALO, 2 * D_CONV), lambda b_, t: (b_, jnp.maximum(t * hb - 1, 0), 0)),
            pl.BlockSpec((1, CONV_HALO, 2 * D_CONV), lambda b_, t: (b_, jnp.minimum((t + 1) * hb, last), 0)),
            pl.BlockSpec((CONV_WIDTH, SUBLANES, D_CONV), lambda b_, t: (0, 0, 0)),
            pl.BlockSpec((1, D_CONV), lambda b_, t: (0, 0)),
            pl.BlockSpec((1, D_CONV), lambda b_, t: (0, 0)),
            pl.BlockSpec((1, D_CONV), lambda b_, t: (0, 0)),
        ],
        out_specs=pl.BlockSpec((1, CONV_TL, D_CONV), lambda b_, t: (b_, t, 0)),
        out_shape=jax.ShapeDtypeStruct((nb, n, D_CONV), BF16),
        scratch_shapes=[pltpu.VMEM((SUBLANES, CONV_N, D_CONV), F32)],
        compiler_params=_params("arbitrary", "arbitrary"),
        name="conv_module",
    )(u, u, u, w_rep, row(b), row(ln_g), row(ln_b))


NA_HG = 4
NA_DR_PAD = 8
NA_T2 = 2 * NA_KH - 1 + 2 * NA_DR_PAD - 1


def _na_bias_table(rpb):
    qc = np.arange(GRID_W)
    cs = np.clip(qc - NA_KW // 2, 0, GRID_W - NA_KW)
    kc = np.arange(GRID_W)
    col_valid = (kc[None, :] >= cs[:, None]) & (kc[None, :] < cs[:, None] + NA_KW)
    dc = kc[None, :] - qc[:, None] + NA_KW - 1
    oh_c = np.zeros((2 * NA_KW - 1, GRID_W, GRID_W), np.float32)
    qi, ki = np.nonzero(col_valid)
    oh_c[dc[qi, ki], qi, ki] = 1.0
    t_col = jnp.einsum('hrd,dqk->hrqk', rpb, jnp.asarray(oh_c), precision=lax.Precision.HIGHEST)
    t_col = jnp.where(jnp.asarray(col_valid)[None, None], t_col, NEG)
    t_pad = jnp.pad(t_col, ((0, 0), (NA_DR_PAD, NA_DR_PAD), (0, 0), (0, 0)))
    return jnp.concatenate([t_pad[:, :NA_T2], t_pad[:, 1:NA_T2 + 1]], axis=-1)


def _na_body(q_ref, k_ref, v_ref, kc_ref, vc_ref, t2_ref, o_ref):
    blk = pl.program_id(2)
    w0 = jnp.clip(NA_ROWS * blk - NA_KH // 2, 0, GRID_W - NA_KROWS)
    start = pl.multiple_of(w0 * GRID_W, 256)
    base = w0 - NA_ROWS * blk + NA_KH - 1 + NA_DR_PAD
    q2 = q_ref[0] * jnp.asarray(HEAD_DIM ** -0.5, BF16)
    kw = k_ref[0, pl.ds(start, NA_KB), :]
    vw = v_ref[0, pl.ds(start, NA_KB), :]
    kc = kc_ref[0]
    vc = vc_ref[0]
    lane = lax.broadcasted_iota(jnp.int32, (1, NA_HG * HEAD_DIM), 1)
    klane = lax.broadcasted_iota(jnp.int32, (1, NA_KB), 1)
    row_valid = []
    for i in range(NA_ROWS):
        a_lo = jnp.clip(NA_ROWS * blk + i - NA_KH // 2, 0, GRID_W - NA_KH) - w0
        row_valid.append((klane >= a_lo * GRID_W) & (klane < (a_lo + NA_KH) * GRID_W))
    nt = (((1,), (1,)), ((), ()))
    one = jnp.ones((), BF16)
    out = None
    for j in range(NA_HG):
        in_head = (lane >= j * HEAD_DIM) & (lane < (j + 1) * HEAD_DIM)
        qm = jnp.where(in_head, q2, jnp.zeros_like(q2))
        s = lax.dot_general(qm, kw, nt, preferred_element_type=F32)
        parts = []
        for i in range(NA_ROWS):
            bias = jnp.concatenate([t2_ref[j, base + 2 * p - i] for p in range(NA_KROWS // 2)], axis=-1)
            parts.append(jnp.where(row_valid[i], s[i * GRID_W:(i + 1) * GRID_W] + bias, NEG))
        s = jnp.concatenate(parts, axis=0)
        sc = lax.dot_general(qm, kc, nt, preferred_element_type=F32)
        m = jnp.maximum(jnp.max(s, axis=-1, keepdims=True), jnp.max(sc, axis=-1, keepdims=True))
        p = jnp.exp(s - m).astype(BF16)
        pc = jnp.exp(sc - m).astype(BF16)
        o = _dot(p, jnp.where(in_head, vw, one)) + _dot(pc, jnp.where(in_head, vc, one))
        o = o * (1.0 / pltpu.roll(o, HEAD_DIM, axis=1))
        out = o if out is None else jnp.where(in_head, o, out)
    o_ref[0] = out.astype(BF16)


def _natten(qkv, qkv_c, t2):
    nb, n, _ = qkv.shape
    nctx = qkv_c.shape[1]
    ng = NA_HEADS // NA_HG
    lanes = NA_HG * HEAD_DIM
    return pl.pallas_call(
        _na_body,
        grid=(ng, nb, n // NA_QB),
        in_specs=[
            pl.BlockSpec((1, NA_QB, lanes), lambda h, b, i: (b, i, h)),
            pl.BlockSpec((1, n, lanes), lambda h, b, i: (b, 0, ng + h)),
            pl.BlockSpec((1, n, lanes), lambda h, b, i: (b, 0, 2 * ng + h)),
            pl.BlockSpec((1, nctx, lanes), lambda h, b, i: (b, 0, ng + h)),
            pl.BlockSpec((1, nctx, lanes), lambda h, b, i: (b, 0, 2 * ng + h)),
            pl.BlockSpec((NA_HG, NA_T2, GRID_W, 2 * GRID_W), lambda h, b, i: (h, 0, 0, 0)),
        ],
        out_specs=pl.BlockSpec((1, NA_QB, lanes), lambda h, b, i: (b, i, h)),
        out_shape=jax.ShapeDtypeStruct((nb, n, D_NA), BF16),
        compiler_params=_params("arbitrary", "arbitrary", "arbitrary"),
        name="natten",
    )(qkv, qkv, qkv, qkv_c, qkv_c, t2)


SEQ = GRID_W * GRID_W
FN_R = SUBLANES
FN_M = SEQ // FN_R
FN_TA = 1024
FN_LANES = 256
FN_CH = 16
LANES = 128


def _dft_tables(n):
    idx = np.arange(n, dtype=np.int64)
    ang = 2.0 * np.pi * ((idx[:, None] * idx[None, :]) % n).astype(np.float64) / n
    scale = 1.0 / np.sqrt(n)
    return (np.cos(ang) * scale).astype(np.float32), (np.sin(ang) * scale).astype(np.float32)


def _seq_tables():
    k2 = np.arange(FN_M, dtype=np.int64)
    n2 = np.arange(FN_M, dtype=np.int64)
    out = np.zeros((FN_R, 2 * FN_M, 2 * FN_M), np.float32)
    for n1 in range(FN_R):
        num = (k2[:, None] * n2[None, :] * FN_R + n1 * k2[:, None]) % SEQ
        ang = 2.0 * np.pi * num.astype(np.float64) / SEQ
        c = np.cos(ang) / np.sqrt(SEQ)
        s = np.sin(ang) / np.sqrt(SEQ)
        out[n1, :FN_M, :FN_M] = c
        out[n1, :FN_M, FN_M:] = s
        out[n1, FN_M:, :FN_M] = -s
        out[n1, FN_M:, FN_M:] = c
    return out


def _fnet_a_body(*refs):
    nx = D_MODEL // LANES
    x_refs = refs[:nx]
    sh_ref, sc_ref, g_ref, cs_ref, a_ref, b_ref = refs[nx:]
    gw = D_MODEL // FNET_GROUPS
    for s in range(FN_R):
        xs = jnp.concatenate([x[0, pl.ds(s, FN_TA // FN_R, stride=FN_R), :] for x in x_refs], axis=-1)
        xb = _modnorm(xs, g_ref[0], sh_ref[0], sc_ref[0])
        for grp in range(FNET_GROUPS):
            ab = _dot(xb[:, grp * gw:(grp + 1) * gw], cs_ref[...])
            a_ref[0, s, :, grp * gw:(grp + 1) * gw] = ab[:, :gw].astype(BF16)
            b_ref[0, s, :, grp * gw:(grp + 1) * gw] = ab[:, gw:].astype(BF16)


def _fnet_a(h, mod, row0, k0, g_all, g_row, cs):
    nb, n, _ = h.shape
    gw = D_MODEL // FNET_GROUPS
    nx = D_MODEL // LANES
    out_spec = pl.BlockSpec((1, FN_R, FN_TA // FN_R, D_MODEL), lambda b, m: (b, 0, m, 0))
    return pl.pallas_call(
        _fnet_a_body,
        grid=(nb, n // FN_TA),
        in_specs=[pl.BlockSpec((1, FN_TA, LANES), functools.partial(lambda b, m, c: (b, m, c), c=c))
                  for c in range(nx)]
        + [_mod_spec(row0, k0), _mod_spec(row0, k0 + 1), _const_row_spec(g_row),
           _resident((gw, 2 * gw), lambda b, m: (0, 0))],
        out_specs=[out_spec, out_spec],
        out_shape=[jax.ShapeDtypeStruct((nb, FN_R, n // FN_R, D_MODEL), BF16)] * 2,
        compiler_params=_params("arbitrary", "arbitrary"),
        name="fnet_channel_dft",
    )(*([h] * nx), mod, mod, g_all, cs)


def _fnet_b_body(zr_ref, zi_ref, m_ref, o_ref, v_ref):
    for n1 in range(FN_R):
        z = jnp.concatenate([zr_ref[0, n1], zi_ref[0, n1]], axis=0)
        v_ref[n1] = _dot(m_ref[n1], z)

    rt = np.float32(np.sqrt(0.5))

    def chunk(i, carry):
        r0 = pl.multiple_of(i * FN_CH, FN_CH)
        for lt in range(FN_LANES // LANES):
            ls = slice(lt * LANES, (lt + 1) * LANES)
            re = [v_ref[n, pl.ds(r0, FN_CH), ls] for n in range(FN_R)]
            im = [v_ref[n, pl.ds(FN_M + r0, FN_CH), ls] for n in range(FN_R)]
            e0 = (re[0] + re[4]) + (re[2] + re[6])
            e2 = (re[0] + re[4]) - (re[2] + re[6])
            e1 = (re[0] - re[4]) + (im[2] - im[6])
            e3 = (re[0] - re[4]) - (im[2] - im[6])
            t0r, t0i = re[1] + re[5], im[1] + im[5]
            t1r, t1i = re[1] - re[5], im[1] - im[5]
            t2r, t2i = re[3] + re[7], im[3] + im[7]
            t3r, t3i = re[3] - re[7], im[3] - im[7]
            p0 = t0r + t2r
            p2 = t0i - t2i
            al = t1r - t3r
            be = t1i + t3i
            p1 = (al + be) * rt
            p3 = (be - al) * rt
            ys = (e0 + p0, e1 + p1, e2 + p2, e3 + p3, e0 - p0, e1 - p1, e2 - p2, e3 - p3)
            for k1 in range(FN_R):
                o_ref[0, pl.ds(k1 * FN_M + r0, FN_CH), ls] = ys[k1].astype(BF16)
        return carry

    lax.fori_loop(0, FN_M // FN_CH, chunk, 0)


def _fnet_b(zr, zi, mtab):
    assert FN_R == 8
    nb = zr.shape[0]
    z_spec = pl.BlockSpec((1, FN_R, FN_M, FN_LANES), lambda b, l: (b, 0, 0, l))
    return pl.pallas_call(
        _fnet_b_body,
        grid=(nb, D_MODEL // FN_LANES),
        in_specs=[z_spec, z_spec, _resident((FN_R, 2 * FN_M, 2 * FN_M), lambda b, l: (0, 0, 0))],
        out_specs=pl.BlockSpec((1, SEQ, FN_LANES), lambda b, l: (b, 0, l)),
        out_shape=jax.ShapeDtypeStruct((nb, SEQ, D_MODEL), BF16),
        scratch_shapes=[pltpu.VMEM((FN_R, 2 * FN_M, FN_LANES), F32)],
        compiler_params=_params("arbitrary", "arbitrary"),
        name="fnet_seq_dft",
    )(zr, zi, mtab)


def kernel(x, c, ctx, c_ctx, ada_w, ada_b, norm_g, ffn_w_in, ffn_w_out, ab_w_in, conv_w, conv_b,
           conv_ln_g, conv_ln_b, na_rpb, ab_w_out, fnet_w, fnet_b, final_g):
    nb, n, d = x.shape
    depth = ada_w.shape[0]
    nctx = ctx.shape[1]
    assert (d, depth, nb) == (D_MODEL, 2, 4) and n == GRID_W * GRID_W

    cc = jnp.concatenate([c, c_ctx[None], jnp.zeros((8 - nb - 1, d), F32)], axis=0)
    mod = _ada(cc, ada_w, ada_b).reshape(depth * 8, 1, N_MOD * d)
    ctx_row = nb

    g_all = norm_g.reshape(depth * 3, 1, d)
    fg = final_g.reshape(1, 1, d)
    w_in = ffn_w_in.reshape(depth * 2, d, 2 * D_FF)
    w_out = ffn_w_out.reshape(depth * 2, D_FF, d)

    h = _ffn(x, mod, 0, 0, g_all, 0, w_in, w_out, 0, fg, False)
    hc = _ffn(ctx.reshape(1, nb * nctx, d), mod, ctx_row, 0, g_all, 0, w_in, w_out, 0, fg, False)
    w_ab = ab_w_in[0].astype(BF16)
    u, qkv = _proj(h, mod, 0, 3, g_all, 1, w_ab)
    _, qkv_c = _proj(hc, mod, ctx_row, 3, g_all, 1, w_ab)
    conv_x = _conv(u, conv_w[0], conv_b[0], conv_ln_g[0], conv_ln_b[0])
    att_x = _natten(qkv, qkv_c.reshape(nb, nctx, 3 * D_NA), _na_bias_table(na_rpb[0]))
    h = _ffn(h, mod, 0, 6, g_all, 2, w_in, w_out, 1, fg, False,
             mix=((conv_x, att_x), ab_w_out[0].astype(BF16), None, 5))

    h = _ffn(h, mod, 8, 0, g_all, 3, w_in, w_out, 2, fg, False)
    gw = d // FNET_GROUPS
    cc_tab, sc_tab = _dft_tables(gw)
    cs = jnp.asarray(np.concatenate([cc_tab, -sc_tab], axis=1)).astype(BF16)
    zr, zi = _fnet_a(h, mod, 8, 3, g_all, 4, cs)
    f = _fnet_b(zr, zi, jnp.asarray(_seq_tables()).astype(BF16))
    return _ffn(h, mod, 8, 6, g_all, 5, w_in, w_out, 3, fg, True,
                mix=((f,), fnet_w[0].astype(BF16), fnet_b[0].reshape(1, d), 5))
```

```python
import functools

import numpy as np
import jax
import jax.numpy as jnp
from jax import lax
from jax.experimental import pallas as pl
from jax.experimental.pallas import tpu as pltpu

D_MODEL = 1024
GRID_W = 64
D_CONV = 512
D_NA = 512
NA_HEADS = 8
HEAD_DIM = 64
CONV_WIDTH = 31
NA_KH = 8
NA_KW = 16
FNET_GROUPS = 4
D_FF = 2816
N_MOD = 9
EPS = 1e-6

BF16 = jnp.bfloat16
F32 = jnp.float32

VMEM_LIMIT = 56 * 1024 * 1024
FFN_VMEM_LIMIT = 60 * 1024 * 1024
TM = 1024
FF_CHUNK = 256
CONV_TL = 512
CONV_HALO = 16
NA_ROWS = 8
NA_QB = NA_ROWS * GRID_W
NA_KROWS = 16
NA_KB = NA_KROWS * GRID_W
NEG = -1e30


def _params(*sem):
    return pltpu.CompilerParams(dimension_semantics=sem, vmem_limit_bytes=VMEM_LIMIT)


def _resident(shape, index_map):
    return pl.BlockSpec(shape, index_map, pipeline_mode=pl.Buffered(1))


def _silu(x):
    return x * (1.0 / (1.0 + jnp.exp(-x)))


def _dot(a, b):
    return jnp.dot(a, b, preferred_element_type=F32)


def _rms(x, g):
    return x * lax.rsqrt(jnp.mean(x * x, axis=-1, keepdims=True) + EPS) * g


def _modnorm(x, g, shift, scale):
    return (_rms(x, g) * (1.0 + scale) + shift).astype(BF16)


ADA_TN = 768
ADA_SPLIT = 3


def _ada_body(cc_ref, *refs):
    w_refs, b_ref, o_ref = refs[:ADA_SPLIT], refs[ADA_SPLIT], refs[ADA_SPLIT + 1]
    s = _silu(cc_ref[...]).astype(BF16)
    for q, w_ref in enumerate(w_refs):
        cols = slice(q * ADA_TN, (q + 1) * ADA_TN)
        o_ref[0, :, cols] = _dot(s, w_ref[0].astype(BF16)) + b_ref[0, :, cols]


def _ada(cc, ada_w, ada_b):
    depth, _, n = ada_w.shape
    step = ADA_SPLIT * ADA_TN

    def w_spec(q):
        return pl.BlockSpec((1, D_MODEL, ADA_TN), lambda i, j: (i, 0, ADA_SPLIT * j + q))

    return pl.pallas_call(
        _ada_body,
        grid=(depth, n // step),
        in_specs=[pl.BlockSpec((8, D_MODEL), lambda i, j: (0, 0))]
        + [w_spec(q) for q in range(ADA_SPLIT)]
        + [pl.BlockSpec((1, 1, step), lambda i, j: (i, 0, j))],
        out_specs=pl.BlockSpec((1, 8, step), lambda i, j: (i, 0, j)),
        out_shape=jax.ShapeDtypeStruct((depth, 8, n), F32),
        compiler_params=_params("arbitrary", "arbitrary"),
        name="ada_mod",
    )(cc, *([ada_w] * ADA_SPLIT), ada_b.reshape(depth, 1, n))


def _mod_spec(row0, k):
    return pl.BlockSpec((1, 1, D_MODEL), lambda b, m: (row0 + b, 0, k))


def _const_row_spec(row):
    return pl.BlockSpec((1, 1, D_MODEL), lambda b, m: (row, 0, 0))


W_SLOTS = 2


def _weight_copies(win_hbm, wout_hbm, st_in, st_out, sem, w_idx, j, slot):
    lo = j * FF_CHUNK
    return (
        pltpu.make_async_copy(win_hbm.at[w_idx, :, pl.ds(lo, FF_CHUNK)], st_in.at[slot, 0], sem.at[slot, 0]),
        pltpu.make_async_copy(win_hbm.at[w_idx, :, pl.ds(D_FF + lo, FF_CHUNK)], st_in.at[slot, 1], sem.at[slot, 1]),
        pltpu.make_async_copy(wout_hbm.at[w_idx, pl.ds(lo, FF_CHUNK), :], st_out.at[slot], sem.at[slot, 2]),
    )


def _ffn_compute(stream, x_ref, sh_ref, sc_ref, gt_ref, g_ref, win_hbm, wout_hbm, fg_ref, mix_refs,
                 o_ref, mid_ref, win_ref, wout_ref, st_in, st_out, sem, *, final, n_mix, mix_bias, w_idx):
    n_ch = D_FF // FF_CHUNK
    copies = functools.partial(_weight_copies, win_hbm, wout_hbm, st_in, st_out, sem, w_idx)
    if stream:
        for j in range(W_SLOTS):
            for cp in copies(j, j):
                cp.start()
    x = x_ref[0]
    if n_mix:
        acts, mg_ref, mw_ref = mix_refs[:n_mix], mix_refs[n_mix], mix_refs[n_mix + 1]
        y0, r = None, 0
        for a_ref in acts:
            k = a_ref.shape[-1]
            t = _dot(a_ref[0], mw_ref[r:r + k, :])
            y0 = t if y0 is None else y0 + t
            r += k
        if mix_bias:
            y0 = y0 + mix_refs[n_mix + 2][...]
        x = x + mg_ref[0] * y0
    xb = _modnorm(x, g_ref[0], sh_ref[0], sc_ref[0])
    for j in range(n_ch):
        lo = j * FF_CHUNK
        if stream:
            slot = j % W_SLOTS
            for cp in copies(j, slot):
                cp.wait()
            win_ref[:, lo:lo + FF_CHUNK] = st_in[slot, 0].astype(BF16)
            win_ref[:, D_FF + lo:D_FF + lo + FF_CHUNK] = st_in[slot, 1].astype(BF16)
            wout_ref[lo:lo + FF_CHUNK, :] = st_out[slot].astype(BF16)
            if j + W_SLOTS < n_ch:
                for cp in copies(j + W_SLOTS, slot):
                    cp.start()
        gate = _dot(xb, win_ref[:, lo:lo + FF_CHUNK])
        up = _dot(xb, win_ref[:, D_FF + lo:D_FF + lo + FF_CHUNK])
        mid_ref[:, lo:lo + FF_CHUNK] = (_silu(gate) * up).astype(BF16)
    y = _dot(mid_ref[...], wout_ref[...])
    h = x + (0.5 * gt_ref[0]) * y
    if final:
        h = _rms(h, fg_ref[0])
    o_ref[0] = h


def _ffn_body(x_ref, sh_ref, sc_ref, gt_ref, g_ref, win_hbm, wout_hbm, fg_ref, *rest, **static):
    args = (x_ref, sh_ref, sc_ref, gt_ref, g_ref, win_hbm, wout_hbm, fg_ref, rest[:-7], *rest[-7:])
    is_first = (pl.program_id(0) == 0) & (pl.program_id(1) == 0)
    pl.when(is_first)(lambda: _ffn_compute(True, *args, **static))
    pl.when(jnp.logical_not(is_first))(lambda: _ffn_compute(False, *args, **static))


def _ffn(h, mod, row0, k0, g_all, g_row, w_in, w_out, w_idx, final_g, final, mix=None):
    nb, n, _ = h.shape
    tm = min(TM, n)
    in_specs = [
        pl.BlockSpec((1, tm, D_MODEL), lambda b, m: (b, m, 0)),
        _mod_spec(row0, k0), _mod_spec(row0, k0 + 1), _mod_spec(row0, k0 + 2),
        _const_row_spec(g_row),
        pl.BlockSpec(memory_space=pl.ANY),
        pl.BlockSpec(memory_space=pl.ANY),
        _const_row_spec(0),
    ]
    args = [h, mod, mod, mod, g_all, w_in, w_out, final_g]
    n_mix, mix_bias = 0, False
    if mix is not None:
        acts, mw, mb, mk = mix
        n_mix, mix_bias = len(acts), mb is not None
        in_specs += [pl.BlockSpec((1, tm, a.shape[-1]), lambda b, m: (b, m, 0)) for a in acts]
        in_specs += [_mod_spec(row0, mk), _resident(mw.shape, lambda b, m: (0, 0))]
        args += [*acts, mod, mw]
        if mix_bias:
            in_specs.append(pl.BlockSpec((1, D_MODEL), lambda b, m: (0, 0)))
            args.append(mb)
    return pl.pallas_call(
        functools.partial(_ffn_body, final=final, n_mix=n_mix, mix_bias=mix_bias, w_idx=w_idx),
        grid=(nb, n // tm),
        in_specs=in_specs,
        out_specs=pl.BlockSpec((1, tm, D_MODEL), lambda b, m: (b, m, 0)),
        out_shape=jax.ShapeDtypeStruct(h.shape, F32),
        scratch_shapes=[
            pltpu.VMEM((tm, D_FF), BF16),
            pltpu.VMEM((D_MODEL, 2 * D_FF), BF16),
            pltpu.VMEM((D_FF, D_MODEL), BF16),
            pltpu.VMEM((W_SLOTS, 2, D_MODEL, FF_CHUNK), F32),
            pltpu.VMEM((W_SLOTS, FF_CHUNK, D_MODEL), F32),
            pltpu.SemaphoreType.DMA((W_SLOTS, 3)),
        ],
        compiler_params=pltpu.CompilerParams(dimension_semantics=("arbitrary", "arbitrary"),
                                             vmem_limit_bytes=FFN_VMEM_LIMIT),
        name="ffn_final" if final else "ffn",
    )(*args)


def _proj_body(x_ref, sh_ref, sc_ref, g_ref, w_ref, u_ref, qkv_ref):
    xb = _modnorm(x_ref[0], g_ref[0], sh_ref[0], sc_ref[0])
    u_ref[0] = _dot(xb, w_ref[:, :2 * D_CONV])
    qkv_ref[0] = _dot(xb, w_ref[:, 2 * D_CONV:]).astype(BF16)


def _proj(h, mod, row0, k0, g_all, g_row, w):
    nb, n, _ = h.shape
    tm = min(TM, n)
    n_out = w.shape[1]
    return pl.pallas_call(
        _proj_body,
        grid=(nb, n // tm),
        in_specs=[
            pl.BlockSpec((1, tm, D_MODEL), lambda b, m: (b, m, 0)),
            _mod_spec(row0, k0), _mod_spec(row0, k0 + 1),
            _const_row_spec(g_row),
            _resident((D_MODEL, n_out), lambda b, m: (0, 0)),
        ],
        out_specs=[
            pl.BlockSpec((1, tm, 2 * D_CONV), lambda b, m: (b, m, 0)),
            pl.BlockSpec((1, tm, 3 * D_NA), lambda b, m: (b, m, 0)),
        ],
        out_shape=[
            jax.ShapeDtypeStruct((nb, n, 2 * D_CONV), F32),
            jax.ShapeDtypeStruct((nb, n, 3 * D_NA), BF16),
        ],
        compiler_params=_params("arbitrary", "arbitrary"),
        name="ab_proj",
    )(h, mod, mod, g_all, w)


def _glu(v):
    return v[:, :D_CONV] * (1.0 / (1.0 + jnp.exp(-v[:, D_CONV:])))


CONV_RC = 64
CONV_N = CONV_TL + 2 * CONV_HALO
SUBLANES = 8


def _conv_body(cur_ref, prev_ref, next_ref, w_ref, b_ref, lg_ref, lb_ref, o_ref, y_ref):
    t = pl.program_id(1)
    nt = pl.num_programs(1)
    y_ref[0, CONV_HALO:CONV_HALO + CONV_TL, :] = _glu(cur_ref[0])
    y_ref[0, 0:CONV_HALO, :] = jnp.where(t > 0, _glu(prev_ref[0]), 0.0)
    y_ref[0, CONV_HALO + CONV_TL:, :] = jnp.where(t < nt - 1, _glu(next_ref[0]), 0.0)
    for s in range(1, SUBLANES):
        y_ref[s, 0:CONV_N - SUBLANES, :] = y_ref[0, s:s + CONV_N - SUBLANES, :]
    off = CONV_HALO - CONV_WIDTH // 2
    for r in range(0, CONV_TL, CONV_RC):
        acc = jnp.zeros((CONV_RC // SUBLANES, SUBLANES, D_CONV), F32)
        for k in range(CONV_WIDTH):
            m8, s = divmod(off + k, SUBLANES)
            lo = r + SUBLANES * m8
            yk = y_ref[s, lo:lo + CONV_RC, :].reshape(CONV_RC // SUBLANES, SUBLANES, D_CONV)
            acc = acc + w_ref[k][None] * yk
        acc = acc.reshape(CONV_RC, D_CONV) + b_ref[...]
        mu = jnp.mean(acc, axis=-1, keepdims=True)
        cen = acc - mu
        var = jnp.mean(cen * cen, axis=-1, keepdims=True)
        z = cen * lax.rsqrt(var + EPS) * lg_ref[...] + lb_ref[...]
        o_ref[0, r:r + CONV_RC, :] = _silu(z).astype(BF16)


def _conv(u, w, b, ln_g, ln_b):
    nb, n, _ = u.shape
    nt = n // CONV_TL
    hb = CONV_TL // CONV_HALO
    last = n // CONV_HALO - 1
    row = lambda v: v.reshape(1, D_CONV)
    w_rep = jnp.broadcast_to(w[:, None, :], (CONV_WIDTH, SUBLANES, D_CONV))
    return pl.pallas_call(
        _conv_body,
        grid=(nb, nt),
        in_specs=[
            pl.BlockSpec((1, CONV_TL, 2 * D_CONV), lambda b_, t: (b_, t, 0)),
            pl.BlockSpec((1, CONV_HALO, 2 * D_CONV), lambda b_, t: (b_, jnp.maximum(t * hb - 1, 0), 0)),
            pl.BlockSpec((1, CONV_HALO, 2 * D_CONV), lambda b_, t: (b_, jnp.minimum((t + 1) * hb, last), 0)),
            pl.BlockSpec((CONV_WIDTH, SUBLANES, D_CONV), lambda b_, t: (0, 0, 0)),
            pl.BlockSpec((1, D_CONV), lambda b_, t: (0, 0)),
            pl.BlockSpec((1, D_CONV), lambda b_, t: (0, 0)),
            pl.BlockSpec((1, D_CONV), lambda b_, t: (0, 0)),
        ],
        out_specs=pl.BlockSpec((1, CONV_TL, D_CONV), lambda b_, t: (b_, t, 0)),
        out_shape=jax.ShapeDtypeStruct((nb, n, D_CONV), BF16),
        scratch_shapes=[pltpu.VMEM((SUBLANES, CONV_N, D_CONV), F32)],
        compiler_params=_params("arbitrary", "arbitrary"),
        name="conv_module",
    )(u, u, u, w_rep, row(b), row(ln_g), row(ln_b))


NA_HG = 4
NA_DR_PAD = 8
NA_T2 = 2 * NA_KH - 1 + 2 * NA_DR_PAD - 1


def _na_bias_table(rpb):
    qc = np.arange(GRID_W)
    cs = np.clip(qc - NA_KW // 2, 0, GRID_W - NA_KW)
    kc = np.arange(GRID_W)
    col_valid = (kc[None, :] >= cs[:, None]) & (kc[None, :] < cs[:, None] + NA_KW)
    dc = kc[None, :] - qc[:, None] + NA_KW - 1
    oh_c = np.zeros((2 * NA_KW - 1, GRID_W, GRID_W), np.float32)
    qi, ki = np.nonzero(col_valid)
    oh_c[dc[qi, ki], qi, ki] = 1.0
    t_col = jnp.einsum('hrd,dqk->hrqk', rpb, jnp.asarray(oh_c), precision=lax.Precision.HIGHEST)
    t_col = jnp.where(jnp.asarray(col_valid)[None, None], t_col, NEG)
    t_pad = jnp.pad(t_col, ((0, 0), (NA_DR_PAD, NA_DR_PAD), (0, 0), (0, 0)))
    return jnp.concatenate([t_pad[:, :NA_T2], t_pad[:, 1:NA_T2 + 1]], axis=-1)


def _na_body(q_ref, k_ref, v_ref, kc_ref, vc_ref, t2_ref, o_ref):
    blk = pl.program_id(2)
    w0 = jnp.clip(NA_ROWS * blk - NA_KH // 2, 0, GRID_W - NA_KROWS)
    start = pl.multiple_of(w0 * GRID_W, 256)
    base = w0 - NA_ROWS * blk + NA_KH - 1 + NA_DR_PAD
    q2 = q_ref[0] * jnp.asarray(HEAD_DIM ** -0.5, BF16)
    kw = k_ref[0, pl.ds(start, NA_KB), :]
    vw = v_ref[0, pl.ds(start, NA_KB), :]
    kc = kc_ref[0]
    vc = vc_ref[0]
    lane = lax.broadcasted_iota(jnp.int32, (1, NA_HG * HEAD_DIM), 1)
    klane = lax.broadcasted_iota(jnp.int32, (1, NA_KB), 1)
    row_valid = []
    for i in range(NA_ROWS):
        a_lo = jnp.clip(NA_ROWS * blk + i - NA_KH // 2, 0, GRID_W - NA_KH) - w0
        row_valid.append((klane >= a_lo * GRID_W) & (klane < (a_lo + NA_KH) * GRID_W))
    nt = (((1,), (1,)), ((), ()))
    one = jnp.ones((), BF16)
    out = None
    for j in range(NA_HG):
        in_head = (lane >= j * HEAD_DIM) & (lane < (j + 1) * HEAD_DIM)
        qm = jnp.where(in_head, q2, jnp.zeros_like(q2))
        s = lax.dot_general(qm, kw, nt, preferred_element_type=F32)
        parts = []
        for i in range(NA_ROWS):
            bias = jnp.concatenate([t2_ref[j, base + 2 * p - i] for p in range(NA_KROWS // 2)], axis=-1)
            parts.append(jnp.where(row_valid[i], s[i * GRID_W:(i + 1) * GRID_W] + bias, NEG))
        s = jnp.concatenate(parts, axis=0)
        sc = lax.dot_general(qm, kc, nt, preferred_element_type=F32)
        m = jnp.maximum(jnp.max(s, axis=-1, keepdims=True), jnp.max(sc, axis=-1, keepdims=True))
        p = jnp.exp(s - m).astype(BF16)
        pc = jnp.exp(sc - m).astype(BF16)
        o = _dot(p, jnp.where(in_head, vw, one)) + _dot(pc, jnp.where(in_head, vc, one))
        o = o * (1.0 / pltpu.roll(o, HEAD_DIM, axis=1))
        out = o if out is None else jnp.where(in_head, o, out)
    o_ref[0] = out.astype(BF16)


def _natten(qkv, qkv_c, t2):
    nb, n, _ = qkv.shape
    nctx = qkv_c.shape[1]
    ng = NA_HEADS // NA_HG
    lanes = NA_HG * HEAD_DIM
    return pl.pallas_call(
        _na_body,
        grid=(ng, nb, n // NA_QB),
        in_specs=[
            pl.BlockSpec((1, NA_QB, lanes), lambda h, b, i: (b, i, h)),
            pl.BlockSpec((1, n, lanes), lambda h, b, i: (b, 0, ng + h)),
            pl.BlockSpec((1, n, lanes), lambda h, b, i: (b, 0, 2 * ng + h)),
            pl.BlockSpec((1, nctx, lanes), lambda h, b, i: (b, 0, ng + h)),
            pl.BlockSpec((1, nctx, lanes), lambda h, b, i: (b, 0, 2 * ng + h)),
            pl.BlockSpec((NA_HG, NA_T2, GRID_W, 2 * GRID_W), lambda h, b, i: (h, 0, 0, 0)),
        ],
        out_specs=pl.BlockSpec((1, NA_QB, lanes), lambda h, b, i: (b, i, h)),
        out_shape=jax.ShapeDtypeStruct((nb, n, D_NA), BF16),
        compiler_params=_params("arbitrary", "arbitrary", "arbitrary"),
        name="natten",
    )(qkv, qkv, qkv, qkv_c, qkv_c, t2)


SEQ = GRID_W * GRID_W
FN_R = SUBLANES
FN_M = SEQ // FN_R
FN_TA = 1024
FN_LANES = 256
FN_CH = 16
LANES = 128


def _dft_tables(n):
    idx = np.arange(n, dtype=np.int64)
    ang = 2.0 * np.pi * ((idx[:, None] * idx[None, :]) % n).astype(np.float64) / n
    scale = 1.0 / np.sqrt(n)
    return (np.cos(ang) * scale).astype(np.float32), (np.sin(ang) * scale).astype(np.float32)


def _seq_tables():
    k2 = np.arange(FN_M, dtype=np.int64)
    n2 = np.arange(FN_M, dtype=np.int64)
    out = np.zeros((FN_R, 2 * FN_M, 2 * FN_M), np.float32)
    for n1 in range(FN_R):
        num = (k2[:, None] * n2[None, :] * FN_R + n1 * k2[:, None]) % SEQ
        ang = 2.0 * np.pi * num.astype(np.float64) / SEQ
        c = np.cos(ang) / np.sqrt(SEQ)
        s = np.sin(ang) / np.sqrt(SEQ)
        out[n1, :FN_M, :FN_M] = c
        out[n1, :FN_M, FN_M:] = s
        out[n1, FN_M:, :FN_M] = -s
        out[n1, FN_M:, FN_M:] = c
    return out


def _fnet_a_body(*refs):
    nx = D_MODEL // LANES
    x_refs = refs[:nx]
    sh_ref, sc_ref, g_ref, cs_ref, a_ref, b_ref = refs[nx:]
    gw = D_MODEL // FNET_GROUPS
    for s in range(FN_R):
        xs = jnp.concatenate([x[0, pl.ds(s, FN_TA // FN_R, stride=FN_R), :] for x in x_refs], axis=-1)
        xb = _modnorm(xs, g_ref[0], sh_ref[0], sc_ref[0])
        for grp in range(FNET_GROUPS):
            ab = _dot(xb[:, grp * gw:(grp + 1) * gw], cs_ref[...])
            a_ref[0, s, :, grp * gw:(grp + 1) * gw] = ab[:, :gw].astype(BF16)
            b_ref[0, s, :, grp * gw:(grp + 1) * gw] = ab[:, gw:].astype(BF16)


def _fnet_a(h, mod, row0, k0, g_all, g_row, cs):
    nb, n, _ = h.shape
    gw = D_MODEL // FNET_GROUPS
    nx = D_MODEL // LANES
    out_spec = pl.BlockSpec((1, FN_R, FN_TA // FN_R, D_MODEL), lambda b, m: (b, 0, m, 0))
    return pl.pallas_call(
        _fnet_a_body,
        grid=(nb, n // FN_TA),
        in_specs=[pl.BlockSpec((1, FN_TA, LANES), functools.partial(lambda b, m, c: (b, m, c), c=c))
                  for c in range(nx)]
        + [_mod_spec(row0, k0), _mod_spec(row0, k0 + 1), _const_row_spec(g_row),
           _resident((gw, 2 * gw), lambda b, m: (0, 0))],
        out_specs=[out_spec, out_spec],
        out_shape=[jax.ShapeDtypeStruct((nb, FN_R, n // FN_R, D_MODEL), BF16)] * 2,
        compiler_params=_params("arbitrary", "arbitrary"),
        name="fnet_channel_dft",
    )(*([h] * nx), mod, mod, g_all, cs)


def _fnet_b_body(zr_ref, zi_ref, m_ref, o_ref, v_ref):
    for n1 in range(FN_R):
        z = jnp.concatenate([zr_ref[0, n1], zi_ref[0, n1]], axis=0)
        v_ref[n1] = _dot(m_ref[n1], z)

    rt = np.float32(np.sqrt(0.5))

    def chunk(i, carry):
        r0 = pl.multiple_of(i * FN_CH, FN_CH)
        for lt in range(FN_LANES // LANES):
            ls = slice(lt * LANES, (lt + 1) * LANES)
            re = [v_ref[n, pl.ds(r0, FN_CH), ls] for n in range(FN_R)]
            im = [v_ref[n, pl.ds(FN_M + r0, FN_CH), ls] for n in range(FN_R)]
            e0 = (re[0] + re[4]) + (re[2] + re[6])
            e2 = (re[0] + re[4]) - (re[2] + re[6])
            e1 = (re[0] - re[4]) + (im[2] - im[6])
            e3 = (re[0] - re[4]) - (im[2] - im[6])
            t0r, t0i = re[1] + re[5], im[1] + im[5]
            t1r, t1i = re[1] - re[5], im[1] - im[5]
            t2r, t2i = re[3] + re[7], im[3] + im[7]
            t3r, t3i = re[3] - re[7], im[3] - im[7]
            p0 = t0r + t2r
            p2 = t0i - t2i
            al = t1r - t3r
            be = t1i + t3i
            p1 = (al + be) * rt
            p3 = (be - al) * rt
            ys = (e0 + p0, e1 + p1, e2 + p2, e3 + p3, e0 - p0, e1 - p1, e2 - p2, e3 - p3)
            for k1 in range(FN_R):
                o_ref[0, pl.ds(k1 * FN_M + r0, FN_CH), ls] = ys[k1].astype(BF16)
        return carry

    lax.fori_loop(0, FN_M // FN_CH, chunk, 0)


def _fnet_b(zr, zi, mtab):
    assert FN_R == 8
    nb = zr.shape[0]
    z_spec = pl.BlockSpec((1, FN_R, FN_M, FN_LANES), lambda b, l: (b, 0, 0, l))
    return pl.pallas_call(
        _fnet_b_body,
        grid=(nb, D_MODEL // FN_LANES),
        in_specs=[z_spec, z_spec, _resident((FN_R, 2 * FN_M, 2 * FN_M), lambda b, l: (0, 0, 0))],
        out_specs=pl.BlockSpec((1, SEQ, FN_LANES), lambda b, l: (b, 0, l)),
        out_shape=jax.ShapeDtypeStruct((nb, SEQ, D_MODEL), BF16),
        scratch_shapes=[pltpu.VMEM((FN_R, 2 * FN_M, FN_LANES), F32)],
        compiler_params=_params("arbitrary", "arbitrary"),
        name="fnet_seq_dft",
    )(zr, zi, mtab)


def kernel(x, c, ctx, c_ctx, ada_w, ada_b, norm_g, ffn_w_in, ffn_w_out, ab_w_in, conv_w, conv_b,
           conv_ln_g, conv_ln_b, na_rpb, ab_w_out, fnet_w, fnet_b, final_g):
    nb, n, d = x.shape
    depth = ada_w.shape[0]
    nctx = ctx.shape[1]
    assert (d, depth, nb) == (D_MODEL, 2, 4) and n == GRID_W * GRID_W

    cc = jnp.concatenate([c, c_ctx[None], jnp.zeros((8 - nb - 1, d), F32)], axis=0)
    mod = _ada(cc, ada_w, ada_b).reshape(depth * 8, 1, N_MOD * d)
    ctx_row = nb

    g_all = norm_g.reshape(depth * 3, 1, d)
    fg = final_g.reshape(1, 1, d)
    w_in = ffn_w_in.reshape(depth * 2, d, 2 * D_FF)
    w_out = ffn_w_out.reshape(depth * 2, D_FF, d)

    h = _ffn(x, mod, 0, 0, g_all, 0, w_in, w_out, 0, fg, False)
    hc = _ffn(ctx.reshape(1, nb * nctx, d), mod, ctx_row, 0, g_all, 0, w_in, w_out, 0, fg, False)
    w_ab = ab_w_in[0].astype(BF16)
    u, qkv = _proj(h, mod, 0, 3, g_all, 1, w_ab)
    _, qkv_c = _proj(hc, mod, ctx_row, 3, g_all, 1, w_ab)
    conv_x = _conv(u, conv_w[0], conv_b[0], conv_ln_g[0], conv_ln_b[0])
    att_x = _natten(qkv, qkv_c.reshape(nb, nctx, 3 * D_NA), _na_bias_table(na_rpb[0]))
    h = _ffn(h, mod, 0, 6, g_all, 2, w_in, w_out, 1, fg, False,
             mix=((conv_x, att_x), ab_w_out[0].astype(BF16), None, 5))

    h = _ffn(h, mod, 8, 0, g_all, 3, w_in, w_out, 2, fg, False)
    gw = d // FNET_GROUPS
    cc_tab, sc_tab = _dft_tables(gw)
    cs = jnp.asarray(np.concatenate([cc_tab, -sc_tab], axis=1)).astype(BF16)
    zr, zi = _fnet_a(h, mod, 8, 3, g_all, 4, cs)
    f = _fnet_b(zr, zi, jnp.asarray(_seq_tables()).astype(BF16))
    return _ffn(h, mod, 8, 6, g_all, 5, w_in, w_out, 3, fg, True,
                mix=((f,), fnet_w[0].astype(BF16), fnet_b[0].reshape(1, d), 5))
```

```python
import functools

import numpy as np
import jax
import jax.numpy as jnp
from jax import lax
from jax.experimental import pallas as pl
from jax.experimental.pallas import tpu as pltpu

D_MODEL = 1024
GRID_W = 64
D_CONV = 512
D_NA = 512
NA_HEADS = 8
HEAD_DIM = 64
CONV_WIDTH = 31
NA_KH = 8
NA_KW = 16
FNET_GROUPS = 4
D_FF = 2816
N_MOD = 9
EPS = 1e-6

BF16 = jnp.bfloat16
F32 = jnp.float32

VMEM_LIMIT = 56 * 1024 * 1024
FFN_VMEM_LIMIT = 60 * 1024 * 1024
TM = 1024
FF_CHUNK = 256
CONV_TL = 512
CONV_HALO = 16
NA_ROWS = 8
NA_QB = NA_ROWS * GRID_W
NA_KROWS = 16
NA_KB = NA_KROWS * GRID_W
NEG = -1e30


def _params(*sem):
    return pltpu.CompilerParams(dimension_semantics=sem, vmem_limit_bytes=VMEM_LIMIT)


def _resident(shape, index_map):
    return pl.BlockSpec(shape, index_map, pipeline_mode=pl.Buffered(1))


def _silu(x):
    return x * (1.0 / (1.0 + jnp.exp(-x)))


def _dot(a, b):
    return jnp.dot(a, b, preferred_element_type=F32)


def _rms(x, g):
    return x * lax.rsqrt(jnp.mean(x * x, axis=-1, keepdims=True) + EPS) * g


def _modnorm(x, g, shift, scale):
    return (_rms(x, g) * (1.0 + scale) + shift).astype(BF16)


ADA_TN = 768
ADA_SPLIT = 3


def _ada_body(cc_ref, *refs):
    w_refs, b_ref, o_ref = refs[:ADA_SPLIT], refs[ADA_SPLIT], refs[ADA_SPLIT + 1]
    s = _silu(cc_ref[...]).astype(BF16)
    for q, w_ref in enumerate(w_refs):
        cols = slice(q * ADA_TN, (q + 1) * ADA_TN)
        o_ref[0, :, cols] = _dot(s, w_ref[0].astype(BF16)) + b_ref[0, :, cols]


def _ada(cc, ada_w, ada_b):
    depth, _, n = ada_w.shape
    step = ADA_SPLIT * ADA_TN

    def w_spec(q):
        return pl.BlockSpec((1, D_MODEL, ADA_TN), lambda i, j: (i, 0, ADA_SPLIT * j + q))

    return pl.pallas_call(
        _ada_body,
        grid=(depth, n // step),
        in_specs=[pl.BlockSpec((8, D_MODEL), lambda i, j: (0, 0))]
        + [w_spec(q) for q in range(ADA_SPLIT)]
        + [pl.BlockSpec((1, 1, step), lambda i, j: (i, 0, j))],
        out_specs=pl.BlockSpec((1, 8, step), lambda i, j: (i, 0, j)),
        out_shape=jax.ShapeDtypeStruct((depth, 8, n), F32),
        compiler_params=_params("arbitrary", "arbitrary"),
        name="ada_mod",
    )(cc, *([ada_w] * ADA_SPLIT), ada_b.reshape(depth, 1, n))


def _mod_spec(row0, k):
    return pl.BlockSpec((1, 1, D_MODEL), lambda b, m: (row0 + b, 0, k))


def _const_row_spec(row):
    return pl.BlockSpec((1, 1, D_MODEL), lambda b, m: (row, 0, 0))


W_SLOTS = 2


def _weight_copies(win_hbm, wout_hbm, st_in, st_out, sem, w_idx, j, slot):
    lo = j * FF_CHUNK
    return (
        pltpu.make_async_copy(win_hbm.at[w_idx, :, pl.ds(lo, FF_CHUNK)], st_in.at[slot, 0], sem.at[slot, 0]),
        pltpu.make_async_copy(win_hbm.at[w_idx, :, pl.ds(D_FF + lo, FF_CHUNK)], st_in.at[slot, 1], sem.at[slot, 1]),
        pltpu.make_async_copy(wout_hbm.at[w_idx, pl.ds(lo, FF_CHUNK), :], st_out.at[slot], sem.at[slot, 2]),
    )


def _start_stream(copies):
    for j in range(W_SLOTS):
        for cp in copies(j, j):
            cp.start()


def _land_chunk(copies, j, win_ref, wout_ref, st_in, st_out):
    slot, lo = j % W_SLOTS, j * FF_CHUNK
    for cp in copies(j, slot):
        cp.wait()
    win_ref[:, lo:lo + FF_CHUNK] = st_in[slot, 0].astype(BF16)
    win_ref[:, D_FF + lo:D_FF + lo + FF_CHUNK] = st_in[slot, 1].astype(BF16)
    wout_ref[lo:lo + FF_CHUNK, :] = st_out[slot].astype(BF16)
    if j + W_SLOTS < D_FF // FF_CHUNK:
        for cp in copies(j + W_SLOTS, slot):
            cp.start()


def _ffn_compute(stream, x_ref, sh_ref, sc_ref, gt_ref, g_ref, win_hbm, wout_hbm, fg_ref, mix_refs,
                 o_ref, mid_ref, win_ref, wout_ref, st_in, st_out, sem, *, final, n_mix, mix_bias, w_idx):
    copies = functools.partial(_weight_copies, win_hbm, wout_hbm, st_in, st_out, sem, w_idx)
    if stream:
        _start_stream(copies)
    x = x_ref[0]
    if n_mix:
        acts, mg_ref, mw_ref = mix_refs[:n_mix], mix_refs[n_mix], mix_refs[n_mix + 1]
        y0, r = None, 0
        for a_ref in acts:
            k = a_ref.shape[-1]
            t = _dot(a_ref[0], mw_ref[r:r + k, :])
            y0 = t if y0 is None else y0 + t
            r += k
        if mix_bias:
            y0 = y0 + mix_refs[n_mix + 2][...]
        x = x + mg_ref[0] * y0
    xb = _modnorm(x, g_ref[0], sh_ref[0], sc_ref[0])
    for j in range(D_FF // FF_CHUNK):
        lo = j * FF_CHUNK
        if stream:
            _land_chunk(copies, j, win_ref, wout_ref, st_in, st_out)
        gate = _dot(xb, win_ref[:, lo:lo + FF_CHUNK])
        up = _dot(xb, win_ref[:, D_FF + lo:D_FF + lo + FF_CHUNK])
        mid_ref[:, lo:lo + FF_CHUNK] = (_silu(gate) * up).astype(BF16)
    y = _dot(mid_ref[...], wout_ref[...])
    h = x + (0.5 * gt_ref[0]) * y
    if final:
        h = _rms(h, fg_ref[0])
    o_ref[0] = h


def _preload_weights(win_hbm, wout_hbm, win_ref, wout_ref, st_in, st_out, sem, w_idx):
    copies = functools.partial(_weight_copies, win_hbm, wout_hbm, st_in, st_out, sem, w_idx)
    _start_stream(copies)
    for j in range(D_FF // FF_CHUNK):
        _land_chunk(copies, j, win_ref, wout_ref, st_in, st_out)


def _ffn_body(x_ref, sh_ref, sc_ref, gt_ref, g_ref, win_hbm, wout_hbm, fg_ref, *rest, **static):
    args = (x_ref, sh_ref, sc_ref, gt_ref, g_ref, win_hbm, wout_hbm, fg_ref, rest[:-7], *rest[-7:])
    is_first = (pl.program_id(0) == 0) & (pl.program_id(1) == 0)
    if static["n_mix"]:
        pl.when(is_first)(lambda: _preload_weights(win_hbm, wout_hbm, *rest[-5:], static["w_idx"]))
        _ffn_compute(False, *args, **static)
    else:
        pl.when(is_first)(lambda: _ffn_compute(True, *args, **static))
        pl.when(jnp.logical_not(is_first))(lambda: _ffn_compute(False, *args, **static))


def _ffn(h, mod, row0, k0, g_all, g_row, w_in, w_out, w_idx, final_g, final, mix=None):
    nb, n, _ = h.shape
    tm = min(TM, n)
    in_specs = [
        pl.BlockSpec((1, tm, D_MODEL), lambda b, m: (b, m, 0)),
        _mod_spec(row0, k0), _mod_spec(row0, k0 + 1), _mod_spec(row0, k0 + 2),
        _const_row_spec(g_row),
        pl.BlockSpec(memory_space=pl.ANY),
        pl.BlockSpec(memory_space=pl.ANY),
        _const_row_spec(0),
    ]
    args = [h, mod, mod, mod, g_all, w_in, w_out, final_g]
    n_mix, mix_bias = 0, False
    if mix is not None:
        acts, mw, mb, mk = mix
        n_mix, mix_bias = len(acts), mb is not None
        in_specs += [pl.BlockSpec((1, tm, a.shape[-1]), lambda b, m: (b, m, 0)) for a in acts]
        in_specs += [_mod_spec(row0, mk), _resident(mw.shape, lambda b, m: (0, 0))]
        args += [*acts, mod, mw]
        if mix_bias:
            in_specs.append(pl.BlockSpec((1, D_MODEL), lambda b, m: (0, 0)))
            args.append(mb)
    return pl.pallas_call(
        functools.partial(_ffn_body, final=final, n_mix=n_mix, mix_bias=mix_bias, w_idx=w_idx),
        grid=(nb, n // tm),
        in_specs=in_specs,
        out_specs=pl.BlockSpec((1, tm, D_MODEL), lambda b, m: (b, m, 0)),
        out_shape=jax.ShapeDtypeStruct(h.shape, F32),
        scratch_shapes=[
            pltpu.VMEM((tm, D_FF), BF16),
            pltpu.VMEM((D_MODEL, 2 * D_FF), BF16),
            pltpu.VMEM((D_FF, D_MODEL), BF16),
            pltpu.VMEM((W_SLOTS, 2, D_MODEL, FF_CHUNK), F32),
            pltpu.VMEM((W_SLOTS, FF_CHUNK, D_MODEL), F32),
            pltpu.SemaphoreType.DMA((W_SLOTS, 3)),
        ],
        compiler_params=pltpu.CompilerParams(dimension_semantics=("arbitrary", "arbitrary"),
                                             vmem_limit_bytes=FFN_VMEM_LIMIT),
        name="ffn_final" if final else "ffn",
    )(*args)


def _proj_body(x_ref, sh_ref, sc_ref, g_ref, w_ref, u_ref, qkv_ref):
    xb = _modnorm(x_ref[0], g_ref[0], sh_ref[0], sc_ref[0])
    u_ref[0] = _dot(xb, w_ref[:, :2 * D_CONV])
    qkv_ref[0] = _dot(xb, w_ref[:, 2 * D_CONV:]).astype(BF16)


def _proj(h, mod, row0, k0, g_all, g_row, w):
    nb, n, _ = h.shape
    tm = min(TM, n)
    n_out = w.shape[1]
    return pl.pallas_call(
        _proj_body,
        grid=(nb, n // tm),
        in_specs=[
            pl.BlockSpec((1, tm, D_MODEL), lambda b, m: (b, m, 0)),
            _mod_spec(row0, k0), _mod_spec(row0, k0 + 1),
            _const_row_spec(g_row),
            _resident((D_MODEL, n_out), lambda b, m: (0, 0)),
        ],
        out_specs=[
            pl.BlockSpec((1, tm, 2 * D_CONV), lambda b, m: (b, m, 0)),
            pl.BlockSpec((1, tm, 3 * D_NA), lambda b, m: (b, m, 0)),
        ],
        out_shape=[
            jax.ShapeDtypeStruct((nb, n, 2 * D_CONV), F32),
            jax.ShapeDtypeStruct((nb, n, 3 * D_NA), BF16),
        ],
        compiler_params=_params("arbitrary", "arbitrary"),
        name="ab_proj",
    )(h, mod, mod, g_all, w)


def _glu(v):
    return v[:, :D_CONV] * (1.0 / (1.0 + jnp.exp(-v[:, D_CONV:])))


CONV_RC = 64
CONV_N = CONV_TL + 2 * CONV_HALO
SUBLANES = 8


def _conv_body(cur_ref, prev_ref, next_ref, w_ref, b_ref, lg_ref, lb_ref, o_ref, y_ref):
    t = pl.program_id(1)
    nt = pl.num_programs(1)
    y_ref[0, CONV_HALO:CONV_HALO + CONV_TL, :] = _glu(cur_ref[0])
    y_ref[0, 0:CONV_HALO, :] = jnp.where(t > 0, _glu(prev_ref[0]), 0.0)
    y_ref[0, CONV_HALO + CONV_TL:, :] = jnp.where(t < nt - 1, _glu(next_ref[0]), 0.0)
    for s in range(1, SUBLANES):
        y_ref[s, 0:CONV_N - SUBLANES, :] = y_ref[0, s:s + CONV_N - SUBLANES, :]
    off = CONV_HALO - CONV_WIDTH // 2
    for r in range(0, CONV_TL, CONV_RC):
        acc = jnp.zeros((CONV_RC // SUBLANES, SUBLANES, D_CONV), F32)
        for k in range(CONV_WIDTH):
            m8, s = divmod(off + k, SUBLANES)
            lo = r + SUBLANES * m8
            yk = y_ref[s, lo:lo + CONV_RC, :].reshape(CONV_RC // SUBLANES, SUBLANES, D_CONV)
            acc = acc + w_ref[k][None] * yk
        acc = acc.reshape(CONV_RC, D_CONV) + b_ref[...]
        mu = jnp.mean(acc, axis=-1, keepdims=True)
        cen = acc - mu
        var = jnp.mean(cen * cen, axis=-1, keepdims=True)
        z = cen * lax.rsqrt(var + EPS) * lg_ref[...] + lb_ref[...]
        o_ref[0, r:r + CONV_RC, :] = _silu(z).astype(BF16)


def _conv(u, w, b, ln_g, ln_b):
    nb, n, _ = u.shape
    nt = n // CONV_TL
    hb = CONV_TL // CONV_HALO
    last = n // CONV_HALO - 1
    row = lambda v: v.reshape(1, D_CONV)
    w_rep = jnp.broadcast_to(w[:, None, :], (CONV_WIDTH, SUBLANES, D_CONV))
    return pl.pallas_call(
        _conv_body,
        grid=(nb, nt),
        in_specs=[
            pl.BlockSpec((1, CONV_TL, 2 * D_CONV), lambda b_, t: (b_, t, 0)),
            pl.BlockSpec((1, CONV_HALO, 2 * D_CONV), lambda b_, t: (b_, jnp.maximum(t * hb - 1, 0), 0)),
            pl.BlockSpec((1, CONV_HALO, 2 * D_CONV), lambda b_, t: (b_, jnp.minimum((t + 1) * hb, last), 0)),
            pl.BlockSpec((CONV_WIDTH, SUBLANES, D_CONV), lambda b_, t: (0, 0, 0)),
            pl.BlockSpec((1, D_CONV), lambda b_, t: (0, 0)),
            pl.BlockSpec((1, D_CONV), lambda b_, t: (0, 0)),
            pl.BlockSpec((1, D_CONV), lambda b_, t: (0, 0)),
        ],
        out_specs=pl.BlockSpec((1, CONV_TL, D_CONV), lambda b_, t: (b_, t, 0)),
        out_shape=jax.ShapeDtypeStruct((nb, n, D_CONV), BF16),
        scratch_shapes=[pltpu.VMEM((SUBLANES, CONV_N, D_CONV), F32)],
        compiler_params=_params("arbitrary", "arbitrary"),
        name="conv_module",
    )(u, u, u, w_rep, row(b), row(ln_g), row(ln_b))


NA_HG = 4
NA_DR_PAD = 8
NA_T2 = 2 * NA_KH - 1 + 2 * NA_DR_PAD - 1


def _na_bias_table(rpb):
    qc = np.arange(GRID_W)
    cs = np.clip(qc - NA_KW // 2, 0, GRID_W - NA_KW)
    kc = np.arange(GRID_W)
    col_valid = (kc[None, :] >= cs[:, None]) & (kc[None, :] < cs[:, None] + NA_KW)
    dc = kc[None, :] - qc[:, None] + NA_KW - 1
    oh_c = np.zeros((2 * NA_KW - 1, GRID_W, GRID_W), np.float32)
    qi, ki = np.nonzero(col_valid)
    oh_c[dc[qi, ki], qi, ki] = 1.0
    t_col = jnp.einsum('hrd,dqk->hrqk', rpb, jnp.asarray(oh_c), precision=lax.Precision.HIGHEST)
    t_col = jnp.where(jnp.asarray(col_valid)[None, None], t_col, NEG)
    t_pad = jnp.pad(t_col, ((0, 0), (NA_DR_PAD, NA_DR_PAD), (0, 0), (0, 0)))
    return jnp.concatenate([t_pad[:, :NA_T2], t_pad[:, 1:NA_T2 + 1]], axis=-1)


def _na_body(q_ref, k_ref, v_ref, kc_ref, vc_ref, t2_ref, o_ref):
    blk = pl.program_id(2)
    w0 = jnp.clip(NA_ROWS * blk - NA_KH // 2, 0, GRID_W - NA_KROWS)
    start = pl.multiple_of(w0 * GRID_W, 256)
    base = w0 - NA_ROWS * blk + NA_KH - 1 + NA_DR_PAD
    q2 = q_ref[0] * jnp.asarray(HEAD_DIM ** -0.5, BF16)
    kw = k_ref[0, pl.ds(start, NA_KB), :]
    vw = v_ref[0, pl.ds(start, NA_KB), :]
    kc = kc_ref[0]
    vc = vc_ref[0]
    lane = lax.broadcasted_iota(jnp.int32, (1, NA_HG * HEAD_DIM), 1)
    klane = lax.broadcasted_iota(jnp.int32, (1, NA_KB), 1)
    row_valid = []
    for i in range(NA_ROWS):
        a_lo = jnp.clip(NA_ROWS * blk + i - NA_KH // 2, 0, GRID_W - NA_KH) - w0
        row_valid.append((klane >= a_lo * GRID_W) & (klane < (a_lo + NA_KH) * GRID_W))
    nt = (((1,), (1,)), ((), ()))
    one = jnp.ones((), BF16)
    out = None
    for j in range(NA_HG):
        in_head = (lane >= j * HEAD_DIM) & (lane < (j + 1) * HEAD_DIM)
        qm = jnp.where(in_head, q2, jnp.zeros_like(q2))
        s = lax.dot_general(qm, kw, nt, preferred_element_type=F32)
        parts = []
        for i in range(NA_ROWS):
            bias = jnp.concatenate([t2_ref[j, base + 2 * p - i] for p in range(NA_KROWS // 2)], axis=-1)
            parts.append(jnp.where(row_valid[i], s[i * GRID_W:(i + 1) * GRID_W] + bias, NEG))
        s = jnp.concatenate(parts, axis=0)
        sc = lax.dot_general(qm, kc, nt, preferred_element_type=F32)
        m = jnp.maximum(jnp.max(s, axis=-1, keepdims=True), jnp.max(sc, axis=-1, keepdims=True))
        p = jnp.exp(s - m).astype(BF16)
        pc = jnp.exp(sc - m).astype(BF16)
        o = _dot(p, jnp.where(in_head, vw, one)) + _dot(pc, jnp.where(in_head, vc, one))
        o = o * (1.0 / pltpu.roll(o, HEAD_DIM, axis=1))
        out = o if out is None else jnp.where(in_head, o, out)
    o_ref[0] = out.astype(BF16)


def _natten(qkv, qkv_c, t2):
    nb, n, _ = qkv.shape
    nctx = qkv_c.shape[1]
    ng = NA_HEADS // NA_HG
    lanes = NA_HG * HEAD_DIM
    return pl.pallas_call(
        _na_body,
        grid=(ng, nb, n // NA_QB),
        in_specs=[
            pl.BlockSpec((1, NA_QB, lanes), lambda h, b, i: (b, i, h)),
            pl.BlockSpec((1, n, lanes), lambda h, b, i: (b, 0, ng + h)),
            pl.BlockSpec((1, n, lanes), lambda h, b, i: (b, 0, 2 * ng + h)),
            pl.BlockSpec((1, nctx, lanes), lambda h, b, i: (b, 0, ng + h)),
            pl.BlockSpec((1, nctx, lanes), lambda h, b, i: (b, 0, 2 * ng + h)),
            pl.BlockSpec((NA_HG, NA_T2, GRID_W, 2 * GRID_W), lambda h, b, i: (h, 0, 0, 0)),
        ],
        out_specs=pl.BlockSpec((1, NA_QB, lanes), lambda h, b, i: (b, i, h)),
        out_shape=jax.ShapeDtypeStruct((nb, n, D_NA), BF16),
        compiler_params=_params("arbitrary", "arbitrary", "arbitrary"),
        name="natten",
    )(qkv, qkv, qkv, qkv_c, qkv_c, t2)


SEQ = GRID_W * GRID_W
FN_R = SUBLANES
FN_M = SEQ // FN_R
FN_TA = 1024
FN_LANES = 256
FN_CH = 16
LANES = 128


def _dft_tables(n):
    idx = np.arange(n, dtype=np.int64)
    ang = 2.0 * np.pi * ((idx[:, None] * idx[None, :]) % n).astype(np.float64) / n
    scale = 1.0 / np.sqrt(n)
    return (np.cos(ang) * scale).astype(np.float32), (np.sin(ang) * scale).astype(np.float32)


def _seq_tables():
    k2 = np.arange(FN_M, dtype=np.int64)
    n2 = np.arange(FN_M, dtype=np.int64)
    out = np.zeros((FN_R, 2 * FN_M, 2 * FN_M), np.float32)
    for n1 in range(FN_R):
        num = (k2[:, None] * n2[None, :] * FN_R + n1 * k2[:, None]) % SEQ
        ang = 2.0 * np.pi * num.astype(np.float64) / SEQ
        c = np.cos(ang) / np.sqrt(SEQ)
        s = np.sin(ang) / np.sqrt(SEQ)
        out[n1, :FN_M, :FN_M] = c
        out[n1, :FN_M, FN_M:] = s
        out[n1, FN_M:, :FN_M] = -s
        out[n1, FN_M:, FN_M:] = c
    return out


def _fnet_a_body(*refs):
    nx = D_MODEL // LANES
    x_refs = refs[:nx]
    sh_ref, sc_ref, g_ref, cs_ref, a_ref, b_ref = refs[nx:]
    gw = D_MODEL // FNET_GROUPS
    for s in range(FN_R):
        xs = jnp.concatenate([x[0, pl.ds(s, FN_TA // FN_R, stride=FN_R), :] for x in x_refs], axis=-1)
        xb = _modnorm(xs, g_ref[0], sh_ref[0], sc_ref[0])
        for grp in range(FNET_GROUPS):
            ab = _dot(xb[:, grp * gw:(grp + 1) * gw], cs_ref[...])
            a_ref[0, s, :, grp * gw:(grp + 1) * gw] = ab[:, :gw].astype(BF16)
            b_ref[0, s, :, grp * gw:(grp + 1) * gw] = ab[:, gw:].astype(BF16)


def _fnet_a(h, mod, row0, k0, g_all, g_row, cs):
    nb, n, _ = h.shape
    gw = D_MODEL // FNET_GROUPS
    nx = D_MODEL // LANES
    out_spec = pl.BlockSpec((1, FN_R, FN_TA // FN_R, D_MODEL), lambda b, m: (b, 0, m, 0))
    return pl.pallas_call(
        _fnet_a_body,
        grid=(nb, n // FN_TA),
        in_specs=[pl.BlockSpec((1, FN_TA, LANES), functools.partial(lambda b, m, c: (b, m, c), c=c))
                  for c in range(nx)]
        + [_mod_spec(row0, k0), _mod_spec(row0, k0 + 1), _const_row_spec(g_row),
           _resident((gw, 2 * gw), lambda b, m: (0, 0))],
        out_specs=[out_spec, out_spec],
        out_shape=[jax.ShapeDtypeStruct((nb, FN_R, n // FN_R, D_MODEL), BF16)] * 2,
        compiler_params=_params("arbitrary", "arbitrary"),
        name="fnet_channel_dft",
    )(*([h] * nx), mod, mod, g_all, cs)


def _fnet_b_body(zr_ref, zi_ref, m_ref, o_ref, v_ref):
    for n1 in range(FN_R):
        z = jnp.concatenate([zr_ref[0, n1], zi_ref[0, n1]], axis=0)
        v_ref[n1] = _dot(m_ref[n1], z)

    rt = np.float32(np.sqrt(0.5))

    def chunk(i, carry):
        r0 = pl.multiple_of(i * FN_CH, FN_CH)
        for lt in range(FN_LANES // LANES):
            ls = slice(lt * LANES, (lt + 1) * LANES)
            re = [v_ref[n, pl.ds(r0, FN_CH), ls] for n in range(FN_R)]
            im = [v_ref[n, pl.ds(FN_M + r0, FN_CH), ls] for n in range(FN_R)]
            e0 = (re[0] + re[4]) + (re[2] + re[6])
            e2 = (re[0] + re[4]) - (re[2] + re[6])
            e1 = (re[0] - re[4]) + (im[2] - im[6])
            e3 = (re[0] - re[4]) - (im[2] - im[6])
            t0r, t0i = re[1] + re[5], im[1] + im[5]
            t1r, t1i = re[1] - re[5], im[1] - im[5]
            t2r, t2i = re[3] + re[7], im[3] + im[7]
            t3r, t3i = re[3] - re[7], im[3] - im[7]
            p0 = t0r + t2r
            p2 = t0i - t2i
            al = t1r - t3r
            be = t1i + t3i
            p1 = (al + be) * rt
            p3 = (be - al) * rt
            ys = (e0 + p0, e1 + p1, e2 + p2, e3 + p3, e0 - p0, e1 - p1, e2 - p2, e3 - p3)
            for k1 in range(FN_R):
                o_ref[0, pl.ds(k1 * FN_M + r0, FN_CH), ls] = ys[k1].astype(BF16)
        return carry

    lax.fori_loop(0, FN_M // FN_CH, chunk, 0)


def _fnet_b(zr, zi, mtab):
    assert FN_R == 8
    nb = zr.shape[0]
    z_spec = pl.BlockSpec((1, FN_R, FN_M, FN_LANES), lambda b, l: (b, 0, 0, l))
    return pl.pallas_call(
        _fnet_b_body,
        grid=(nb, D_MODEL // FN_LANES),
        in_specs=[z_spec, z_spec, _resident((FN_R, 2 * FN_M, 2 * FN_M), lambda b, l: (0, 0, 0))],
        out_specs=pl.BlockSpec((1, SEQ, FN_LANES), lambda b, l: (b, 0, l)),
        out_shape=jax.ShapeDtypeStruct((nb, SEQ, D_MODEL), BF16),
        scratch_shapes=[pltpu.VMEM((FN_R, 2 * FN_M, FN_LANES), F32)],
        compiler_params=_params("arbitrary", "arbitrary"),
        name="fnet_seq_dft",
    )(zr, zi, mtab)


def kernel(x, c, ctx, c_ctx, ada_w, ada_b, norm_g, ffn_w_in, ffn_w_out, ab_w_in, conv_w, conv_b,
           conv_ln_g, conv_ln_b, na_rpb, ab_w_out, fnet_w, fnet_b, final_g):
    nb, n, d = x.shape
    depth = ada_w.shape[0]
    nctx = ctx.shape[1]
    assert (d, depth, nb) == (D_MODEL, 2, 4) and n == GRID_W * GRID_W

    cc = jnp.concatenate([c, c_ctx[None], jnp.zeros((8 - nb - 1, d), F32)], axis=0)
    mod = _ada(cc, ada_w, ada_b).reshape(depth * 8, 1, N_MOD * d)
    ctx_row = nb

    g_all = norm_g.reshape(depth * 3, 1, d)
    fg = final_g.reshape(1, 1, d)
    w_in = ffn_w_in.reshape(depth * 2, d, 2 * D_FF)
    w_out = ffn_w_out.reshape(depth * 2, D_FF, d)

    h = _ffn(x, mod, 0, 0, g_all, 0, w_in, w_out, 0, fg, False)
    hc = _ffn(ctx.reshape(1, nb * nctx, d), mod, ctx_row, 0, g_all, 0, w_in, w_out, 0, fg, False)
    w_ab = ab_w_in[0].astype(BF16)
    u, qkv = _proj(h, mod, 0, 3, g_all, 1, w_ab)
    _, qkv_c = _proj(hc, mod, ctx_row, 3, g_all, 1, w_ab)
    conv_x = _conv(u, conv_w[0], conv_b[0], conv_ln_g[0], conv_ln_b[0])
    att_x = _natten(qkv, qkv_c.reshape(nb, nctx, 3 * D_NA), _na_bias_table(na_rpb[0]))
    h = _ffn(h, mod, 0, 6, g_all, 2, w_in, w_out, 1, fg, False,
             mix=((conv_x, att_x), ab_w_out[0].astype(BF16), None, 5))

    h = _ffn(h, mod, 8, 0, g_all, 3, w_in, w_out, 2, fg, False)
    gw = d // FNET_GROUPS
    cc_tab, sc_tab = _dft_tables(gw)
    cs = jnp.asarray(np.concatenate([cc_tab, -sc_tab], axis=1)).astype(BF16)
    zr, zi = _fnet_a(h, mod, 8, 3, g_all, 4, cs)
    f = _fnet_b(zr, zi, jnp.asarray(_seq_tables()).astype(BF16))
    return _ffn(h, mod, 8, 6, g_all, 5, w_in, w_out, 3, fg, True,
                mix=((f,), fnet_w[0].astype(BF16), fnet_b[0].reshape(1, d), 5))
```

```python
import functools

import numpy as np
import jax
import jax.numpy as jnp
from jax import lax
from jax.experimental import pallas as pl
from jax.experimental.pallas import tpu as pltpu

D_MODEL = 1024
GRID_W = 64
D_CONV = 512
D_NA = 512
NA_HEADS = 8
HEAD_DIM = 64
CONV_WIDTH = 31
NA_KH = 8
NA_KW = 16
FNET_GROUPS = 4
D_FF = 2816
N_MOD = 9
EPS = 1e-6

BF16 = jnp.bfloat16
F32 = jnp.float32

VMEM_LIMIT = 56 * 1024 * 1024
FFN_VMEM_LIMIT = 60 * 1024 * 1024
SUBLANES = 8
LANES = 128
TM = 1024
FF_CHUNK = 256
CONV_TL = 512
CONV_HALO = 16
NA_ROWS = 8
NA_QB = NA_ROWS * GRID_W
NA_KROWS = 16
NA_KB = NA_KROWS * GRID_W
NEG = -1e30


def _params(*sem):
    return pltpu.CompilerParams(dimension_semantics=sem, vmem_limit_bytes=VMEM_LIMIT)


def _resident(shape, index_map):
    return pl.BlockSpec(shape, index_map, pipeline_mode=pl.Buffered(1))


def _silu(x):
    return x * (1.0 / (1.0 + jnp.exp(-x)))


def _dot(a, b):
    return jnp.dot(a, b, preferred_element_type=F32)


def _rms(x, g):
    return x * lax.rsqrt(jnp.mean(x * x, axis=-1, keepdims=True) + EPS) * g


def _modnorm(x, g, shift, scale):
    return (_rms(x, g) * (1.0 + scale) + shift).astype(BF16)


ADA_TN = 768
ADA_SPLIT = 3


def _ada_body(cc_ref, *refs):
    w_refs, b_ref, o_ref = refs[:ADA_SPLIT], refs[ADA_SPLIT], refs[ADA_SPLIT + 1]
    s = _silu(cc_ref[...]).astype(BF16)
    for q, w_ref in enumerate(w_refs):
        cols = slice(q * ADA_TN, (q + 1) * ADA_TN)
        o_ref[0, :, cols] = _dot(s, w_ref[0].astype(BF16)) + b_ref[0, :, cols]


def _ada(cc, ada_w, ada_b):
    depth, _, n = ada_w.shape
    step = ADA_SPLIT * ADA_TN

    def w_spec(q):
        return pl.BlockSpec((1, D_MODEL, ADA_TN), lambda i, j: (i, 0, ADA_SPLIT * j + q))

    return pl.pallas_call(
        _ada_body,
        grid=(depth, n // step),
        in_specs=[pl.BlockSpec((8, D_MODEL), lambda i, j: (0, 0))]
        + [w_spec(q) for q in range(ADA_SPLIT)]
        + [pl.BlockSpec((1, 1, step), lambda i, j: (i, 0, j))],
        out_specs=pl.BlockSpec((1, 8, step), lambda i, j: (i, 0, j)),
        out_shape=jax.ShapeDtypeStruct((depth, 8, n), F32),
        compiler_params=_params("arbitrary", "arbitrary"),
        name="ada_mod",
    )(cc, *([ada_w] * ADA_SPLIT), ada_b.reshape(depth, 1, n))


def _mod_spec(row0, k):
    return pl.BlockSpec((1, 1, D_MODEL), lambda b, m: (row0 + b, 0, k))


def _const_row_spec(row):
    return pl.BlockSpec((1, 1, D_MODEL), lambda b, m: (row, 0, 0))


W_SLOTS = 2


def _weight_copies(win_hbm, wout_hbm, st_in, st_out, sem, w_idx, j, slot):
    lo = j * FF_CHUNK
    return (
        pltpu.make_async_copy(win_hbm.at[w_idx, :, pl.ds(lo, FF_CHUNK)], st_in.at[slot, 0], sem.at[slot, 0]),
        pltpu.make_async_copy(win_hbm.at[w_idx, :, pl.ds(D_FF + lo, FF_CHUNK)], st_in.at[slot, 1], sem.at[slot, 1]),
        pltpu.make_async_copy(wout_hbm.at[w_idx, pl.ds(lo, FF_CHUNK), :], st_out.at[slot], sem.at[slot, 2]),
    )


def _start_stream(copies):
    for j in range(W_SLOTS):
        for cp in copies(j, j):
            cp.start()


def _land_chunk(copies, j, win_ref, wout_ref, st_in, st_out):
    slot, lo = j % W_SLOTS, j * FF_CHUNK
    for cp in copies(j, slot):
        cp.wait()
    win_ref[:, lo:lo + FF_CHUNK] = st_in[slot, 0].astype(BF16)
    win_ref[:, D_FF + lo:D_FF + lo + FF_CHUNK] = st_in[slot, 1].astype(BF16)
    wout_ref[lo:lo + FF_CHUNK, :] = st_out[slot].astype(BF16)
    if j + W_SLOTS < D_FF // FF_CHUNK:
        for cp in copies(j + W_SLOTS, slot):
            cp.start()


def _ffn_compute(stream, x_ref, sh_ref, sc_ref, gt_ref, g_ref, win_hbm, wout_hbm, fg_ref, mix_refs,
                 o_ref, mid_ref, win_ref, wout_ref, st_in, st_out, sem, *, final, n_mix, mix_bias, w_idx):
    copies = functools.partial(_weight_copies, win_hbm, wout_hbm, st_in, st_out, sem, w_idx)
    if stream:
        _start_stream(copies)
    x = x_ref[0]
    if n_mix:
        acts, mg_ref, mw_ref = mix_refs[:n_mix], mix_refs[n_mix], mix_refs[n_mix + 1]
        y0, r = None, 0
        for a_ref in acts:
            k = a_ref.shape[-1]
            t = _dot(a_ref[0], mw_ref[r:r + k, :])
            y0 = t if y0 is None else y0 + t
            r += k
        if mix_bias:
            y0 = y0 + mix_refs[n_mix + 2][...]
        x = x + mg_ref[0] * y0
    xb = _modnorm(x, g_ref[0], sh_ref[0], sc_ref[0])
    for j in range(D_FF // FF_CHUNK):
        lo = j * FF_CHUNK
        if stream:
            _land_chunk(copies, j, win_ref, wout_ref, st_in, st_out)
        gate = _dot(xb, win_ref[:, lo:lo + FF_CHUNK])
        up = _dot(xb, win_ref[:, D_FF + lo:D_FF + lo + FF_CHUNK])
        mid_ref[:, lo:lo + FF_CHUNK] = (_silu(gate) * up).astype(BF16)
    y = _dot(mid_ref[...], wout_ref[...])
    h = x + (0.5 * gt_ref[0]) * y
    if final:
        h = _rms(h, fg_ref[0])
    o_ref[0] = h


def _preload_weights(win_hbm, wout_hbm, win_ref, wout_ref, st_in, st_out, sem, w_idx):
    copies = functools.partial(_weight_copies, win_hbm, wout_hbm, st_in, st_out, sem, w_idx)
    _start_stream(copies)
    for j in range(D_FF // FF_CHUNK):
        _land_chunk(copies, j, win_ref, wout_ref, st_in, st_out)


def _ffn_body(x_ref, sh_ref, sc_ref, gt_ref, g_ref, win_hbm, wout_hbm, fg_ref, *rest, **static):
    args = (x_ref, sh_ref, sc_ref, gt_ref, g_ref, win_hbm, wout_hbm, fg_ref, rest[:-7], *rest[-7:])
    is_first = (pl.program_id(0) == 0) & (pl.program_id(1) == 0)
    if static["n_mix"]:
        pl.when(is_first)(lambda: _preload_weights(win_hbm, wout_hbm, *rest[-5:], static["w_idx"]))
        _ffn_compute(False, *args, **static)
    else:
        pl.when(is_first)(lambda: _ffn_compute(True, *args, **static))
        pl.when(jnp.logical_not(is_first))(lambda: _ffn_compute(False, *args, **static))


def _ffn(h, mod, row0, k0, g_all, g_row, w_in, w_out, w_idx, final_g, final, mix=None):
    nb, n, _ = h.shape
    tm = min(TM, n)
    in_specs = [
        pl.BlockSpec((1, tm, D_MODEL), lambda b, m: (b, m, 0)),
        _mod_spec(row0, k0), _mod_spec(row0, k0 + 1), _mod_spec(row0, k0 + 2),
        _const_row_spec(g_row),
        pl.BlockSpec(memory_space=pl.ANY),
        pl.BlockSpec(memory_space=pl.ANY),
        _const_row_spec(0),
    ]
    args = [h, mod, mod, mod, g_all, w_in, w_out, final_g]
    n_mix, mix_bias = 0, False
    if mix is not None:
        acts, mw, mb, mk = mix
        n_mix, mix_bias = len(acts), mb is not None
        in_specs += [pl.BlockSpec((1, tm, a.shape[-1]), lambda b, m: (b, m, 0)) for a in acts]
        in_specs += [_mod_spec(row0, mk), _resident(mw.shape, lambda b, m: (0, 0))]
        args += [*acts, mod, mw]
        if mix_bias:
            in_specs.append(pl.BlockSpec((1, D_MODEL), lambda b, m: (0, 0)))
            args.append(mb)
    return pl.pallas_call(
        functools.partial(_ffn_body, final=final, n_mix=n_mix, mix_bias=mix_bias, w_idx=w_idx),
        grid=(nb, n // tm),
        in_specs=in_specs,
        out_specs=pl.BlockSpec((1, tm, D_MODEL), lambda b, m: (b, m, 0)),
        out_shape=jax.ShapeDtypeStruct(h.shape, F32),
        scratch_shapes=[
            pltpu.VMEM((tm, D_FF), BF16),
            pltpu.VMEM((D_MODEL, 2 * D_FF), BF16),
            pltpu.VMEM((D_FF, D_MODEL), BF16),
            pltpu.VMEM((W_SLOTS, 2, D_MODEL, FF_CHUNK), F32),
            pltpu.VMEM((W_SLOTS, FF_CHUNK, D_MODEL), F32),
            pltpu.SemaphoreType.DMA((W_SLOTS, 3)),
        ],
        compiler_params=pltpu.CompilerParams(dimension_semantics=("arbitrary", "arbitrary"),
                                             vmem_limit_bytes=FFN_VMEM_LIMIT),
        name="ffn_final" if final else "ffn",
    )(*args)


def _qkv_body(x_ref, sh_ref, sc_ref, g_ref, w_ref, qkv_ref):
    xb = _modnorm(x_ref[0], g_ref[0], sh_ref[0], sc_ref[0])
    qkv_ref[0] = _dot(xb, w_ref[:, 2 * D_CONV:]).astype(BF16)


def _qkv_proj(h, mod, row0, k0, g_all, g_row, w):
    nb, n, _ = h.shape
    tm = min(TM, n)
    return pl.pallas_call(
        _qkv_body,
        grid=(nb, n // tm),
        in_specs=[
            pl.BlockSpec((1, tm, D_MODEL), lambda b, m: (b, m, 0)),
            _mod_spec(row0, k0), _mod_spec(row0, k0 + 1),
            _const_row_spec(g_row),
            _resident(w.shape, lambda b, m: (0, 0)),
        ],
        out_specs=pl.BlockSpec((1, tm, 3 * D_NA), lambda b, m: (b, m, 0)),
        out_shape=jax.ShapeDtypeStruct((nb, n, 3 * D_NA), BF16),
        compiler_params=_params("arbitrary", "arbitrary"),
        name="ctx_qkv",
    )(h, mod, mod, g_all, w)


def _glu(v):
    return v[:, :D_CONV] * (1.0 / (1.0 + jnp.exp(-v[:, D_CONV:])))


CONV_RC = 64
CONV_N = CONV_TL + 2 * CONV_HALO


PC_ROWS = TM + 2 * CONV_HALO


def _conv_rows(yp_ref, base, w_ref, b_ref, lg_ref, lb_ref, o_ref, y_ref):
    y_ref[0] = yp_ref[base:base + CONV_N, :]
    for s in range(1, SUBLANES):
        y_ref[s, 0:CONV_N - SUBLANES, :] = y_ref[0, s:s + CONV_N - SUBLANES, :]
    off = CONV_HALO - CONV_WIDTH // 2
    for r in range(0, CONV_TL, CONV_RC):
        acc = jnp.zeros((CONV_RC // SUBLANES, SUBLANES, D_CONV), F32)
        for k in range(CONV_WIDTH):
            m8, s = divmod(off + k, SUBLANES)
            lo = r + SUBLANES * m8
            yk = y_ref[s, lo:lo + CONV_RC, :].reshape(CONV_RC // SUBLANES, SUBLANES, D_CONV)
            acc = acc + w_ref[k][None] * yk
        acc = acc.reshape(CONV_RC, D_CONV) + b_ref[...]
        mu = jnp.mean(acc, axis=-1, keepdims=True)
        cen = acc - mu
        var = jnp.mean(cen * cen, axis=-1, keepdims=True)
        z = cen * lax.rsqrt(var + EPS) * lg_ref[...] + lb_ref[...]
        o_ref[0, base + r:base + r + CONV_RC, :] = _silu(z).astype(BF16)


def _proj_conv_body(x_ref, sh_ref, sc_ref, g_ref, w_ref, cw_ref, cb_ref, lg_ref, lb_ref,
                    qkv_ref, cx_ref, yp_ref, y_ref):
    t = pl.program_id(1)
    nt = pl.num_programs(1) - 1

    def proj():
        xb = _modnorm(x_ref[0], g_ref[0], sh_ref[0], sc_ref[0])
        qkv_ref[0] = _dot(xb, w_ref[:, 2 * D_CONV:]).astype(BF16)
        return _glu(_dot(xb, w_ref[:, :2 * D_CONV]))

    def conv():
        for base in range(0, TM, CONV_TL):
            _conv_rows(yp_ref, base, cw_ref, cb_ref, lg_ref, lb_ref, cx_ref, y_ref)

    @pl.when(t == 0)
    def _():
        yp_ref[0:CONV_HALO, :] = jnp.zeros((CONV_HALO, D_CONV), F32)
        yp_ref[CONV_HALO:CONV_HALO + TM, :] = proj()

    @pl.when((t > 0) & (t < nt))
    def _():
        xh = _modnorm(x_ref[0, 0:CONV_HALO, :], g_ref[0], sh_ref[0], sc_ref[0])
        yp_ref[CONV_HALO + TM:, :] = _glu(_dot(xh, w_ref[:, :2 * D_CONV]))
        conv()
        y = proj()
        yp_ref[0:CONV_HALO, :] = yp_ref[TM:TM + CONV_HALO, :]
        yp_ref[CONV_HALO:CONV_HALO + TM, :] = y

    @pl.when(t == nt)
    def _():
        yp_ref[CONV_HALO + TM:, :] = jnp.zeros((CONV_HALO, D_CONV), F32)
        conv()


def _proj_conv(h, mod, row0, k0, g_all, g_row, w, cw, cb, ln_g, ln_b):
    nb, n, _ = h.shape
    nt = n // TM
    row = lambda v: v.reshape(1, D_CONV)
    w_rep = jnp.broadcast_to(cw[:, None, :], (CONV_WIDTH, SUBLANES, D_CONV))
    const2 = lambda b, t: (0, 0)
    return pl.pallas_call(
        _proj_conv_body,
        grid=(nb, nt + 1),
        in_specs=[
            pl.BlockSpec((1, TM, D_MODEL), lambda b, t: (b, jnp.minimum(t, nt - 1), 0)),
            _mod_spec(row0, k0), _mod_spec(row0, k0 + 1), _const_row_spec(g_row),
            _resident(w.shape, const2),
            pl.BlockSpec((CONV_WIDTH, SUBLANES, D_CONV), lambda b, t: (0, 0, 0)),
            pl.BlockSpec((1, D_CONV), const2), pl.BlockSpec((1, D_CONV), const2), pl.BlockSpec((1, D_CONV), const2),
        ],
        out_specs=[
            pl.BlockSpec((1, TM, 3 * D_NA), lambda b, t: (b, jnp.minimum(t, nt - 1), 0)),
            pl.BlockSpec((1, TM, D_CONV), lambda b, t: (b, jnp.maximum(t - 1, 0), 0)),
        ],
        out_shape=[
            jax.ShapeDtypeStruct((nb, n, 3 * D_NA), BF16),
            jax.ShapeDtypeStruct((nb, n, D_CONV), BF16),
        ],
        scratch_shapes=[pltpu.VMEM((PC_ROWS, D_CONV), F32), pltpu.VMEM((SUBLANES, CONV_N, D_CONV), F32)],
        compiler_params=_params("arbitrary", "arbitrary"),
        name="ab_proj_conv",
    )(h, mod, mod, g_all, w, w_rep, row(cb), row(ln_g), row(ln_b))


NA_HG = 4
NA_DR_PAD = 8
NA_T2 = 2 * NA_KH - 1 + 2 * NA_DR_PAD - 1


def _na_bias_table(rpb):
    qc = np.arange(GRID_W)
    cs = np.clip(qc - NA_KW // 2, 0, GRID_W - NA_KW)
    kc = np.arange(GRID_W)
    col_valid = (kc[None, :] >= cs[:, None]) & (kc[None, :] < cs[:, None] + NA_KW)
    dc = kc[None, :] - qc[:, None] + NA_KW - 1
    oh_c = np.zeros((2 * NA_KW - 1, GRID_W, GRID_W), np.float32)
    qi, ki = np.nonzero(col_valid)
    oh_c[dc[qi, ki], qi, ki] = 1.0
    t_col = jnp.einsum('hrd,dqk->hrqk', rpb, jnp.asarray(oh_c), precision=lax.Precision.HIGHEST)
    t_col = jnp.where(jnp.asarray(col_valid)[None, None], t_col, NEG)
    t_pad = jnp.pad(t_col, ((0, 0), (NA_DR_PAD, NA_DR_PAD), (0, 0), (0, 0)))
    return jnp.concatenate([t_pad[:, :NA_T2], t_pad[:, 1:NA_T2 + 1]], axis=-1)


def _na_probs_clamped(t2_ref, base, row_valid, j, s, sc):
    parts = []
    for i in range(NA_ROWS):
        bias = jnp.concatenate([t2_ref[j, base + 2 * p - i] for p in range(NA_KROWS // 2)], axis=-1)
        parts.append(jnp.where(row_valid[i], s[i * GRID_W:(i + 1) * GRID_W] + bias, NEG))
    s = jnp.concatenate(parts, axis=0)
    m = jnp.maximum(jnp.max(s, axis=-1, keepdims=True), jnp.max(sc, axis=-1, keepdims=True))
    return jnp.exp(s - m).astype(BF16), jnp.exp(sc - m).astype(BF16)


def _na_probs_interior(t2_ref, j, s, sc):
    base = NA_KH - 1 + NA_DR_PAD - NA_KH // 2
    plane = lax.broadcasted_iota(jnp.int32, (1, LANES), 1)
    zero = jnp.zeros((GRID_W, LANES), BF16)
    p_rows, pc_rows = [], []
    for i in range(NA_ROWS):
        rows = slice(i * GRID_W, (i + 1) * GRID_W)
        pieces = {}
        for p in range(NA_KROWS // 2):
            first, second = i <= 2 * p < i + NA_KH, i <= 2 * p + 1 < i + NA_KH
            if not (first or second):
                continue
            piece = s[rows, p * LANES:(p + 1) * LANES] + t2_ref[j, base + 2 * p - i]
            if not (first and second):
                piece = jnp.where((plane < GRID_W) if first else (plane >= GRID_W), piece, NEG)
            pieces[p] = piece
        sc_i = sc[rows]
        m = jnp.max(sc_i, axis=-1, keepdims=True)
        for piece in pieces.values():
            m = jnp.maximum(m, jnp.max(piece, axis=-1, keepdims=True))
        p_rows.append(jnp.concatenate(
            [jnp.exp(pieces[p] - m).astype(BF16) if p in pieces else zero for p in range(NA_KROWS // 2)], axis=-1))
        pc_rows.append(jnp.exp(sc_i - m).astype(BF16))
    return jnp.concatenate(p_rows, axis=0), jnp.concatenate(pc_rows, axis=0)


def _na_body(q_ref, k_ref, v_ref, kc_ref, vc_ref, t2_ref, o_ref):
    blk = pl.program_id(2)
    w0 = jnp.clip(NA_ROWS * blk - NA_KH // 2, 0, GRID_W - NA_KROWS)
    start = pl.multiple_of(w0 * GRID_W, 256)
    q2 = q_ref[0] * jnp.asarray(HEAD_DIM ** -0.5, BF16)
    kw = k_ref[0, pl.ds(start, NA_KB), :]
    vw = v_ref[0, pl.ds(start, NA_KB), :]
    kc = kc_ref[0]
    vc = vc_ref[0]
    lane = lax.broadcasted_iota(jnp.int32, (1, NA_HG * HEAD_DIM), 1)
    nt = (((1,), (1,)), ((), ()))
    one = jnp.ones((), BF16)

    def run(probs_fn):
        out = None
        for j in range(NA_HG):
            in_head = (lane >= j * HEAD_DIM) & (lane < (j + 1) * HEAD_DIM)
            qm = jnp.where(in_head, q2, jnp.zeros_like(q2))
            s = lax.dot_general(qm, kw, nt, preferred_element_type=F32)
            sc = lax.dot_general(qm, kc, nt, preferred_element_type=F32)
            p, pc = probs_fn(j, s, sc)
            o = _dot(p, jnp.where(in_head, vw, one)) + _dot(pc, jnp.where(in_head, vc, one))
            o = o * (1.0 / pltpu.roll(o, HEAD_DIM, axis=1))
            out = o if out is None else jnp.where(in_head, o, out)
        o_ref[0] = out.astype(BF16)

    interior = (blk > 0) & (blk < pl.num_programs(2) - 1)

    @pl.when(interior)
    def _():
        run(functools.partial(_na_probs_interior, t2_ref))

    @pl.when(jnp.logical_not(interior))
    def _():
        base = w0 - NA_ROWS * blk + NA_KH - 1 + NA_DR_PAD
        klane = lax.broadcasted_iota(jnp.int32, (1, NA_KB), 1)
        row_valid = []
        for i in range(NA_ROWS):
            a_lo = jnp.clip(NA_ROWS * blk + i - NA_KH // 2, 0, GRID_W - NA_KH) - w0
            row_valid.append((klane >= a_lo * GRID_W) & (klane < (a_lo + NA_KH) * GRID_W))
        run(functools.partial(_na_probs_clamped, t2_ref, base, row_valid))


def _natten(qkv, qkv_c, t2):
    nb, n, _ = qkv.shape
    nctx = qkv_c.shape[1]
    ng = NA_HEADS // NA_HG
    lanes = NA_HG * HEAD_DIM
    return pl.pallas_call(
        _na_body,
        grid=(ng, nb, n // NA_QB),
        in_specs=[
            pl.BlockSpec((1, NA_QB, lanes), lambda h, b, i: (b, i, h)),
            pl.BlockSpec((1, n, lanes), lambda h, b, i: (b, 0, ng + h)),
            pl.BlockSpec((1, n, lanes), lambda h, b, i: (b, 0, 2 * ng + h)),
            pl.BlockSpec((1, nctx, lanes), lambda h, b, i: (b, 0, ng + h)),
            pl.BlockSpec((1, nctx, lanes), lambda h, b, i: (b, 0, 2 * ng + h)),
            pl.BlockSpec((NA_HG, NA_T2, GRID_W, 2 * GRID_W), lambda h, b, i: (h, 0, 0, 0)),
        ],
        out_specs=pl.BlockSpec((1, NA_QB, lanes), lambda h, b, i: (b, i, h)),
        out_shape=jax.ShapeDtypeStruct((nb, n, D_NA), BF16),
        compiler_params=_params("arbitrary", "arbitrary", "arbitrary"),
        name="natten",
    )(qkv, qkv, qkv, qkv_c, qkv_c, t2)


SEQ = GRID_W * GRID_W
FN_R = SUBLANES
FN_M = SEQ // FN_R
FN_TA = 1024
FN_LANES = 256
FN_CH = 16


def _dft_tables(n):
    idx = np.arange(n, dtype=np.int64)
    ang = 2.0 * np.pi * ((idx[:, None] * idx[None, :]) % n).astype(np.float64) / n
    scale = 1.0 / np.sqrt(n)
    return (np.cos(ang) * scale).astype(np.float32), (np.sin(ang) * scale).astype(np.float32)


def _seq_tables():
    k2 = np.arange(FN_M, dtype=np.int64)
    n2 = np.arange(FN_M, dtype=np.int64)
    out = np.zeros((FN_R, 2 * FN_M, 2 * FN_M), np.float32)
    for n1 in range(FN_R):
        num = (k2[:, None] * n2[None, :] * FN_R + n1 * k2[:, None]) % SEQ
        ang = 2.0 * np.pi * num.astype(np.float64) / SEQ
        c = np.cos(ang) / np.sqrt(SEQ)
        s = np.sin(ang) / np.sqrt(SEQ)
        out[n1, :FN_M, :FN_M] = c
        out[n1, :FN_M, FN_M:] = s
        out[n1, FN_M:, :FN_M] = -s
        out[n1, FN_M:, FN_M:] = c
    return out


def _fnet_a_body(*refs):
    nx = D_MODEL // LANES
    x_refs = refs[:nx]
    sh_ref, sc_ref, g_ref, cs_ref, a_ref, b_ref = refs[nx:]
    gw = D_MODEL // FNET_GROUPS
    for s in range(FN_R):
        xs = jnp.concatenate([x[0, pl.ds(s, FN_TA // FN_R, stride=FN_R), :] for x in x_refs], axis=-1)
        xb = _modnorm(xs, g_ref[0], sh_ref[0], sc_ref[0])
        for grp in range(FNET_GROUPS):
            ab = _dot(xb[:, grp * gw:(grp + 1) * gw], cs_ref[...])
            a_ref[0, s, :, grp * gw:(grp + 1) * gw] = ab[:, :gw].astype(BF16)
            b_ref[0, s, :, grp * gw:(grp + 1) * gw] = ab[:, gw:].astype(BF16)


def _fnet_a(h, mod, row0, k0, g_all, g_row, cs):
    nb, n, _ = h.shape
    gw = D_MODEL // FNET_GROUPS
    nx = D_MODEL // LANES
    out_spec = pl.BlockSpec((1, FN_R, FN_TA // FN_R, D_MODEL), lambda b, m: (b, 0, m, 0))
    return pl.pallas_call(
        _fnet_a_body,
        grid=(nb, n // FN_TA),
        in_specs=[pl.BlockSpec((1, FN_TA, LANES), functools.partial(lambda b, m, c: (b, m, c), c=c))
                  for c in range(nx)]
        + [_mod_spec(row0, k0), _mod_spec(row0, k0 + 1), _const_row_spec(g_row),
           _resident((gw, 2 * gw), lambda b, m: (0, 0))],
        out_specs=[out_spec, out_spec],
        out_shape=[jax.ShapeDtypeStruct((nb, FN_R, n // FN_R, D_MODEL), BF16)] * 2,
        compiler_params=_params("arbitrary", "arbitrary"),
        name="fnet_channel_dft",
    )(*([h] * nx), mod, mod, g_all, cs)


def _fnet_b_body(zr_ref, zi_ref, m_ref, o_ref, v_ref):
    for n1 in range(FN_R):
        z = jnp.concatenate([zr_ref[0, n1], zi_ref[0, n1]], axis=0)
        v_ref[n1] = _dot(m_ref[n1], z)

    rt = np.float32(np.sqrt(0.5))

    def chunk(i, carry):
        r0 = pl.multiple_of(i * FN_CH, FN_CH)
        for lt in range(FN_LANES // LANES):
            ls = slice(lt * LANES, (lt + 1) * LANES)
            re = [v_ref[n, pl.ds(r0, FN_CH), ls] for n in range(FN_R)]
            im = [v_ref[n, pl.ds(FN_M + r0, FN_CH), ls] for n in range(FN_R)]
            e0 = (re[0] + re[4]) + (re[2] + re[6])
            e2 = (re[0] + re[4]) - (re[2] + re[6])
            e1 = (re[0] - re[4]) + (im[2] - im[6])
            e3 = (re[0] - re[4]) - (im[2] - im[6])
            t0r, t0i = re[1] + re[5], im[1] + im[5]
            t1r, t1i = re[1] - re[5], im[1] - im[5]
            t2r, t2i = re[3] + re[7], im[3] + im[7]
            t3r, t3i = re[3] - re[7], im[3] - im[7]
            p0 = t0r + t2r
            p2 = t0i - t2i
            al = t1r - t3r
            be = t1i + t3i
            p1 = (al + be) * rt
            p3 = (be - al) * rt
            ys = (e0 + p0, e1 + p1, e2 + p2, e3 + p3, e0 - p0, e1 - p1, e2 - p2, e3 - p3)
            for k1 in range(FN_R):
                o_ref[0, pl.ds(k1 * FN_M + r0, FN_CH), ls] = ys[k1].astype(BF16)
        return carry

    lax.fori_loop(0, FN_M // FN_CH, chunk, 0)


def _fnet_b(zr, zi, mtab):
    assert FN_R == 8
    nb = zr.shape[0]
    z_spec = pl.BlockSpec((1, FN_R, FN_M, FN_LANES), lambda b, l: (b, 0, 0, l))
    return pl.pallas_call(
        _fnet_b_body,
        grid=(nb, D_MODEL // FN_LANES),
        in_specs=[z_spec, z_spec, _resident((FN_R, 2 * FN_M, 2 * FN_M), lambda b, l: (0, 0, 0))],
        out_specs=pl.BlockSpec((1, SEQ, FN_LANES), lambda b, l: (b, 0, l)),
        out_shape=jax.ShapeDtypeStruct((nb, SEQ, D_MODEL), BF16),
        scratch_shapes=[pltpu.VMEM((FN_R, 2 * FN_M, FN_LANES), F32)],
        compiler_params=_params("arbitrary", "arbitrary"),
        name="fnet_seq_dft",
    )(zr, zi, mtab)


def kernel(x, c, ctx, c_ctx, ada_w, ada_b, norm_g, ffn_w_in, ffn_w_out, ab_w_in, conv_w, conv_b,
           conv_ln_g, conv_ln_b, na_rpb, ab_w_out, fnet_w, fnet_b, final_g):
    nb, n, d = x.shape
    depth = ada_w.shape[0]
    nctx = ctx.shape[1]
    assert (d, depth, nb) == (D_MODEL, 2, 4) and n == GRID_W * GRID_W

    cc = jnp.concatenate([c, c_ctx[None], jnp.zeros((8 - nb - 1, d), F32)], axis=0)
    mod = _ada(cc, ada_w, ada_b).reshape(depth * 8, 1, N_MOD * d)
    ctx_row = nb

    g_all = norm_g.reshape(depth * 3, 1, d)
    fg = final_g.reshape(1, 1, d)
    w_in = ffn_w_in.reshape(depth * 2, d, 2 * D_FF)
    w_out = ffn_w_out.reshape(depth * 2, D_FF, d)

    h = _ffn(x, mod, 0, 0, g_all, 0, w_in, w_out, 0, fg, False)
    hc = _ffn(ctx.reshape(1, nb * nctx, d), mod, ctx_row, 0, g_all, 0, w_in, w_out, 0, fg, False)
    w_ab = ab_w_in[0].astype(BF16)
    qkv, conv_x = _proj_conv(h, mod, 0, 3, g_all, 1, w_ab, conv_w[0], conv_b[0], conv_ln_g[0], conv_ln_b[0])
    qkv_c = _qkv_proj(hc, mod, ctx_row, 3, g_all, 1, w_ab)
    att_x = _natten(qkv, qkv_c.reshape(nb, nctx, 3 * D_NA), _na_bias_table(na_rpb[0]))
    h = _ffn(h, mod, 0, 6, g_all, 2, w_in, w_out, 1, fg, False,
             mix=((conv_x, att_x), ab_w_out[0].astype(BF16), None, 5))

    h = _ffn(h, mod, 8, 0, g_all, 3, w_in, w_out, 2, fg, False)
    gw = d // FNET_GROUPS
    cc_tab, sc_tab = _dft_tables(gw)
    cs = jnp.asarray(np.concatenate([cc_tab, -sc_tab], axis=1)).astype(BF16)
    zr, zi = _fnet_a(h, mod, 8, 3, g_all, 4, cs)
    f = _fnet_b(zr, zi, jnp.asarray(_seq_tables()).astype(BF16))
    return _ffn(h, mod, 8, 6, g_all, 5, w_in, w_out, 3, fg, True,
                mix=((f,), fnet_w[0].astype(BF16), fnet_b[0].reshape(1, d), 5))
```

```python
import functools

import numpy as np
import jax
import jax.numpy as jnp
from jax import lax
from jax.experimental import pallas as pl
from jax.experimental.pallas import tpu as pltpu

D_MODEL = 1024
GRID_W = 64
D_CONV = 512
D_NA = 512
NA_HEADS = 8
HEAD_DIM = 64
CONV_WIDTH = 31
NA_KH = 8
NA_KW = 16
FNET_GROUPS = 4
D_FF = 2816
N_MOD = 9
EPS = 1e-6

BF16 = jnp.bfloat16
F32 = jnp.float32

VMEM_LIMIT = 56 * 1024 * 1024
FFN_VMEM_LIMIT = 60 * 1024 * 1024
SUBLANES = 8
LANES = 128
TM = 1024
FF_CHUNK = 256
CONV_TL = 512
CONV_HALO = 16
NA_ROWS = 8
NA_QB = NA_ROWS * GRID_W
NA_KROWS = 16
NA_KB = NA_KROWS * GRID_W
NEG = -1e30


def _params(*sem):
    return pltpu.CompilerParams(dimension_semantics=sem, vmem_limit_bytes=VMEM_LIMIT)


def _resident(shape, index_map):
    return pl.BlockSpec(shape, index_map, pipeline_mode=pl.Buffered(1))


def _silu(x):
    return x * (1.0 / (1.0 + jnp.exp(-x)))


def _dot(a, b):
    return jnp.dot(a, b, preferred_element_type=F32)


def _rms(x, g):
    return x * lax.rsqrt(jnp.mean(x * x, axis=-1, keepdims=True) + EPS) * g


def _modnorm(x, g, shift, scale):
    return (_rms(x, g) * (1.0 + scale) + shift).astype(BF16)


ADA_TN = 768
ADA_SPLIT = 3


def _ada_body(cc_ref, *refs):
    w_refs, b_ref, o_ref = refs[:ADA_SPLIT], refs[ADA_SPLIT], refs[ADA_SPLIT + 1]
    s = _silu(cc_ref[...]).astype(BF16)
    for q, w_ref in enumerate(w_refs):
        cols = slice(q * ADA_TN, (q + 1) * ADA_TN)
        o_ref[0, :, cols] = _dot(s, w_ref[0].astype(BF16)) + b_ref[0, :, cols]


def _ada(cc, ada_w, ada_b):
    depth, _, n = ada_w.shape
    step = ADA_SPLIT * ADA_TN

    def w_spec(q):
        return pl.BlockSpec((1, D_MODEL, ADA_TN), lambda i, j: (i, 0, ADA_SPLIT * j + q))

    return pl.pallas_call(
        _ada_body,
        grid=(depth, n // step),
        in_specs=[pl.BlockSpec((8, D_MODEL), lambda i, j: (0, 0))]
        + [w_spec(q) for q in range(ADA_SPLIT)]
        + [pl.BlockSpec((1, 1, step), lambda i, j: (i, 0, j))],
        out_specs=pl.BlockSpec((1, 8, step), lambda i, j: (i, 0, j)),
        out_shape=jax.ShapeDtypeStruct((depth, 8, n), F32),
        compiler_params=_params("arbitrary", "arbitrary"),
        name="ada_mod",
    )(cc, *([ada_w] * ADA_SPLIT), ada_b.reshape(depth, 1, n))


def _mod_spec(row0, k):
    return pl.BlockSpec((1, 1, D_MODEL), lambda b, m: (row0 + b, 0, k))


def _const_row_spec(row):
    return pl.BlockSpec((1, 1, D_MODEL), lambda b, m: (row, 0, 0))


W_SLOTS = 2


def _weight_copies(win_hbm, wout_hbm, st_in, st_out, sem, w_idx, j, slot):
    lo = j * FF_CHUNK
    return (
        pltpu.make_async_copy(win_hbm.at[w_idx, :, pl.ds(lo, FF_CHUNK)], st_in.at[slot, 0], sem.at[slot, 0]),
        pltpu.make_async_copy(win_hbm.at[w_idx, :, pl.ds(D_FF + lo, FF_CHUNK)], st_in.at[slot, 1], sem.at[slot, 1]),
        pltpu.make_async_copy(wout_hbm.at[w_idx, pl.ds(lo, FF_CHUNK), :], st_out.at[slot], sem.at[slot, 2]),
    )


def _start_stream(copies):
    for j in range(W_SLOTS):
        for cp in copies(j, j):
            cp.start()


def _land_chunk(copies, j, win_ref, wout_ref, st_in, st_out):
    slot, lo = j % W_SLOTS, j * FF_CHUNK
    for cp in copies(j, slot):
        cp.wait()
    win_ref[:, lo:lo + FF_CHUNK] = st_in[slot, 0].astype(BF16)
    win_ref[:, D_FF + lo:D_FF + lo + FF_CHUNK] = st_in[slot, 1].astype(BF16)
    wout_ref[lo:lo + FF_CHUNK, :] = st_out[slot].astype(BF16)
    if j + W_SLOTS < D_FF // FF_CHUNK:
        for cp in copies(j + W_SLOTS, slot):
            cp.start()


def _ffn_compute(stream, x_ref, sh_ref, sc_ref, gt_ref, g_ref, win_hbm, wout_hbm, fg_ref, mix_refs,
                 o_ref, mid_ref, win_ref, wout_ref, st_in, st_out, sem, *, final, n_mix, mix_bias, w_idx):
    copies = functools.partial(_weight_copies, win_hbm, wout_hbm, st_in, st_out, sem, w_idx)
    if stream:
        _start_stream(copies)
    x = x_ref[0]
    if n_mix:
        acts, mg_ref, mw_ref = mix_refs[:n_mix], mix_refs[n_mix], mix_refs[n_mix + 1]
        y0, r = None, 0
        for a_ref in acts:
            k = a_ref.shape[-1]
            t = _dot(a_ref[0], mw_ref[r:r + k, :])
            y0 = t if y0 is None else y0 + t
            r += k
        if mix_bias:
            y0 = y0 + mix_refs[n_mix + 2][...]
        x = x + mg_ref[0] * y0
    xb = _modnorm(x, g_ref[0], sh_ref[0], sc_ref[0])
    for j in range(D_FF // FF_CHUNK):
        lo = j * FF_CHUNK
        if stream:
            _land_chunk(copies, j, win_ref, wout_ref, st_in, st_out)
        gate = _dot(xb, win_ref[:, lo:lo + FF_CHUNK])
        up = _dot(xb, win_ref[:, D_FF + lo:D_FF + lo + FF_CHUNK])
        mid_ref[:, lo:lo + FF_CHUNK] = (_silu(gate) * up).astype(BF16)
    y = _dot(mid_ref[...], wout_ref[...])
    h = x + (0.5 * gt_ref[0]) * y
    if final:
        h = _rms(h, fg_ref[0])
    o_ref[0] = h


def _preload_weights(win_hbm, wout_hbm, win_ref, wout_ref, st_in, st_out, sem, w_idx):
    copies = functools.partial(_weight_copies, win_hbm, wout_hbm, st_in, st_out, sem, w_idx)
    _start_stream(copies)
    for j in range(D_FF // FF_CHUNK):
        _land_chunk(copies, j, win_ref, wout_ref, st_in, st_out)


def _ffn_body(x_ref, sh_ref, sc_ref, gt_ref, g_ref, win_hbm, wout_hbm, fg_ref, *rest, single_step, **static):
    args = (x_ref, sh_ref, sc_ref, gt_ref, g_ref, win_hbm, wout_hbm, fg_ref, rest[:-7], *rest[-7:])
    is_first = (pl.program_id(0) == 0) & (pl.program_id(1) == 0)
    if single_step:
        _ffn_compute(True, *args, **static)
    elif static["n_mix"]:
        pl.when(is_first)(lambda: _preload_weights(win_hbm, wout_hbm, *rest[-5:], static["w_idx"]))
        _ffn_compute(False, *args, **static)
    else:
        pl.when(is_first)(lambda: _ffn_compute(True, *args, **static))
        pl.when(jnp.logical_not(is_first))(lambda: _ffn_compute(False, *args, **static))


def _ffn(h, mod, row0, k0, g_all, g_row, w_in, w_out, w_idx, final_g, final, mix=None):
    nb, n, _ = h.shape
    tm = min(TM, n)
    in_specs = [
        pl.BlockSpec((1, tm, D_MODEL), lambda b, m: (b, m, 0)),
        _mod_spec(row0, k0), _mod_spec(row0, k0 + 1), _mod_spec(row0, k0 + 2),
        _const_row_spec(g_row),
        pl.BlockSpec(memory_space=pl.ANY),
        pl.BlockSpec(memory_space=pl.ANY),
        _const_row_spec(0),
    ]
    args = [h, mod, mod, mod, g_all, w_in, w_out, final_g]
    n_mix, mix_bias = 0, False
    if mix is not None:
        acts, mw, mb, mk = mix
        n_mix, mix_bias = len(acts), mb is not None
        in_specs += [pl.BlockSpec((1, tm, a.shape[-1]), lambda b, m: (b, m, 0)) for a in acts]
        in_specs += [_mod_spec(row0, mk), _resident(mw.shape, lambda b, m: (0, 0))]
        args += [*acts, mod, mw]
        if mix_bias:
            in_specs.append(pl.BlockSpec((1, D_MODEL), lambda b, m: (0, 0)))
            args.append(mb)
    return pl.pallas_call(
        functools.partial(_ffn_body, final=final, n_mix=n_mix, mix_bias=mix_bias, w_idx=w_idx,
                          single_step=(nb * (n // tm) == 1)),
        grid=(nb, n // tm),
        in_specs=in_specs,
        out_specs=pl.BlockSpec((1, tm, D_MODEL), lambda b, m: (b, m, 0)),
        out_shape=jax.ShapeDtypeStruct(h.shape, F32),
        scratch_shapes=[
            pltpu.VMEM((tm, D_FF), BF16),
            pltpu.VMEM((D_MODEL, 2 * D_FF), BF16),
            pltpu.VMEM((D_FF, D_MODEL), BF16),
            pltpu.VMEM((W_SLOTS, 2, D_MODEL, FF_CHUNK), F32),
            pltpu.VMEM((W_SLOTS, FF_CHUNK, D_MODEL), F32),
            pltpu.SemaphoreType.DMA((W_SLOTS, 3)),
        ],
        compiler_params=pltpu.CompilerParams(dimension_semantics=("arbitrary", "arbitrary"),
                                             vmem_limit_bytes=FFN_VMEM_LIMIT),
        name="ffn_final" if final else "ffn",
    )(*args)


def _proj_body(x_ref, sh_ref, sc_ref, g_ref, w_ref, u_ref, qkv_ref):
    xb = _modnorm(x_ref[0], g_ref[0], sh_ref[0], sc_ref[0])
    u_ref[0] = _dot(xb, w_ref[:, :2 * D_CONV])
    qkv_ref[0] = _dot(xb, w_ref[:, 2 * D_CONV:]).astype(BF16)


def _proj(h, mod, row0, k0, g_all, g_row, w):
    nb, n, _ = h.shape
    return pl.pallas_call(
        _proj_body,
        grid=(nb, n // TM),
        in_specs=[
            pl.BlockSpec((1, TM, D_MODEL), lambda b, m: (b, m, 0)),
            _mod_spec(row0, k0), _mod_spec(row0, k0 + 1),
            _const_row_spec(g_row),
            _resident(w.shape, lambda b, m: (0, 0)),
        ],
        out_specs=[
            pl.BlockSpec((1, TM, 2 * D_CONV), lambda b, m: (b, m, 0)),
            pl.BlockSpec((1, TM, 3 * D_NA), lambda b, m: (b, m, 0)),
        ],
        out_shape=[
            jax.ShapeDtypeStruct((nb, n, 2 * D_CONV), F32),
            jax.ShapeDtypeStruct((nb, n, 3 * D_NA), BF16),
        ],
        compiler_params=_params("arbitrary", "arbitrary"),
        name="ab_proj",
    )(h, mod, mod, g_all, w)


def _qkv_body(x_ref, sh_ref, sc_ref, g_ref, w_ref, qkv_ref):
    xb = _modnorm(x_ref[0], g_ref[0], sh_ref[0], sc_ref[0])
    qkv_ref[0] = _dot(xb, w_ref[:, 2 * D_CONV:]).astype(BF16)


def _qkv_proj(h, mod, row0, k0, g_all, g_row, w):
    nb, n, _ = h.shape
    tm = min(TM, n)
    return pl.pallas_call(
        _qkv_body,
        grid=(nb, n // tm),
        in_specs=[
            pl.BlockSpec((1, tm, D_MODEL), lambda b, m: (b, m, 0)),
            _mod_spec(row0, k0), _mod_spec(row0, k0 + 1),
            _const_row_spec(g_row),
            _resident(w.shape, lambda b, m: (0, 0)),
        ],
        out_specs=pl.BlockSpec((1, tm, 3 * D_NA), lambda b, m: (b, m, 0)),
        out_shape=jax.ShapeDtypeStruct((nb, n, 3 * D_NA), BF16),
        compiler_params=_params("arbitrary", "arbitrary"),
        name="ctx_qkv",
    )(h, mod, mod, g_all, w)


def _glu(v):
    return v[:, :D_CONV] * (1.0 / (1.0 + jnp.exp(-v[:, D_CONV:])))


CONV_RC = 64
CONV_N = CONV_TL + 2 * CONV_HALO


def _conv_body(cur_ref, prev_ref, next_ref, w_ref, b_ref, lg_ref, lb_ref, o_ref, y_ref):
    t = pl.program_id(1)
    nt = pl.num_programs(1)
    y_ref[0, CONV_HALO:CONV_HALO + CONV_TL, :] = _glu(cur_ref[0])
    y_ref[0, 0:CONV_HALO, :] = jnp.where(t > 0, _glu(prev_ref[0]), 0.0)
    y_ref[0, CONV_HALO + CONV_TL:, :] = jnp.where(t < nt - 1, _glu(next_ref[0]), 0.0)
    for s in range(1, SUBLANES):
        y_ref[s, 0:CONV_N - SUBLANES, :] = y_ref[0, s:s + CONV_N - SUBLANES, :]
    off = CONV_HALO - CONV_WIDTH // 2
    for r in range(0, CONV_TL, CONV_RC):
        acc = jnp.zeros((CONV_RC // SUBLANES, SUBLANES, D_CONV), F32)
        for k in range(CONV_WIDTH):
            m8, s = divmod(off + k, SUBLANES)
            lo = r + SUBLANES * m8
            yk = y_ref[s, lo:lo + CONV_RC, :].reshape(CONV_RC // SUBLANES, SUBLANES, D_CONV)
            acc = acc + w_ref[k][None] * yk
        acc = acc.reshape(CONV_RC, D_CONV) + b_ref[...]
        mu = jnp.mean(acc, axis=-1, keepdims=True)
        cen = acc - mu
        var = jnp.mean(cen * cen, axis=-1, keepdims=True)
        z = cen * lax.rsqrt(var + EPS) * lg_ref[...] + lb_ref[...]
        o_ref[0, r:r + CONV_RC, :] = _silu(z).astype(BF16)


def _conv(u, w, b, ln_g, ln_b):
    nb, n, _ = u.shape
    nt = n // CONV_TL
    hb = CONV_TL // CONV_HALO
    last = n // CONV_HALO - 1
    row = lambda v: v.reshape(1, D_CONV)
    w_rep = jnp.broadcast_to(w[:, None, :], (CONV_WIDTH, SUBLANES, D_CONV))
    return pl.pallas_call(
        _conv_body,
        grid=(nb, nt),
        in_specs=[
            pl.BlockSpec((1, CONV_TL, 2 * D_CONV), lambda b_, t: (b_, t, 0)),
            pl.BlockSpec((1, CONV_HALO, 2 * D_CONV), lambda b_, t: (b_, jnp.maximum(t * hb - 1, 0), 0)),
            pl.BlockSpec((1, CONV_HALO, 2 * D_CONV), lambda b_, t: (b_, jnp.minimum((t + 1) * hb, last), 0)),
            pl.BlockSpec((CONV_WIDTH, SUBLANES, D_CONV), lambda b_, t: (0, 0, 0)),
            pl.BlockSpec((1, D_CONV), lambda b_, t: (0, 0)),
            pl.BlockSpec((1, D_CONV), lambda b_, t: (0, 0)),
            pl.BlockSpec((1, D_CONV), lambda b_, t: (0, 0)),
        ],
        out_specs=pl.BlockSpec((1, CONV_TL, D_CONV), lambda b_, t: (b_, t, 0)),
        out_shape=jax.ShapeDtypeStruct((nb, n, D_CONV), BF16),
        scratch_shapes=[pltpu.VMEM((SUBLANES, CONV_N, D_CONV), F32)],
        compiler_params=_params("arbitrary", "arbitrary"),
        name="conv_module",
    )(u, u, u, w_rep, row(b), row(ln_g), row(ln_b))


NA_HG = 4
NA_DR_PAD = 8
NA_T2 = 2 * NA_KH - 1 + 2 * NA_DR_PAD - 1


def _na_bias_table(rpb):
    qc = np.arange(GRID_W)
    cs = np.clip(qc - NA_KW // 2, 0, GRID_W - NA_KW)
    kc = np.arange(GRID_W)
    col_valid = (kc[None, :] >= cs[:, None]) & (kc[None, :] < cs[:, None] + NA_KW)
    dc = kc[None, :] - qc[:, None] + NA_KW - 1
    oh_c = np.zeros((2 * NA_KW - 1, GRID_W, GRID_W), np.float32)
    qi, ki = np.nonzero(col_valid)
    oh_c[dc[qi, ki], qi, ki] = 1.0
    t_col = jnp.einsum('hrd,dqk->hrqk', rpb, jnp.asarray(oh_c), precision=lax.Precision.HIGHEST)
    t_col = jnp.where(jnp.asarray(col_valid)[None, None], t_col, NEG)
    t_pad = jnp.pad(t_col, ((0, 0), (NA_DR_PAD, NA_DR_PAD), (0, 0), (0, 0)))
    return jnp.concatenate([t_pad[:, :NA_T2], t_pad[:, 1:NA_T2 + 1]], axis=-1)


def _na_probs_clamped(t2_ref, base, row_valid, j, s, sc):
    parts = []
    for i in range(NA_ROWS):
        bias = jnp.concatenate([t2_ref[j, base + 2 * p - i] for p in range(NA_KROWS // 2)], axis=-1)
        parts.append(jnp.where(row_valid[i], s[i * GRID_W:(i + 1) * GRID_W] + bias, NEG))
    s = jnp.concatenate(parts, axis=0)
    m = jnp.maximum(jnp.max(s, axis=-1, keepdims=True), jnp.max(sc, axis=-1, keepdims=True))
    return jnp.exp(s - m).astype(BF16), jnp.exp(sc - m).astype(BF16)


def _na_probs_interior(t2_ref, j, s, sc):
    base = NA_KH - 1 + NA_DR_PAD - NA_KH // 2
    plane = lax.broadcasted_iota(jnp.int32, (1, LANES), 1)
    zero = jnp.zeros((GRID_W, LANES), BF16)
    p_rows, pc_rows = [], []
    for i in range(NA_ROWS):
        rows = slice(i * GRID_W, (i + 1) * GRID_W)
        pieces = {}
        for p in range(NA_KROWS // 2):
            first, second = i <= 2 * p < i + NA_KH, i <= 2 * p + 1 < i + NA_KH
            if not (first or second):
                continue
            piece = s[rows, p * LANES:(p + 1) * LANES] + t2_ref[j, base + 2 * p - i]
            if not (first and second):
                piece = jnp.where((plane < GRID_W) if first else (plane >= GRID_W), piece, NEG)
            pieces[p] = piece
        sc_i = sc[rows]
        m = jnp.max(sc_i, axis=-1, keepdims=True)
        for piece in pieces.values():
            m = jnp.maximum(m, jnp.max(piece, axis=-1, keepdims=True))
        p_rows.append(jnp.concatenate(
            [jnp.exp(pieces[p] - m).astype(BF16) if p in pieces else zero for p in range(NA_KROWS // 2)], axis=-1))
        pc_rows.append(jnp.exp(sc_i - m).astype(BF16))
    return jnp.concatenate(p_rows, axis=0), jnp.concatenate(pc_rows, axis=0)


def _na_body(q_ref, k_ref, v_ref, kc_ref, vc_ref, t2_ref, o_ref):
    blk = pl.program_id(2)
    w0 = jnp.clip(NA_ROWS * blk - NA_KH // 2, 0, GRID_W - NA_KROWS)
    start = pl.multiple_of(w0 * GRID_W, 256)
    q2 = q_ref[0] * jnp.asarray(HEAD_DIM ** -0.5, BF16)
    kw = k_ref[0, pl.ds(start, NA_KB), :]
    vw = v_ref[0, pl.ds(start, NA_KB), :]
    kc = kc_ref[0]
    vc = vc_ref[0]
    lane = lax.broadcasted_iota(jnp.int32, (1, NA_HG * HEAD_DIM), 1)
    nt = (((1,), (1,)), ((), ()))
    one = jnp.ones((), BF16)

    def run(probs_fn):
        out = None
        for j in range(NA_HG):
            in_head = (lane >= j * HEAD_DIM) & (lane < (j + 1) * HEAD_DIM)
            qm = jnp.where(in_head, q2, jnp.zeros_like(q2))
            s = lax.dot_general(qm, kw, nt, preferred_element_type=F32)
            sc = lax.dot_general(qm, kc, nt, preferred_element_type=F32)
            p, pc = probs_fn(j, s, sc)
            o = _dot(p, jnp.where(in_head, vw, one)) + _dot(pc, jnp.where(in_head, vc, one))
            o = o * (1.0 / pltpu.roll(o, HEAD_DIM, axis=1))
            out = o if out is None else jnp.where(in_head, o, out)
        o_ref[0] = out.astype(BF16)

    interior = (blk > 0) & (blk < pl.num_programs(2) - 1)

    @pl.when(interior)
    def _():
        run(functools.partial(_na_probs_interior, t2_ref))

    @pl.when(jnp.logical_not(interior))
    def _():
        base = w0 - NA_ROWS * blk + NA_KH - 1 + NA_DR_PAD
        klane = lax.broadcasted_iota(jnp.int32, (1, NA_KB), 1)
        row_valid = []
        for i in range(NA_ROWS):
            a_lo = jnp.clip(NA_ROWS * blk + i - NA_KH // 2, 0, GRID_W - NA_KH) - w0
            row_valid.append((klane >= a_lo * GRID_W) & (klane < (a_lo + NA_KH) * GRID_W))
        run(functools.partial(_na_probs_clamped, t2_ref, base, row_valid))


def _natten(qkv, qkv_c, t2):
    nb, n, _ = qkv.shape
    nctx = qkv_c.shape[1]
    ng = NA_HEADS // NA_HG
    lanes = NA_HG * HEAD_DIM
    return pl.pallas_call(
        _na_body,
        grid=(ng, nb, n // NA_QB),
        in_specs=[
            pl.BlockSpec((1, NA_QB, lanes), lambda h, b, i: (b, i, h)),
            pl.BlockSpec((1, n, lanes), lambda h, b, i: (b, 0, ng + h)),
            pl.BlockSpec((1, n, lanes), lambda h, b, i: (b, 0, 2 * ng + h)),
            pl.BlockSpec((1, nctx, lanes), lambda h, b, i: (b, 0, ng + h)),
            pl.BlockSpec((1, nctx, lanes), lambda h, b, i: (b, 0, 2 * ng + h)),
            pl.BlockSpec((NA_HG, NA_T2, GRID_W, 2 * GRID_W), lambda h, b, i: (h, 0, 0, 0)),
        ],
        out_specs=pl.BlockSpec((1, NA_QB, lanes), lambda h, b, i: (b, i, h)),
        out_shape=jax.ShapeDtypeStruct((nb, n, D_NA), BF16),
        compiler_params=_params("arbitrary", "arbitrary", "arbitrary"),
        name="natten",
    )(qkv, qkv, qkv, qkv_c, qkv_c, t2)


SEQ = GRID_W * GRID_W
FN_R = SUBLANES
FN_M = SEQ // FN_R
FN_TA = 1024
FN_LANES = 256
FN_CH = 16


def _dft_tables(n):
    idx = np.arange(n, dtype=np.int64)
    ang = 2.0 * np.pi * ((idx[:, None] * idx[None, :]) % n).astype(np.float64) / n
    scale = 1.0 / np.sqrt(n)
    return (np.cos(ang) * scale).astype(np.float32), (np.sin(ang) * scale).astype(np.float32)


def _seq_tables():
    k2 = np.arange(FN_M, dtype=np.int64)
    n2 = np.arange(FN_M, dtype=np.int64)
    out = np.zeros((FN_R, 2 * FN_M, 2 * FN_M), np.float32)
    for n1 in range(FN_R):
        num = (k2[:, None] * n2[None, :] * FN_R + n1 * k2[:, None]) % SEQ
        ang = 2.0 * np.pi * num.astype(np.float64) / SEQ
        c = np.cos(ang) / np.sqrt(SEQ)
        s = np.sin(ang) / np.sqrt(SEQ)
        out[n1, :FN_M, :FN_M] = c
        out[n1, :FN_M, FN_M:] = s
        out[n1, FN_M:, :FN_M] = -s
        out[n1, FN_M:, FN_M:] = c
    return out


def _fnet_a_body(*refs):
    nx = D_MODEL // LANES
    x_refs = refs[:nx]
    sh_ref, sc_ref, g_ref, cs_ref, a_ref, b_ref = refs[nx:]
    gw = D_MODEL // FNET_GROUPS
    for s in range(FN_R):
        xs = jnp.concatenate([x[0, pl.ds(s, FN_TA // FN_R, stride=FN_R), :] for x in x_refs], axis=-1)
        xb = _modnorm(xs, g_ref[0], sh_ref[0], sc_ref[0])
        for grp in range(FNET_GROUPS):
            ab = _dot(xb[:, grp * gw:(grp + 1) * gw], cs_ref[...])
            a_ref[0, s, :, grp * gw:(grp + 1) * gw] = ab[:, :gw].astype(BF16)
            b_ref[0, s, :, grp * gw:(grp + 1) * gw] = ab[:, gw:].astype(BF16)


def _fnet_a(h, mod, row0, k0, g_all, g_row, cs):
    nb, n, _ = h.shape
    gw = D_MODEL // FNET_GROUPS
    nx = D_MODEL // LANES
    out_spec = pl.BlockSpec((1, FN_R, FN_TA // FN_R, D_MODEL), lambda b, m: (b, 0, m, 0))
    return pl.pallas_call(
        _fnet_a_body,
        grid=(nb, n // FN_TA),
        in_specs=[pl.BlockSpec((1, FN_TA, LANES), functools.partial(lambda b, m, c: (b, m, c), c=c))
                  for c in range(nx)]
        + [_mod_spec(row0, k0), _mod_spec(row0, k0 + 1), _const_row_spec(g_row),
           _resident((gw, 2 * gw), lambda b, m: (0, 0))],
        out_specs=[out_spec, out_spec],
        out_shape=[jax.ShapeDtypeStruct((nb, FN_R, n // FN_R, D_MODEL), BF16)] * 2,
        compiler_params=_params("arbitrary", "arbitrary"),
        name="fnet_channel_dft",
    )(*([h] * nx), mod, mod, g_all, cs)


def _fnet_b_body(zr_ref, zi_ref, m_ref, o_ref, v_ref):
    for n1 in range(FN_R):
        z = jnp.concatenate([zr_ref[0, n1], zi_ref[0, n1]], axis=0)
        v_ref[n1] = _dot(m_ref[n1], z)

    rt = np.float32(np.sqrt(0.5))

    def chunk(i, carry):
        r0 = pl.multiple_of(i * FN_CH, FN_CH)
        for lt in range(FN_LANES // LANES):
            ls = slice(lt * LANES, (lt + 1) * LANES)
            re = [v_ref[n, pl.ds(r0, FN_CH), ls] for n in range(FN_R)]
            im = [v_ref[n, pl.ds(FN_M + r0, FN_CH), ls] for n in range(FN_R)]
            e0 = (re[0] + re[4]) + (re[2] + re[6])
            e2 = (re[0] + re[4]) - (re[2] + re[6])
            e1 = (re[0] - re[4]) + (im[2] - im[6])
            e3 = (re[0] - re[4]) - (im[2] - im[6])
            t0r, t0i = re[1] + re[5], im[1] + im[5]
            t1r, t1i = re[1] - re[5], im[1] - im[5]
            t2r, t2i = re[3] + re[7], im[3] + im[7]
            t3r, t3i = re[3] - re[7], im[3] - im[7]
            p0 = t0r + t2r
            p2 = t0i - t2i
            al = t1r - t3r
            be = t1i + t3i
            p1 = (al + be) * rt
            p3 = (be - al) * rt
            ys = (e0 + p0, e1 + p1, e2 + p2, e3 + p3, e0 - p0, e1 - p1, e2 - p2, e3 - p3)
            for k1 in range(FN_R):
                o_ref[0, pl.ds(k1 * FN_M + r0, FN_CH), ls] = ys[k1].astype(BF16)
        return carry

    lax.fori_loop(0, FN_M // FN_CH, chunk, 0)


def _fnet_b(zr, zi, mtab):
    assert FN_R == 8
    nb = zr.shape[0]
    z_spec = pl.BlockSpec((1, FN_R, FN_M, FN_LANES), lambda b, l: (b, 0, 0, l))
    return pl.pallas_call(
        _fnet_b_body,
        grid=(nb, D_MODEL // FN_LANES),
        in_specs=[z_spec, z_spec, _resident((FN_R, 2 * FN_M, 2 * FN_M), lambda b, l: (0, 0, 0))],
        out_specs=pl.BlockSpec((1, SEQ, FN_LANES), lambda b, l: (b, 0, l)),
        out_shape=jax.ShapeDtypeStruct((nb, SEQ, D_MODEL), BF16),
        scratch_shapes=[pltpu.VMEM((FN_R, 2 * FN_M, FN_LANES), F32)],
        compiler_params=_params("arbitrary", "arbitrary"),
        name="fnet_seq_dft",
    )(zr, zi, mtab)


def kernel(x, c, ctx, c_ctx, ada_w, ada_b, norm_g, ffn_w_in, ffn_w_out, ab_w_in, conv_w, conv_b,
           conv_ln_g, conv_ln_b, na_rpb, ab_w_out, fnet_w, fnet_b, final_g):
    nb, n, d = x.shape
    depth = ada_w.shape[0]
    nctx = ctx.shape[1]
    assert (d, depth, nb) == (D_MODEL, 2, 4) and n == GRID_W * GRID_W

    cc = jnp.concatenate([c, c_ctx[None], jnp.zeros((8 - nb - 1, d), F32)], axis=0)
    mod = _ada(cc, ada_w, ada_b).reshape(depth * 8, 1, N_MOD * d)
    ctx_row = nb

    g_all = norm_g.reshape(depth * 3, 1, d)
    fg = final_g.reshape(1, 1, d)
    w_in = ffn_w_in.reshape(depth * 2, d, 2 * D_FF)
    w_out = ffn_w_out.reshape(depth * 2, D_FF, d)

    h = _ffn(x, mod, 0, 0, g_all, 0, w_in, w_out, 0, fg, False)
    hc = _ffn(ctx.reshape(1, nb * nctx, d), mod, ctx_row, 0, g_all, 0, w_in, w_out, 0, fg, False)
    w_ab = ab_w_in[0].astype(BF16)
    u, qkv = _proj(h, mod, 0, 3, g_all, 1, w_ab)
    qkv_c = _qkv_proj(hc, mod, ctx_row, 3, g_all, 1, w_ab)
    conv_x = _conv(u, conv_w[0], conv_b[0], conv_ln_g[0], conv_ln_b[0])
    att_x = _natten(qkv, qkv_c.reshape(nb, nctx, 3 * D_NA), _na_bias_table(na_rpb[0]))
    h = _ffn(h, mod, 0, 6, g_all, 2, w_in, w_out, 1, fg, False,
             mix=((conv_x, att_x), ab_w_out[0].astype(BF16), None, 5))

    h = _ffn(h, mod, 8, 0, g_all, 3, w_in, w_out, 2, fg, False)
    gw = d // FNET_GROUPS
    cc_tab, sc_tab = _dft_tables(gw)
    cs = jnp.asarray(np.concatenate([cc_tab, -sc_tab], axis=1)).astype(BF16)
    zr, zi = _fnet_a(h, mod, 8, 3, g_all, 4, cs)
    f = _fnet_b(zr, zi, jnp.asarray(_seq_tables()).astype(BF16))
    return _ffn(h, mod, 8, 6, g_all, 5, w_in, w_out, 3, fg, True,
                mix=((f,), fnet_w[0].astype(BF16), fnet_b[0].reshape(1, d), 5))
```

```python
import functools

import numpy as np
import jax
import jax.numpy as jnp
from jax import lax
from jax.experimental import pallas as pl
from jax.experimental.pallas import tpu as pltpu

D_MODEL = 1024
GRID_W = 64
D_CONV = 512
D_NA = 512
NA_HEADS = 8
HEAD_DIM = 64
CONV_WIDTH = 31
NA_KH = 8
NA_KW = 16
FNET_GROUPS = 4
D_FF = 2816
N_MOD = 9
EPS = 1e-6

BF16 = jnp.bfloat16
F32 = jnp.float32

VMEM_LIMIT = 56 * 1024 * 1024
FFN_VMEM_LIMIT = 60 * 1024 * 1024
SUBLANES = 8
LANES = 128
TM = 1024
FF_CHUNK = 256
CONV_TL = 512
CONV_HALO = 16
NA_ROWS = 8
NA_QB = NA_ROWS * GRID_W
NA_KROWS = 16
NA_KB = NA_KROWS * GRID_W
NEG = -1e30


def _params(*sem):
    return pltpu.CompilerParams(dimension_semantics=sem, vmem_limit_bytes=VMEM_LIMIT)


def _resident(shape, index_map):
    return pl.BlockSpec(shape, index_map, pipeline_mode=pl.Buffered(1))


def _silu(x):
    return x * (1.0 / (1.0 + jnp.exp(-x)))


def _dot(a, b):
    return jnp.dot(a, b, preferred_element_type=F32)


def _rms(x, g):
    return x * lax.rsqrt(jnp.mean(x * x, axis=-1, keepdims=True) + EPS) * g


def _modnorm(x, g, shift, scale):
    return (_rms(x, g) * (1.0 + scale) + shift).astype(BF16)


ADA_TN = 768
ADA_SPLIT = 3


def _ada_body(cc_ref, *refs):
    w_refs, b_ref, o_ref = refs[:ADA_SPLIT], refs[ADA_SPLIT], refs[ADA_SPLIT + 1]
    s = _silu(cc_ref[...]).astype(BF16)
    for q, w_ref in enumerate(w_refs):
        cols = slice(q * ADA_TN, (q + 1) * ADA_TN)
        o_ref[0, :, cols] = _dot(s, w_ref[0].astype(BF16)) + b_ref[0, :, cols]


def _ada(cc, ada_w, ada_b):
    depth, _, n = ada_w.shape
    step = ADA_SPLIT * ADA_TN

    def w_spec(q):
        return pl.BlockSpec((1, D_MODEL, ADA_TN), lambda i, j: (i, 0, ADA_SPLIT * j + q))

    return pl.pallas_call(
        _ada_body,
        grid=(depth, n // step),
        in_specs=[pl.BlockSpec((8, D_MODEL), lambda i, j: (0, 0))]
        + [w_spec(q) for q in range(ADA_SPLIT)]
        + [pl.BlockSpec((1, 1, step), lambda i, j: (i, 0, j))],
        out_specs=pl.BlockSpec((1, 8, step), lambda i, j: (i, 0, j)),
        out_shape=jax.ShapeDtypeStruct((depth, 8, n), F32),
        compiler_params=_params("arbitrary", "arbitrary"),
        name="ada_mod",
    )(cc, *([ada_w] * ADA_SPLIT), ada_b.reshape(depth, 1, n))


def _mod_spec(row0, k):
    return pl.BlockSpec((1, 1, D_MODEL), lambda b, m: (row0 + b, 0, k))


def _const_row_spec(row):
    return pl.BlockSpec((1, 1, D_MODEL), lambda b, m: (row, 0, 0))


W_SLOTS = 2


def _weight_copies(win_hbm, wout_hbm, st_in, st_out, sem, w_idx, j, slot):
    lo = j * FF_CHUNK
    return (
        pltpu.make_async_copy(win_hbm.at[w_idx, :, pl.ds(lo, FF_CHUNK)], st_in.at[slot, 0], sem.at[slot, 0]),
        pltpu.make_async_copy(win_hbm.at[w_idx, :, pl.ds(D_FF + lo, FF_CHUNK)], st_in.at[slot, 1], sem.at[slot, 1]),
        pltpu.make_async_copy(wout_hbm.at[w_idx, pl.ds(lo, FF_CHUNK), :], st_out.at[slot], sem.at[slot, 2]),
    )


def _start_stream(copies):
    for j in range(W_SLOTS):
        for cp in copies(j, j):
            cp.start()


def _land_chunk(copies, j, win_ref, wout_ref, st_in, st_out):
    slot, lo = j % W_SLOTS, j * FF_CHUNK
    for cp in copies(j, slot):
        cp.wait()
    win_ref[:, lo:lo + FF_CHUNK] = st_in[slot, 0].astype(BF16)
    win_ref[:, D_FF + lo:D_FF + lo + FF_CHUNK] = st_in[slot, 1].astype(BF16)
    wout_ref[lo:lo + FF_CHUNK, :] = st_out[slot].astype(BF16)
    if j + W_SLOTS < D_FF // FF_CHUNK:
        for cp in copies(j + W_SLOTS, slot):
            cp.start()


def _ffn_compute(stream, x_ref, sh_ref, sc_ref, gt_ref, g_ref, win_hbm, wout_hbm, fg_ref, mix_refs,
                 o_ref, mid_ref, win_ref, wout_ref, st_in, st_out, sem, *, final, n_mix, mix_bias, w_idx):
    copies = functools.partial(_weight_copies, win_hbm, wout_hbm, st_in, st_out, sem, w_idx)
    if stream:
        _start_stream(copies)
    x = x_ref[0]
    if n_mix:
        acts, mg_ref, mw_ref = mix_refs[:n_mix], mix_refs[n_mix], mix_refs[n_mix + 1]
        y0, r = None, 0
        for a_ref in acts:
            k = a_ref.shape[-1]
            t = _dot(a_ref[0], mw_ref[r:r + k, :])
            y0 = t if y0 is None else y0 + t
            r += k
        if mix_bias:
            y0 = y0 + mix_refs[n_mix + 2][...]
        x = x + mg_ref[0] * y0
    xb = _modnorm(x, g_ref[0], sh_ref[0], sc_ref[0])
    for j in range(D_FF // FF_CHUNK):
        lo = j * FF_CHUNK
        if stream:
            _land_chunk(copies, j, win_ref, wout_ref, st_in, st_out)
        gate = _dot(xb, win_ref[:, lo:lo + FF_CHUNK])
        up = _dot(xb, win_ref[:, D_FF + lo:D_FF + lo + FF_CHUNK])
        mid_ref[:, lo:lo + FF_CHUNK] = (_silu(gate) * up).astype(BF16)
    y = _dot(mid_ref[...], wout_ref[...])
    h = x + (0.5 * gt_ref[0]) * y
    if final:
        h = _rms(h, fg_ref[0])
    o_ref[0] = h


def _preload_weights(win_hbm, wout_hbm, win_ref, wout_ref, st_in, st_out, sem, w_idx):
    copies = functools.partial(_weight_copies, win_hbm, wout_hbm, st_in, st_out, sem, w_idx)
    _start_stream(copies)
    for j in range(D_FF // FF_CHUNK):
        _land_chunk(copies, j, win_ref, wout_ref, st_in, st_out)


def _ffn_body(x_ref, sh_ref, sc_ref, gt_ref, g_ref, win_hbm, wout_hbm, fg_ref, *rest, single_step, **static):
    args = (x_ref, sh_ref, sc_ref, gt_ref, g_ref, win_hbm, wout_hbm, fg_ref, rest[:-7], *rest[-7:])
    is_first = (pl.program_id(0) == 0) & (pl.program_id(1) == 0)
    if single_step:
        _ffn_compute(True, *args, **static)
    elif static["n_mix"]:
        pl.when(is_first)(lambda: _preload_weights(win_hbm, wout_hbm, *rest[-5:], static["w_idx"]))
        _ffn_compute(False, *args, **static)
    else:
        pl.when(is_first)(lambda: _ffn_compute(True, *args, **static))
        pl.when(jnp.logical_not(is_first))(lambda: _ffn_compute(False, *args, **static))


def _ffn(h, mod, row0, k0, g_all, g_row, w_in, w_out, w_idx, final_g, final, mix=None):
    nb, n, _ = h.shape
    tm = min(TM, n)
    in_specs = [
        pl.BlockSpec((1, tm, D_MODEL), lambda b, m: (b, m, 0)),
        _mod_spec(row0, k0), _mod_spec(row0, k0 + 1), _mod_spec(row0, k0 + 2),
        _const_row_spec(g_row),
        pl.BlockSpec(memory_space=pl.ANY),
        pl.BlockSpec(memory_space=pl.ANY),
        _const_row_spec(0),
    ]
    args = [h, mod, mod, mod, g_all, w_in, w_out, final_g]
    n_mix, mix_bias = 0, False
    if mix is not None:
        acts, mw, mb, mk = mix
        n_mix, mix_bias = len(acts), mb is not None
        in_specs += [pl.BlockSpec((1, tm, a.shape[-1]), lambda b, m: (b, m, 0)) for a in acts]
        in_specs += [_mod_spec(row0, mk), _resident(mw.shape, lambda b, m: (0, 0))]
        args += [*acts, mod, mw]
        if mix_bias:
            in_specs.append(pl.BlockSpec((1, D_MODEL), lambda b, m: (0, 0)))
            args.append(mb)
    return pl.pallas_call(
        functools.partial(_ffn_body, final=final, n_mix=n_mix, mix_bias=mix_bias, w_idx=w_idx,
                          single_step=(nb * (n // tm) == 1)),
        grid=(nb, n // tm),
        in_specs=in_specs,
        out_specs=pl.BlockSpec((1, tm, D_MODEL), lambda b, m: (b, m, 0)),
        out_shape=jax.ShapeDtypeStruct(h.shape, F32),
        scratch_shapes=[
            pltpu.VMEM((tm, D_FF), BF16),
            pltpu.VMEM((D_MODEL, 2 * D_FF), BF16),
            pltpu.VMEM((D_FF, D_MODEL), BF16),
            pltpu.VMEM((W_SLOTS, 2, D_MODEL, FF_CHUNK), F32),
            pltpu.VMEM((W_SLOTS, FF_CHUNK, D_MODEL), F32),
            pltpu.SemaphoreType.DMA((W_SLOTS, 3)),
        ],
        compiler_params=pltpu.CompilerParams(dimension_semantics=("arbitrary", "arbitrary"),
                                             vmem_limit_bytes=FFN_VMEM_LIMIT),
        name="ffn_final" if final else "ffn",
    )(*args)


def _proj_body(x_ref, sh_ref, sc_ref, g_ref, w_ref, u_ref, qkv_ref, wbf_ref):
    @pl.when((pl.program_id(0) == 0) & (pl.program_id(1) == 0))
    def _():
        wbf_ref[...] = w_ref[...].astype(BF16)

    xb = _modnorm(x_ref[0], g_ref[0], sh_ref[0], sc_ref[0])
    u_ref[0] = _dot(xb, wbf_ref[:, :2 * D_CONV])
    qkv_ref[0] = _dot(xb, wbf_ref[:, 2 * D_CONV:]).astype(BF16)


def _proj(h, mod, row0, k0, g_all, g_row, w):
    nb, n, _ = h.shape
    return pl.pallas_call(
        _proj_body,
        grid=(nb, n // TM),
        in_specs=[
            pl.BlockSpec((1, TM, D_MODEL), lambda b, m: (b, m, 0)),
            _mod_spec(row0, k0), _mod_spec(row0, k0 + 1),
            _const_row_spec(g_row),
            _resident(w.shape, lambda b, m: (0, 0)),
        ],
        out_specs=[
            pl.BlockSpec((1, TM, 2 * D_CONV), lambda b, m: (b, m, 0)),
            pl.BlockSpec((1, TM, 3 * D_NA), lambda b, m: (b, m, 0)),
        ],
        out_shape=[
            jax.ShapeDtypeStruct((nb, n, 2 * D_CONV), F32),
            jax.ShapeDtypeStruct((nb, n, 3 * D_NA), BF16),
        ],
        scratch_shapes=[pltpu.VMEM(w.shape, BF16)],
        compiler_params=_params("arbitrary", "arbitrary"),
        name="ab_proj",
    )(h, mod, mod, g_all, w)


def _qkv_body(x_ref, sh_ref, sc_ref, g_ref, w_ref, qkv_ref):
    xb = _modnorm(x_ref[0], g_ref[0], sh_ref[0], sc_ref[0])
    qkv_ref[0] = _dot(xb, w_ref[:, 2 * D_CONV:].astype(BF16)).astype(BF16)


def _qkv_proj(h, mod, row0, k0, g_all, g_row, w):
    nb, n, _ = h.shape
    tm = min(TM, n)
    return pl.pallas_call(
        _qkv_body,
        grid=(nb, n // tm),
        in_specs=[
            pl.BlockSpec((1, tm, D_MODEL), lambda b, m: (b, m, 0)),
            _mod_spec(row0, k0), _mod_spec(row0, k0 + 1),
            _const_row_spec(g_row),
            _resident(w.shape, lambda b, m: (0, 0)),
        ],
        out_specs=pl.BlockSpec((1, tm, 3 * D_NA), lambda b, m: (b, m, 0)),
        out_shape=jax.ShapeDtypeStruct((nb, n, 3 * D_NA), BF16),
        compiler_params=_params("arbitrary", "arbitrary"),
        name="ctx_qkv",
    )(h, mod, mod, g_all, w)


def _glu(v):
    return v[:, :D_CONV] * (1.0 / (1.0 + jnp.exp(-v[:, D_CONV:])))


CONV_RC = 64
CONV_N = CONV_TL + 2 * CONV_HALO


def _conv_body(cur_ref, prev_ref, next_ref, w_ref, b_ref, lg_ref, lb_ref, o_ref, y_ref):
    t = pl.program_id(1)
    nt = pl.num_programs(1)
    y_ref[0, CONV_HALO:CONV_HALO + CONV_TL, :] = _glu(cur_ref[0])
    y_ref[0, 0:CONV_HALO, :] = jnp.where(t > 0, _glu(prev_ref[0]), 0.0)
    y_ref[0, CONV_HALO + CONV_TL:, :] = jnp.where(t < nt - 1, _glu(next_ref[0]), 0.0)
    for s in range(1, SUBLANES):
        y_ref[s, 0:CONV_N - SUBLANES, :] = y_ref[0, s:s + CONV_N - SUBLANES, :]
    off = CONV_HALO - CONV_WIDTH // 2
    for r in range(0, CONV_TL, CONV_RC):
        acc = jnp.zeros((CONV_RC // SUBLANES, SUBLANES, D_CONV), F32)
        for k in range(CONV_WIDTH):
            m8, s = divmod(off + k, SUBLANES)
            lo = r + SUBLANES * m8
            yk = y_ref[s, lo:lo + CONV_RC, :].reshape(CONV_RC // SUBLANES, SUBLANES, D_CONV)
            acc = acc + w_ref[k][None] * yk
        acc = acc.reshape(CONV_RC, D_CONV) + b_ref[...]
        mu = jnp.mean(acc, axis=-1, keepdims=True)
        cen = acc - mu
        var = jnp.mean(cen * cen, axis=-1, keepdims=True)
        z = cen * lax.rsqrt(var + EPS) * lg_ref[...] + lb_ref[...]
        o_ref[0, r:r + CONV_RC, :] = _silu(z).astype(BF16)


def _conv(u, w, b, ln_g, ln_b):
    nb, n, _ = u.shape
    nt = n // CONV_TL
    hb = CONV_TL // CONV_HALO
    last = n // CONV_HALO - 1
    row = lambda v: v.reshape(1, D_CONV)
    w_rep = jnp.broadcast_to(w[:, None, :], (CONV_WIDTH, SUBLANES, D_CONV))
    return pl.pallas_call(
        _conv_body,
        grid=(nb, nt),
        in_specs=[
            pl.BlockSpec((1, CONV_TL, 2 * D_CONV), lambda b_, t: (b_, t, 0)),
            pl.BlockSpec((1, CONV_HALO, 2 * D_CONV), lambda b_, t: (b_, jnp.maximum(t * hb - 1, 0), 0)),
            pl.BlockSpec((1, CONV_HALO, 2 * D_CONV), lambda b_, t: (b_, jnp.minimum((t + 1) * hb, last), 0)),
            pl.BlockSpec((CONV_WIDTH, SUBLANES, D_CONV), lambda b_, t: (0, 0, 0)),
            pl.BlockSpec((1, D_CONV), lambda b_, t: (0, 0)),
            pl.BlockSpec((1, D_CONV), lambda b_, t: (0, 0)),
            pl.BlockSpec((1, D_CONV), lambda b_, t: (0, 0)),
        ],
        out_specs=pl.BlockSpec((1, CONV_TL, D_CONV), lambda b_, t: (b_, t, 0)),
        out_shape=jax.ShapeDtypeStruct((nb, n, D_CONV), BF16),
        scratch_shapes=[pltpu.VMEM((SUBLANES, CONV_N, D_CONV), F32)],
        compiler_params=_params("arbitrary", "arbitrary"),
        name="conv_module",
    )(u, u, u, w_rep, row(b), row(ln_g), row(ln_b))


NA_HG = 4
NA_DR_PAD = 8
NA_T2 = 2 * NA_KH - 1 + 2 * NA_DR_PAD - 1


def _na_bias_table(rpb):
    qc = np.arange(GRID_W)
    cs = np.clip(qc - NA_KW // 2, 0, GRID_W - NA_KW)
    kc = np.arange(GRID_W)
    col_valid = (kc[None, :] >= cs[:, None]) & (kc[None, :] < cs[:, None] + NA_KW)
    dc = kc[None, :] - qc[:, None] + NA_KW - 1
    oh_c = np.zeros((2 * NA_KW - 1, GRID_W, GRID_W), np.float32)
    qi, ki = np.nonzero(col_valid)
    oh_c[dc[qi, ki], qi, ki] = 1.0
    t_col = jnp.einsum('hrd,dqk->hrqk', rpb, jnp.asarray(oh_c), precision=lax.Precision.HIGHEST)
    t_col = jnp.where(jnp.asarray(col_valid)[None, None], t_col, NEG)
    t_pad = jnp.pad(t_col, ((0, 0), (NA_DR_PAD, NA_DR_PAD), (0, 0), (0, 0)))
    return jnp.concatenate([t_pad[:, :NA_T2], t_pad[:, 1:NA_T2 + 1]], axis=-1)


def _na_window(blk):
    w0 = min(max(NA_ROWS * blk - NA_KH // 2, 0), GRID_W - NA_KROWS)
    a_lo = [min(max(NA_ROWS * blk + i - NA_KH // 2, 0), GRID_W - NA_KH) - w0 for i in range(NA_ROWS)]
    return w0, a_lo


def _na_probs(t2_ref, blk, j, s, sc):
    w0, a_lo = _na_window(blk)
    base = w0 - NA_ROWS * blk + NA_KH - 1 + NA_DR_PAD
    plane = lax.broadcasted_iota(jnp.int32, (1, LANES), 1)
    zero = jnp.zeros((GRID_W, LANES), BF16)
    p_rows, pc_rows = [], []
    for i in range(NA_ROWS):
        rows = slice(i * GRID_W, (i + 1) * GRID_W)
        pieces = {}
        for p in range(NA_KROWS // 2):
            first = a_lo[i] <= 2 * p < a_lo[i] + NA_KH
            second = a_lo[i] <= 2 * p + 1 < a_lo[i] + NA_KH
            if not (first or second):
                continue
            piece = s[rows, p * LANES:(p + 1) * LANES] + t2_ref[j, base + 2 * p - i]
            if not (first and second):
                piece = jnp.where((plane < GRID_W) if first else (plane >= GRID_W), piece, NEG)
            pieces[p] = piece
        sc_i = sc[rows]
        m = jnp.max(sc_i, axis=-1, keepdims=True)
        for piece in pieces.values():
            m = jnp.maximum(m, jnp.max(piece, axis=-1, keepdims=True))
        p_rows.append(jnp.concatenate(
            [jnp.exp(pieces[p] - m).astype(BF16) if p in pieces else zero for p in range(NA_KROWS // 2)], axis=-1))
        pc_rows.append(jnp.exp(sc_i - m).astype(BF16))
    return jnp.concatenate(p_rows, axis=0), jnp.concatenate(pc_rows, axis=0)


def _na_body(q_ref, k_ref, v_ref, kc_ref, vc_ref, t2_ref, o_ref):
    blk = pl.program_id(2)
    w0 = jnp.clip(NA_ROWS * blk - NA_KH // 2, 0, GRID_W - NA_KROWS)
    start = pl.multiple_of(w0 * GRID_W, 256)
    q2 = q_ref[0] * jnp.asarray(HEAD_DIM ** -0.5, BF16)
    kw = k_ref[0, pl.ds(start, NA_KB), :]
    vw = v_ref[0, pl.ds(start, NA_KB), :]
    kc = kc_ref[0]
    vc = vc_ref[0]
    lane = lax.broadcasted_iota(jnp.int32, (1, NA_HG * HEAD_DIM), 1)
    nt = (((1,), (1,)), ((), ()))
    one = jnp.ones((), BF16)

    def run(probs_fn):
        out = None
        for j in range(NA_HG):
            in_head = (lane >= j * HEAD_DIM) & (lane < (j + 1) * HEAD_DIM)
            qm = jnp.where(in_head, q2, jnp.zeros_like(q2))
            s = lax.dot_general(qm, kw, nt, preferred_element_type=F32)
            sc = lax.dot_general(qm, kc, nt, preferred_element_type=F32)
            p, pc = probs_fn(j, s, sc)
            o = _dot(p, jnp.where(in_head, vw, one)) + _dot(pc, jnp.where(in_head, vc, one))
            o = o * (1.0 / pltpu.roll(o, HEAD_DIM, axis=1))
            out = o if out is None else jnp.where(in_head, o, out)
        o_ref[0] = out.astype(BF16)

    last = GRID_W // NA_ROWS - 1
    pl.when(blk == 0)(lambda: run(functools.partial(_na_probs, t2_ref, 0)))
    pl.when((blk > 0) & (blk < last))(lambda: run(functools.partial(_na_probs, t2_ref, 1)))
    pl.when(blk == last)(lambda: run(functools.partial(_na_probs, t2_ref, last)))


def _natten(qkv, qkv_c, t2):
    nb, n, _ = qkv.shape
    nctx = qkv_c.shape[1]
    ng = NA_HEADS // NA_HG
    lanes = NA_HG * HEAD_DIM
    return pl.pallas_call(
        _na_body,
        grid=(ng, nb, n // NA_QB),
        in_specs=[
            pl.BlockSpec((1, NA_QB, lanes), lambda h, b, i: (b, i, h)),
            pl.BlockSpec((1, n, lanes), lambda h, b, i: (b, 0, ng + h)),
            pl.BlockSpec((1, n, lanes), lambda h, b, i: (b, 0, 2 * ng + h)),
            pl.BlockSpec((1, nctx, lanes), lambda h, b, i: (b, 0, ng + h)),
            pl.BlockSpec((1, nctx, lanes), lambda h, b, i: (b, 0, 2 * ng + h)),
            pl.BlockSpec((NA_HG, NA_T2, GRID_W, 2 * GRID_W), lambda h, b, i: (h, 0, 0, 0)),
        ],
        out_specs=pl.BlockSpec((1, NA_QB, lanes), lambda h, b, i: (b, i, h)),
        out_shape=jax.ShapeDtypeStruct((nb, n, D_NA), BF16),
        compiler_params=_params("arbitrary", "arbitrary", "arbitrary"),
        name="natten",
    )(qkv, qkv, qkv, qkv_c, qkv_c, t2)


SEQ = GRID_W * GRID_W
FN_R = SUBLANES
FN_M = SEQ // FN_R
FN_TA = 1024
FN_LANES = 256
FN_CH = 16


def _dft_tables(n):
    idx = np.arange(n, dtype=np.int64)
    ang = 2.0 * np.pi * ((idx[:, None] * idx[None, :]) % n).astype(np.float64) / n
    scale = 1.0 / np.sqrt(n)
    return (np.cos(ang) * scale).astype(np.float32), (np.sin(ang) * scale).astype(np.float32)


def _seq_tables():
    k2 = np.arange(FN_M, dtype=np.int64)
    n2 = np.arange(FN_M, dtype=np.int64)
    out = np.zeros((FN_R, 3, FN_M, FN_M), np.float32)
    for n1 in range(FN_R):
        num = (k2[:, None] * n2[None, :] * FN_R + n1 * k2[:, None]) % SEQ
        ang = 2.0 * np.pi * num.astype(np.float64) / SEQ
        c = np.cos(ang) / np.sqrt(SEQ)
        s = np.sin(ang) / np.sqrt(SEQ)
        out[n1, 0] = c
        out[n1, 1] = c - s
        out[n1, 2] = -(c + s)
    return out


def _fnet_a_body(*refs):
    nx = D_MODEL // LANES
    x_refs = refs[:nx]
    sh_ref, sc_ref, g_ref, cs_ref, a_ref, b_ref = refs[nx:]
    gw = D_MODEL // FNET_GROUPS
    for s in range(FN_R):
        xs = jnp.concatenate([x[0, pl.ds(s, FN_TA // FN_R, stride=FN_R), :] for x in x_refs], axis=-1)
        xb = _modnorm(xs, g_ref[0], sh_ref[0], sc_ref[0])
        for grp in range(FNET_GROUPS):
            ab = _dot(xb[:, grp * gw:(grp + 1) * gw], cs_ref[...])
            a_ref[0, s, :, grp * gw:(grp + 1) * gw] = ab[:, :gw].astype(BF16)
            b_ref[0, s, :, grp * gw:(grp + 1) * gw] = ab[:, gw:].astype(BF16)


def _fnet_a(h, mod, row0, k0, g_all, g_row, cs):
    nb, n, _ = h.shape
    gw = D_MODEL // FNET_GROUPS
    nx = D_MODEL // LANES
    out_spec = pl.BlockSpec((1, FN_R, FN_TA // FN_R, D_MODEL), lambda b, m: (b, 0, m, 0))
    return pl.pallas_call(
        _fnet_a_body,
        grid=(nb, n // FN_TA),
        in_specs=[pl.BlockSpec((1, FN_TA, LANES), functools.partial(lambda b, m, c: (b, m, c), c=c))
                  for c in range(nx)]
        + [_mod_spec(row0, k0), _mod_spec(row0, k0 + 1), _const_row_spec(g_row),
           _resident((gw, 2 * gw), lambda b, m: (0, 0))],
        out_specs=[out_spec, out_spec],
        out_shape=[jax.ShapeDtypeStruct((nb, FN_R, n // FN_R, D_MODEL), BF16)] * 2,
        compiler_params=_params("arbitrary", "arbitrary"),
        name="fnet_channel_dft",
    )(*([h] * nx), mod, mod, g_all, cs)


def _fnet_b_body(zr_ref, zi_ref, m_ref, o_ref, v_ref):
    for n1 in range(FN_R):
        zr, zi = zr_ref[0, n1], zi_ref[0, n1]
        k1 = _dot(m_ref[n1, 0], (zr.astype(F32) + zi.astype(F32)).astype(BF16))
        k3 = _dot(m_ref[n1, 1], zi)
        k2 = _dot(m_ref[n1, 2], zr)
        v_ref[n1, :FN_M, :] = k1 - k3
        v_ref[n1, FN_M:, :] = k1 + k2

    rt = np.float32(np.sqrt(0.5))

    def chunk(i, carry):
        r0 = pl.multiple_of(i * FN_CH, FN_CH)
        for lt in range(FN_LANES // LANES):
            ls = slice(lt * LANES, (lt + 1) * LANES)
            re = [v_ref[n, pl.ds(r0, FN_CH), ls] for n in range(FN_R)]
            im = [v_ref[n, pl.ds(FN_M + r0, FN_CH), ls] for n in range(FN_R)]
            e0 = (re[0] + re[4]) + (re[2] + re[6])
            e2 = (re[0] + re[4]) - (re[2] + re[6])
            e1 = (re[0] - re[4]) + (im[2] - im[6])
            e3 = (re[0] - re[4]) - (im[2] - im[6])
            t0r, t0i = re[1] + re[5], im[1] + im[5]
            t1r, t1i = re[1] - re[5], im[1] - im[5]
            t2r, t2i = re[3] + re[7], im[3] + im[7]
            t3r, t3i = re[3] - re[7], im[3] - im[7]
            p0 = t0r + t2r
            p2 = t0i - t2i
            al = t1r - t3r
            be = t1i + t3i
            p1 = (al + be) * rt
            p3 = (be - al) * rt
            ys = (e0 + p0, e1 + p1, e2 + p2, e3 + p3, e0 - p0, e1 - p1, e2 - p2, e3 - p3)
            for k1 in range(FN_R):
                o_ref[0, pl.ds(k1 * FN_M + r0, FN_CH), ls] = ys[k1].astype(BF16)
        return carry

    lax.fori_loop(0, FN_M // FN_CH, chunk, 0)


def _fnet_b(zr, zi, mtab):
    assert FN_R == 8
    nb = zr.shape[0]
    z_spec = pl.BlockSpec((1, FN_R, FN_M, FN_LANES), lambda b, l: (b, 0, 0, l))
    return pl.pallas_call(
        _fnet_b_body,
        grid=(nb, D_MODEL // FN_LANES),
        in_specs=[z_spec, z_spec, _resident((FN_R, 3, FN_M, FN_M), lambda b, l: (0, 0, 0, 0))],
        out_specs=pl.BlockSpec((1, SEQ, FN_LANES), lambda b, l: (b, 0, l)),
        out_shape=jax.ShapeDtypeStruct((nb, SEQ, D_MODEL), BF16),
        scratch_shapes=[pltpu.VMEM((FN_R, 2 * FN_M, FN_LANES), F32)],
        compiler_params=_params("arbitrary", "arbitrary"),
        name="fnet_seq_dft",
    )(zr, zi, mtab)


def kernel(x, c, ctx, c_ctx, ada_w, ada_b, norm_g, ffn_w_in, ffn_w_out, ab_w_in, conv_w, conv_b,
           conv_ln_g, conv_ln_b, na_rpb, ab_w_out, fnet_w, fnet_b, final_g):
    nb, n, d = x.shape
    depth = ada_w.shape[0]
    nctx = ctx.shape[1]
    assert (d, depth, nb) == (D_MODEL, 2, 4) and n == GRID_W * GRID_W

    cc = jnp.concatenate([c, c_ctx[None], jnp.zeros((8 - nb - 1, d), F32)], axis=0)
    mod = _ada(cc, ada_w, ada_b).reshape(depth * 8, 1, N_MOD * d)
    ctx_row = nb

    g_all = norm_g.reshape(depth * 3, 1, d)
    fg = final_g.reshape(1, 1, d)
    w_in = ffn_w_in.reshape(depth * 2, d, 2 * D_FF)
    w_out = ffn_w_out.reshape(depth * 2, D_FF, d)

    h = _ffn(x, mod, 0, 0, g_all, 0, w_in, w_out, 0, fg, False)
    hc = _ffn(ctx.reshape(1, nb * nctx, d), mod, ctx_row, 0, g_all, 0, w_in, w_out, 0, fg, False)
    assert ab_w_in.shape[0] == 1
    w_ab = ab_w_in.reshape(ab_w_in.shape[1:])
    u, qkv = _proj(h, mod, 0, 3, g_all, 1, w_ab)
    qkv_c = _qkv_proj(hc, mod, ctx_row, 3, g_all, 1, w_ab)
    conv_x = _conv(u, conv_w[0], conv_b[0], conv_ln_g[0], conv_ln_b[0])
    att_x = _natten(qkv, qkv_c.reshape(nb, nctx, 3 * D_NA), _na_bias_table(na_rpb[0]))
    h = _ffn(h, mod, 0, 6, g_all, 2, w_in, w_out, 1, fg, False,
             mix=((conv_x, att_x), ab_w_out[0].astype(BF16), None, 5))

    h = _ffn(h, mod, 8, 0, g_all, 3, w_in, w_out, 2, fg, False)
    gw = d // FNET_GROUPS
    cc_tab, sc_tab = _dft_tables(gw)
    cs = jnp.asarray(np.concatenate([cc_tab, -sc_tab], axis=1)).astype(BF16)
    zr, zi = _fnet_a(h, mod, 8, 3, g_all, 4, cs)
    f = _fnet_b(zr, zi, jnp.asarray(_seq_tables()).astype(BF16))
    return _ffn(h, mod, 8, 6, g_all, 5, w_in, w_out, 3, fg, True,
                mix=((f,), fnet_w[0].astype(BF16), fnet_b[0].reshape(1, d), 5))
```

```python
import functools

import numpy as np
import jax
import jax.numpy as jnp
from jax import lax
from jax.experimental import pallas as pl
from jax.experimental.pallas import tpu as pltpu

D_MODEL = 1024
GRID_W = 64
D_CONV = 512
D_NA = 512
NA_HEADS = 8
HEAD_DIM = 64
CONV_WIDTH = 31
NA_KH = 8
NA_KW = 16
FNET_GROUPS = 4
D_FF = 2816
N_MOD = 9
EPS = 1e-6

BF16 = jnp.bfloat16
F32 = jnp.float32

VMEM_LIMIT = 56 * 1024 * 1024
FFN_VMEM_LIMIT = 60 * 1024 * 1024
SUBLANES = 8
LANES = 128
TM = 1024
FF_CHUNK = 256
CONV_TL = 1024
CONV_HALO = 16
NA_ROWS = 8
NA_QB = NA_ROWS * GRID_W
NA_KROWS = 16
NA_KB = NA_KROWS * GRID_W
NEG = -1e30


def _params(*sem):
    return pltpu.CompilerParams(dimension_semantics=sem, vmem_limit_bytes=VMEM_LIMIT)


def _resident(shape, index_map):
    return pl.BlockSpec(shape, index_map, pipeline_mode=pl.Buffered(1))


def _silu(x):
    return x * (1.0 / (1.0 + jnp.exp(-x)))


def _dot(a, b):
    return jnp.dot(a, b, preferred_element_type=F32)


def _rms(x, g):
    return x * lax.rsqrt(jnp.mean(x * x, axis=-1, keepdims=True) + EPS) * g


def _modnorm(x, g, shift, scale):
    return (_rms(x, g) * (1.0 + scale) + shift).astype(BF16)


ADA_TN = 768
ADA_SPLIT = 3


def _ada_body(cc_ref, *refs):
    w_refs, b_ref, o_ref = refs[:ADA_SPLIT], refs[ADA_SPLIT], refs[ADA_SPLIT + 1]
    s = _silu(cc_ref[...]).astype(BF16)
    for q, w_ref in enumerate(w_refs):
        cols = slice(q * ADA_TN, (q + 1) * ADA_TN)
        o_ref[0, :, cols] = _dot(s, w_ref[0].astype(BF16)) + b_ref[0, :, cols]


def _ada(cc, ada_w, ada_b):
    depth, _, n = ada_w.shape
    step = ADA_SPLIT * ADA_TN

    def w_spec(q):
        return pl.BlockSpec((1, D_MODEL, ADA_TN), lambda i, j: (i, 0, ADA_SPLIT * j + q))

    return pl.pallas_call(
        _ada_body,
        grid=(depth, n // step),
        in_specs=[pl.BlockSpec((8, D_MODEL), lambda i, j: (0, 0))]
        + [w_spec(q) for q in range(ADA_SPLIT)]
        + [pl.BlockSpec((1, 1, step), lambda i, j: (i, 0, j))],
        out_specs=pl.BlockSpec((1, 8, step), lambda i, j: (i, 0, j)),
        out_shape=jax.ShapeDtypeStruct((depth, 8, n), F32),
        compiler_params=_params("arbitrary", "arbitrary"),
        name="ada_mod",
    )(cc, *([ada_w] * ADA_SPLIT), ada_b.reshape(depth, 1, n))


def _mod_spec(row0, k):
    return pl.BlockSpec((1, 1, D_MODEL), lambda b, m: (row0 + b, 0, k))


def _const_row_spec(row):
    return pl.BlockSpec((1, 1, D_MODEL), lambda b, m: (row, 0, 0))


W_SLOTS = 2


def _weight_copies(win_hbm, wout_hbm, st_in, st_out, sem, w_idx, j, slot):
    lo = j * FF_CHUNK
    return (
        pltpu.make_async_copy(win_hbm.at[w_idx, :, pl.ds(lo, FF_CHUNK)], st_in.at[slot, 0], sem.at[slot, 0]),
        pltpu.make_async_copy(win_hbm.at[w_idx, :, pl.ds(D_FF + lo, FF_CHUNK)], st_in.at[slot, 1], sem.at[slot, 1]),
        pltpu.make_async_copy(wout_hbm.at[w_idx, pl.ds(lo, FF_CHUNK), :], st_out.at[slot], sem.at[slot, 2]),
    )


def _start_stream(copies):
    for j in range(W_SLOTS):
        for cp in copies(j, j):
            cp.start()


def _land_chunk(copies, j, win_ref, wout_ref, st_in, st_out):
    slot, lo = j % W_SLOTS, j * FF_CHUNK
    for cp in copies(j, slot):
        cp.wait()
    win_ref[:, lo:lo + FF_CHUNK] = st_in[slot, 0].astype(BF16)
    win_ref[:, D_FF + lo:D_FF + lo + FF_CHUNK] = st_in[slot, 1].astype(BF16)
    wout_ref[lo:lo + FF_CHUNK, :] = st_out[slot].astype(BF16)
    if j + W_SLOTS < D_FF // FF_CHUNK:
        for cp in copies(j + W_SLOTS, slot):
            cp.start()


def _ffn_compute(stream, x_ref, sh_ref, sc_ref, gt_ref, g_ref, win_hbm, wout_hbm, fg_ref, mix_refs,
                 o_ref, mid_ref, win_ref, wout_ref, st_in, st_out, sem, *, final, n_mix, mix_bias, w_idx):
    copies = functools.partial(_weight_copies, win_hbm, wout_hbm, st_in, st_out, sem, w_idx)
    if stream:
        _start_stream(copies)
    x = x_ref[0]
    if n_mix:
        acts, mg_ref, mw_ref = mix_refs[:n_mix], mix_refs[n_mix], mix_refs[n_mix + 1]
        y0, r = None, 0
        for a_ref in acts:
            k = a_ref.shape[-1]
            t = _dot(a_ref[0], mw_ref[r:r + k, :])
            y0 = t if y0 is None else y0 + t
            r += k
        if mix_bias:
            y0 = y0 + mix_refs[n_mix + 2][...]
        x = x + mg_ref[0] * y0
    xb = _modnorm(x, g_ref[0], sh_ref[0], sc_ref[0])
    for j in range(D_FF // FF_CHUNK):
        lo = j * FF_CHUNK
        if stream:
            _land_chunk(copies, j, win_ref, wout_ref, st_in, st_out)
        gate = _dot(xb, win_ref[:, lo:lo + FF_CHUNK])
        up = _dot(xb, win_ref[:, D_FF + lo:D_FF + lo + FF_CHUNK])
        mid_ref[:, lo:lo + FF_CHUNK] = (_silu(gate) * up).astype(BF16)
    y = _dot(mid_ref[...], wout_ref[...])
    h = x + (0.5 * gt_ref[0]) * y
    if final:
        h = _rms(h, fg_ref[0])
    o_ref[0] = h


def _preload_weights(win_hbm, wout_hbm, win_ref, wout_ref, st_in, st_out, sem, w_idx):
    copies = functools.partial(_weight_copies, win_hbm, wout_hbm, st_in, st_out, sem, w_idx)
    _start_stream(copies)
    for j in range(D_FF // FF_CHUNK):
        _land_chunk(copies, j, win_ref, wout_ref, st_in, st_out)


def _ffn_body(x_ref, sh_ref, sc_ref, gt_ref, g_ref, win_hbm, wout_hbm, fg_ref, *rest, single_step, **static):
    args = (x_ref, sh_ref, sc_ref, gt_ref, g_ref, win_hbm, wout_hbm, fg_ref, rest[:-7], *rest[-7:])
    is_first = (pl.program_id(0) == 0) & (pl.program_id(1) == 0)
    if single_step:
        _ffn_compute(True, *args, **static)
    elif static["n_mix"]:
        pl.when(is_first)(lambda: _preload_weights(win_hbm, wout_hbm, *rest[-5:], static["w_idx"]))
        _ffn_compute(False, *args, **static)
    else:
        pl.when(is_first)(lambda: _ffn_compute(True, *args, **static))
        pl.when(jnp.logical_not(is_first))(lambda: _ffn_compute(False, *args, **static))


def _ffn(h, mod, row0, k0, g_all, g_row, w_in, w_out, w_idx, final_g, final, mix=None):
    nb, n, _ = h.shape
    tm = min(TM, n)
    in_specs = [
        pl.BlockSpec((1, tm, D_MODEL), lambda b, m: (b, m, 0)),
        _mod_spec(row0, k0), _mod_spec(row0, k0 + 1), _mod_spec(row0, k0 + 2),
        _const_row_spec(g_row),
        pl.BlockSpec(memory_space=pl.ANY),
        pl.BlockSpec(memory_space=pl.ANY),
        _const_row_spec(0),
    ]
    args = [h, mod, mod, mod, g_all, w_in, w_out, final_g]
    n_mix, mix_bias = 0, False
    if mix is not None:
        acts, mw, mb, mk = mix
        n_mix, mix_bias = len(acts), mb is not None
        in_specs += [pl.BlockSpec((1, tm, a.shape[-1]), lambda b, m: (b, m, 0)) for a in acts]
        in_specs += [_mod_spec(row0, mk), _resident(mw.shape, lambda b, m: (0, 0))]
        args += [*acts, mod, mw]
        if mix_bias:
            in_specs.append(pl.BlockSpec((1, D_MODEL), lambda b, m: (0, 0)))
            args.append(mb)
    return pl.pallas_call(
        functools.partial(_ffn_body, final=final, n_mix=n_mix, mix_bias=mix_bias, w_idx=w_idx,
                          single_step=(nb * (n // tm) == 1)),
        grid=(nb, n // tm),
        in_specs=in_specs,
        out_specs=pl.BlockSpec((1, tm, D_MODEL), lambda b, m: (b, m, 0)),
        out_shape=jax.ShapeDtypeStruct(h.shape, F32),
        scratch_shapes=[
            pltpu.VMEM((tm, D_FF), BF16),
            pltpu.VMEM((D_MODEL, 2 * D_FF), BF16),
            pltpu.VMEM((D_FF, D_MODEL), BF16),
            pltpu.VMEM((W_SLOTS, 2, D_MODEL, FF_CHUNK), F32),
            pltpu.VMEM((W_SLOTS, FF_CHUNK, D_MODEL), F32),
            pltpu.SemaphoreType.DMA((W_SLOTS, 3)),
        ],
        compiler_params=pltpu.CompilerParams(dimension_semantics=("arbitrary", "arbitrary"),
                                             vmem_limit_bytes=FFN_VMEM_LIMIT),
        name="ffn_final" if final else "ffn",
    )(*args)


def _proj_body(x_ref, sh_ref, sc_ref, g_ref, w_ref, u_ref, qkv_ref, wbf_ref):
    @pl.when((pl.program_id(0) == 0) & (pl.program_id(1) == 0))
    def _():
        wbf_ref[...] = w_ref[...].astype(BF16)

    xb = _modnorm(x_ref[0], g_ref[0], sh_ref[0], sc_ref[0])
    u_ref[0] = _dot(xb, wbf_ref[:, :2 * D_CONV])
    qkv_ref[0] = _dot(xb, wbf_ref[:, 2 * D_CONV:]).astype(BF16)


def _proj(h, mod, row0, k0, g_all, g_row, w):
    nb, n, _ = h.shape
    return pl.pallas_call(
        _proj_body,
        grid=(nb, n // TM),
        in_specs=[
            pl.BlockSpec((1, TM, D_MODEL), lambda b, m: (b, m, 0)),
            _mod_spec(row0, k0), _mod_spec(row0, k0 + 1),
            _const_row_spec(g_row),
            _resident(w.shape, lambda b, m: (0, 0)),
        ],
        out_specs=[
            pl.BlockSpec((1, TM, 2 * D_CONV), lambda b, m: (b, m, 0)),
            pl.BlockSpec((1, TM, 3 * D_NA), lambda b, m: (b, m, 0)),
        ],
        out_shape=[
            jax.ShapeDtypeStruct((nb, n, 2 * D_CONV), F32),
            jax.ShapeDtypeStruct((nb, n, 3 * D_NA), BF16),
        ],
        scratch_shapes=[pltpu.VMEM(w.shape, BF16)],
        compiler_params=_params("arbitrary", "arbitrary"),
        name="ab_proj",
    )(h, mod, mod, g_all, w)


def _qkv_body(x_ref, sh_ref, sc_ref, g_ref, w_ref, qkv_ref):
    xb = _modnorm(x_ref[0], g_ref[0], sh_ref[0], sc_ref[0])
    qkv_ref[0] = _dot(xb, w_ref[:, 2 * D_CONV:].astype(BF16)).astype(BF16)


def _qkv_proj(h, mod, row0, k0, g_all, g_row, w):
    nb, n, _ = h.shape
    tm = min(TM, n)
    return pl.pallas_call(
        _qkv_body,
        grid=(nb, n // tm),
        in_specs=[
            pl.BlockSpec((1, tm, D_MODEL), lambda b, m: (b, m, 0)),
            _mod_spec(row0, k0), _mod_spec(row0, k0 + 1),
            _const_row_spec(g_row),
            _resident(w.shape, lambda b, m: (0, 0)),
        ],
        out_specs=pl.BlockSpec((1, tm, 3 * D_NA), lambda b, m: (b, m, 0)),
        out_shape=jax.ShapeDtypeStruct((nb, n, 3 * D_NA), BF16),
        compiler_params=_params("arbitrary", "arbitrary"),
        name="ctx_qkv",
    )(h, mod, mod, g_all, w)


def _glu(v):
    return v[:, :D_CONV] * (1.0 / (1.0 + jnp.exp(-v[:, D_CONV:])))


CONV_RC = 64
CONV_N = CONV_TL + 2 * CONV_HALO


def _conv_body(cur_ref, prev_ref, next_ref, w_ref, b_ref, lg_ref, lb_ref, o_ref, y_ref):
    t = pl.program_id(1)
    nt = pl.num_programs(1)
    y_ref[0, CONV_HALO:CONV_HALO + CONV_TL, :] = _glu(cur_ref[0])
    y_ref[0, 0:CONV_HALO, :] = jnp.where(t > 0, _glu(prev_ref[0]), 0.0)
    y_ref[0, CONV_HALO + CONV_TL:, :] = jnp.where(t < nt - 1, _glu(next_ref[0]), 0.0)
    for s in range(1, SUBLANES):
        y_ref[s, 0:CONV_N - SUBLANES, :] = y_ref[0, s:s + CONV_N - SUBLANES, :]
    off = CONV_HALO - CONV_WIDTH // 2
    for r in range(0, CONV_TL, CONV_RC):
        acc = jnp.zeros((CONV_RC // SUBLANES, SUBLANES, D_CONV), F32)
        for k in range(CONV_WIDTH):
            m8, s = divmod(off + k, SUBLANES)
            lo = r + SUBLANES * m8
            yk = y_ref[s, lo:lo + CONV_RC, :].reshape(CONV_RC // SUBLANES, SUBLANES, D_CONV)
            acc = acc + w_ref[k][None] * yk
        acc = acc.reshape(CONV_RC, D_CONV) + b_ref[...]
        mu = jnp.mean(acc, axis=-1, keepdims=True)
        cen = acc - mu
        var = jnp.mean(cen * cen, axis=-1, keepdims=True)
        z = cen * lax.rsqrt(var + EPS) * lg_ref[...] + lb_ref[...]
        o_ref[0, r:r + CONV_RC, :] = _silu(z).astype(BF16)


def _conv(u, w, b, ln_g, ln_b):
    nb, n, _ = u.shape
    nt = n // CONV_TL
    hb = CONV_TL // CONV_HALO
    last = n // CONV_HALO - 1
    row = lambda v: v.reshape(1, D_CONV)
    w_rep = jnp.broadcast_to(w[:, None, :], (CONV_WIDTH, SUBLANES, D_CONV))
    return pl.pallas_call(
        _conv_body,
        grid=(nb, nt),
        in_specs=[
            pl.BlockSpec((1, CONV_TL, 2 * D_CONV), lambda b_, t: (b_, t, 0)),
            pl.BlockSpec((1, CONV_HALO, 2 * D_CONV), lambda b_, t: (b_, jnp.maximum(t * hb - 1, 0), 0)),
            pl.BlockSpec((1, CONV_HALO, 2 * D_CONV), lambda b_, t: (b_, jnp.minimum((t + 1) * hb, last), 0)),
            pl.BlockSpec((CONV_WIDTH, SUBLANES, D_CONV), lambda b_, t: (0, 0, 0)),
            pl.BlockSpec((1, D_CONV), lambda b_, t: (0, 0)),
            pl.BlockSpec((1, D_CONV), lambda b_, t: (0, 0)),
            pl.BlockSpec((1, D_CONV), lambda b_, t: (0, 0)),
        ],
        out_specs=pl.BlockSpec((1, CONV_TL, D_CONV), lambda b_, t: (b_, t, 0)),
        out_shape=jax.ShapeDtypeStruct((nb, n, D_CONV), BF16),
        scratch_shapes=[pltpu.VMEM((SUBLANES, CONV_N, D_CONV), F32)],
        compiler_params=_params("arbitrary", "arbitrary"),
        name="conv_module",
    )(u, u, u, w_rep, row(b), row(ln_g), row(ln_b))


NA_HG = 4
NA_DR_PAD = 8
NA_T2 = 2 * NA_KH - 1 + 2 * NA_DR_PAD - 1


def _na_bias_table(rpb):
    qc = np.arange(GRID_W)
    cs = np.clip(qc - NA_KW // 2, 0, GRID_W - NA_KW)
    kc = np.arange(GRID_W)
    col_valid = (kc[None, :] >= cs[:, None]) & (kc[None, :] < cs[:, None] + NA_KW)
    dc = kc[None, :] - qc[:, None] + NA_KW - 1
    oh_c = np.zeros((2 * NA_KW - 1, GRID_W, GRID_W), np.float32)
    qi, ki = np.nonzero(col_valid)
    oh_c[dc[qi, ki], qi, ki] = 1.0
    t_col = jnp.einsum('hrd,dqk->hrqk', rpb, jnp.asarray(oh_c), precision=lax.Precision.HIGHEST)
    t_col = jnp.where(jnp.asarray(col_valid)[None, None], t_col, NEG)
    t_pad = jnp.pad(t_col, ((0, 0), (NA_DR_PAD, NA_DR_PAD), (0, 0), (0, 0)))
    return jnp.concatenate([t_pad[:, :NA_T2], t_pad[:, 1:NA_T2 + 1]], axis=-1)


def _na_window(blk):
    w0 = min(max(NA_ROWS * blk - NA_KH // 2, 0), GRID_W - NA_KROWS)
    a_lo = [min(max(NA_ROWS * blk + i - NA_KH // 2, 0), GRID_W - NA_KH) - w0 for i in range(NA_ROWS)]
    return w0, a_lo


def _na_probs(t2_ref, blk, j, s, sc):
    w0, a_lo = _na_window(blk)
    base = w0 - NA_ROWS * blk + NA_KH - 1 + NA_DR_PAD
    plane = lax.broadcasted_iota(jnp.int32, (1, LANES), 1)
    zero = jnp.zeros((GRID_W, LANES), BF16)
    p_rows, pc_rows = [], []
    for i in range(NA_ROWS):
        rows = slice(i * GRID_W, (i + 1) * GRID_W)
        pieces = {}
        for p in range(NA_KROWS // 2):
            first = a_lo[i] <= 2 * p < a_lo[i] + NA_KH
            second = a_lo[i] <= 2 * p + 1 < a_lo[i] + NA_KH
            if not (first or second):
                continue
            piece = s[rows, p * LANES:(p + 1) * LANES] + t2_ref[j, base + 2 * p - i]
            if not (first and second):
                piece = jnp.where((plane < GRID_W) if first else (plane >= GRID_W), piece, NEG)
            pieces[p] = piece
        sc_i = sc[rows]
        m = jnp.max(sc_i, axis=-1, keepdims=True)
        for piece in pieces.values():
            m = jnp.maximum(m, jnp.max(piece, axis=-1, keepdims=True))
        p_rows.append(jnp.concatenate(
            [jnp.exp(pieces[p] - m).astype(BF16) if p in pieces else zero for p in range(NA_KROWS // 2)], axis=-1))
        pc_rows.append(jnp.exp(sc_i - m).astype(BF16))
    return jnp.concatenate(p_rows, axis=0), jnp.concatenate(pc_rows, axis=0)


def _na_body(q_ref, k_ref, v_ref, kc_ref, vc_ref, t2_ref, o_ref):
    blk = pl.program_id(2)
    w0 = jnp.clip(NA_ROWS * blk - NA_KH // 2, 0, GRID_W - NA_KROWS)
    start = pl.multiple_of(w0 * GRID_W, 256)
    q2 = q_ref[0] * jnp.asarray(HEAD_DIM ** -0.5, BF16)
    kw = k_ref[0, pl.ds(start, NA_KB), :]
    vw = v_ref[0, pl.ds(start, NA_KB), :]
    kc = kc_ref[0]
    vc = vc_ref[0]
    lane = lax.broadcasted_iota(jnp.int32, (1, NA_HG * HEAD_DIM), 1)
    nt = (((1,), (1,)), ((), ()))
    one = jnp.ones((), BF16)

    def run(probs_fn):
        out = None
        for j in range(NA_HG):
            in_head = (lane >= j * HEAD_DIM) & (lane < (j + 1) * HEAD_DIM)
            qm = jnp.where(in_head, q2, jnp.zeros_like(q2))
            s = lax.dot_general(qm, kw, nt, preferred_element_type=F32)
            sc = lax.dot_general(qm, kc, nt, preferred_element_type=F32)
            p, pc = probs_fn(j, s, sc)
            o = _dot(p, jnp.where(in_head, vw, one)) + _dot(pc, jnp.where(in_head, vc, one))
            o = o * (1.0 / pltpu.roll(o, HEAD_DIM, axis=1))
            out = o if out is None else jnp.where(in_head, o, out)
        o_ref[0] = out.astype(BF16)

    last = GRID_W // NA_ROWS - 1
    pl.when(blk == 0)(lambda: run(functools.partial(_na_probs, t2_ref, 0)))
    pl.when((blk > 0) & (blk < last))(lambda: run(functools.partial(_na_probs, t2_ref, 1)))
    pl.when(blk == last)(lambda: run(functools.partial(_na_probs, t2_ref, last)))


def _natten(qkv, qkv_c, t2):
    nb, n, _ = qkv.shape
    nctx = qkv_c.shape[1]
    ng = NA_HEADS // NA_HG
    lanes = NA_HG * HEAD_DIM
    return pl.pallas_call(
        _na_body,
        grid=(ng, nb, n // NA_QB),
        in_specs=[
            pl.BlockSpec((1, NA_QB, lanes), lambda h, b, i: (b, i, h)),
            pl.BlockSpec((1, n, lanes), lambda h, b, i: (b, 0, ng + h)),
            pl.BlockSpec((1, n, lanes), lambda h, b, i: (b, 0, 2 * ng + h)),
            pl.BlockSpec((1, nctx, lanes), lambda h, b, i: (b, 0, ng + h)),
            pl.BlockSpec((1, nctx, lanes), lambda h, b, i: (b, 0, 2 * ng + h)),
            pl.BlockSpec((NA_HG, NA_T2, GRID_W, 2 * GRID_W), lambda h, b, i: (h, 0, 0, 0)),
        ],
        out_specs=pl.BlockSpec((1, NA_QB, lanes), lambda h, b, i: (b, i, h)),
        out_shape=jax.ShapeDtypeStruct((nb, n, D_NA), BF16),
        compiler_params=_params("arbitrary", "arbitrary", "arbitrary"),
        name="natten",
    )(qkv, qkv, qkv, qkv_c, qkv_c, t2)


SEQ = GRID_W * GRID_W
FN_R = SUBLANES
FN_M = SEQ // FN_R
FN_TA = 2048
FN_LANES = 256
FN_CH = 16


def _dft_tables(n):
    idx = np.arange(n, dtype=np.int64)
    ang = 2.0 * np.pi * ((idx[:, None] * idx[None, :]) % n).astype(np.float64) / n
    scale = 1.0 / np.sqrt(n)
    return (np.cos(ang) * scale).astype(np.float32), (np.sin(ang) * scale).astype(np.float32)


def _seq_tables():
    k2 = np.arange(FN_M, dtype=np.int64)
    n2 = np.arange(FN_M, dtype=np.int64)
    out = np.zeros((FN_R, 3, FN_M, FN_M), np.float32)
    for n1 in range(FN_R):
        num = (k2[:, None] * n2[None, :] * FN_R + n1 * k2[:, None]) % SEQ
        ang = 2.0 * np.pi * num.astype(np.float64) / SEQ
        c = np.cos(ang) / np.sqrt(SEQ)
        s = np.sin(ang) / np.sqrt(SEQ)
        out[n1, 0] = c
        out[n1, 1] = c - s
        out[n1, 2] = -(c + s)
    return out


def _fnet_a_body(*refs):
    nx = D_MODEL // LANES
    x_refs = refs[:nx]
    sh_ref, sc_ref, g_ref, cs_ref, a_ref, b_ref = refs[nx:]
    gw = D_MODEL // FNET_GROUPS
    for s in range(FN_R):
        xs = jnp.concatenate([x[0, pl.ds(s, FN_TA // FN_R, stride=FN_R), :] for x in x_refs], axis=-1)
        xb = _modnorm(xs, g_ref[0], sh_ref[0], sc_ref[0])
        for grp in range(FNET_GROUPS):
            ab = _dot(xb[:, grp * gw:(grp + 1) * gw], cs_ref[...])
            a_ref[0, s, :, grp * gw:(grp + 1) * gw] = ab[:, :gw].astype(BF16)
            b_ref[0, s, :, grp * gw:(grp + 1) * gw] = ab[:, gw:].astype(BF16)


def _fnet_a(h, mod, row0, k0, g_all, g_row, cs):
    nb, n, _ = h.shape
    gw = D_MODEL // FNET_GROUPS
    nx = D_MODEL // LANES
    out_spec = pl.BlockSpec((1, FN_R, FN_TA // FN_R, D_MODEL), lambda b, m: (b, 0, m, 0))
    return pl.pallas_call(
        _fnet_a_body,
        grid=(nb, n // FN_TA),
        in_specs=[pl.BlockSpec((1, FN_TA, LANES), functools.partial(lambda b, m, c: (b, m, c), c=c))
                  for c in range(nx)]
        + [_mod_spec(row0, k0), _mod_spec(row0, k0 + 1), _const_row_spec(g_row),
           _resident((gw, 2 * gw), lambda b, m: (0, 0))],
        out_specs=[out_spec, out_spec],
        out_shape=[jax.ShapeDtypeStruct((nb, FN_R, n // FN_R, D_MODEL), BF16)] * 2,
        compiler_params=_params("arbitrary", "arbitrary"),
        name="fnet_channel_dft",
    )(*([h] * nx), mod, mod, g_all, cs)


def _fnet_b_body(zr_ref, zi_ref, m_ref, o_ref, v_ref):
    for n1 in range(FN_R):
        zr, zi = zr_ref[0, n1], zi_ref[0, n1]
        k1 = _dot(m_ref[n1, 0], (zr.astype(F32) + zi.astype(F32)).astype(BF16))
        k3 = _dot(m_ref[n1, 1], zi)
        k2 = _dot(m_ref[n1, 2], zr)
        v_ref[n1, :FN_M, :] = k1 - k3
        v_ref[n1, FN_M:, :] = k1 + k2

    rt = np.float32(np.sqrt(0.5))

    def chunk(i, carry):
        r0 = pl.multiple_of(i * FN_CH, FN_CH)
        for lt in range(FN_LANES // LANES):
            ls = slice(lt * LANES, (lt + 1) * LANES)
            re = [v_ref[n, pl.ds(r0, FN_CH), ls] for n in range(FN_R)]
            im = [v_ref[n, pl.ds(FN_M + r0, FN_CH), ls] for n in range(FN_R)]
            e0 = (re[0] + re[4]) + (re[2] + re[6])
            e2 = (re[0] + re[4]) - (re[2] + re[6])
            e1 = (re[0] - re[4]) + (im[2] - im[6])
            e3 = (re[0] - re[4]) - (im[2] - im[6])
            t0r, t0i = re[1] + re[5], im[1] + im[5]
            t1r, t1i = re[1] - re[5], im[1] - im[5]
            t2r, t2i = re[3] + re[7], im[3] + im[7]
            t3r, t3i = re[3] - re[7], im[3] - im[7]
            p0 = t0r + t2r
            p2 = t0i - t2i
            al = t1r - t3r
            be = t1i + t3i
            p1 = (al + be) * rt
            p3 = (be - al) * rt
            ys = (e0 + p0, e1 + p1, e2 + p2, e3 + p3, e0 - p0, e1 - p1, e2 - p2, e3 - p3)
            for k1 in range(FN_R):
                o_ref[0, pl.ds(k1 * FN_M + r0, FN_CH), ls] = ys[k1].astype(BF16)
        return carry

    lax.fori_loop(0, FN_M // FN_CH, chunk, 0)


def _fnet_b(zr, zi, mtab):
    assert FN_R == 8
    nb = zr.shape[0]
    z_spec = pl.BlockSpec((1, FN_R, FN_M, FN_LANES), lambda b, l: (b, 0, 0, l))
    return pl.pallas_call(
        _fnet_b_body,
        grid=(nb, D_MODEL // FN_LANES),
        in_specs=[z_spec, z_spec, _resident((FN_R, 3, FN_M, FN_M), lambda b, l: (0, 0, 0, 0))],
        out_specs=pl.BlockSpec((1, SEQ, FN_LANES), lambda b, l: (b, 0, l)),
        out_shape=jax.ShapeDtypeStruct((nb, SEQ, D_MODEL), BF16),
        scratch_shapes=[pltpu.VMEM((FN_R, 2 * FN_M, FN_LANES), F32)],
        compiler_params=_params("arbitrary", "arbitrary"),
        name="fnet_seq_dft",
    )(zr, zi, mtab)


def kernel(x, c, ctx, c_ctx, ada_w, ada_b, norm_g, ffn_w_in, ffn_w_out, ab_w_in, conv_w, conv_b,
           conv_ln_g, conv_ln_b, na_rpb, ab_w_out, fnet_w, fnet_b, final_g):
    nb, n, d = x.shape
    depth = ada_w.shape[0]
    nctx = ctx.shape[1]
    assert (d, depth, nb) == (D_MODEL, 2, 4) and n == GRID_W * GRID_W

    cc = jnp.concatenate([c, c_ctx[None], jnp.zeros((8 - nb - 1, d), F32)], axis=0)
    mod = _ada(cc, ada_w, ada_b).reshape(depth * 8, 1, N_MOD * d)
    ctx_row = nb

    g_all = norm_g.reshape(depth * 3, 1, d)
    fg = final_g.reshape(1, 1, d)
    w_in = ffn_w_in.reshape(depth * 2, d, 2 * D_FF)
    w_out = ffn_w_out.reshape(depth * 2, D_FF, d)

    h = _ffn(x, mod, 0, 0, g_all, 0, w_in, w_out, 0, fg, False)
    hc = _ffn(ctx.reshape(1, nb * nctx, d), mod, ctx_row, 0, g_all, 0, w_in, w_out, 0, fg, False)
    assert ab_w_in.shape[0] == 1
    w_ab = ab_w_in.reshape(ab_w_in.shape[1:])
    u, qkv = _proj(h, mod, 0, 3, g_all, 1, w_ab)
    qkv_c = _qkv_proj(hc, mod, ctx_row, 3, g_all, 1, w_ab)
    conv_x = _conv(u, conv_w[0], conv_b[0], conv_ln_g[0], conv_ln_b[0])
    att_x = _natten(qkv, qkv_c.reshape(nb, nctx, 3 * D_NA), _na_bias_table(na_rpb[0]))
    h = _ffn(h, mod, 0, 6, g_all, 2, w_in, w_out, 1, fg, False,
             mix=((conv_x, att_x), ab_w_out[0].astype(BF16), None, 5))

    h = _ffn(h, mod, 8, 0, g_all, 3, w_in, w_out, 2, fg, False)
    gw = d // FNET_GROUPS
    cc_tab, sc_tab = _dft_tables(gw)
    cs = jnp.asarray(np.concatenate([cc_tab, -sc_tab], axis=1)).astype(BF16)
    zr, zi = _fnet_a(h, mod, 8, 3, g_all, 4, cs)
    f = _fnet_b(zr, zi, jnp.asarray(_seq_tables()).astype(BF16))
    return _ffn(h, mod, 8, 6, g_all, 5, w_in, w_out, 3, fg, True,
                mix=((f,), fnet_w[0].astype(BF16), fnet_b[0].reshape(1, d), 5))
```

```python
import functools

import numpy as np
import jax
import jax.numpy as jnp
from jax import lax
from jax.experimental import pallas as pl
from jax.experimental.pallas import tpu as pltpu

D_MODEL = 1024
GRID_W = 64
D_CONV = 512
D_NA = 512
NA_HEADS = 8
HEAD_DIM = 64
CONV_WIDTH = 31
NA_KH = 8
NA_KW = 16
FNET_GROUPS = 4
D_FF = 2816
N_MOD = 9
EPS = 1e-6

BF16 = jnp.bfloat16
F32 = jnp.float32

VMEM_LIMIT = 56 * 1024 * 1024
FFN_VMEM_LIMIT = 60 * 1024 * 1024
SUBLANES = 8
LANES = 128
TM = 1024
FF_CHUNK = 256
CONV_TL = 512
CONV_HALO = 16
NA_ROWS = 8
NA_QB = NA_ROWS * GRID_W
NA_KROWS = 16
NA_KB = NA_KROWS * GRID_W
NEG = -1e30


def _params(*sem):
    return pltpu.CompilerParams(dimension_semantics=sem, vmem_limit_bytes=VMEM_LIMIT)


def _resident(shape, index_map):
    return pl.BlockSpec(shape, index_map, pipeline_mode=pl.Buffered(1))


def _silu(x):
    return x * (1.0 / (1.0 + jnp.exp(-x)))


def _dot(a, b):
    return jnp.dot(a, b, preferred_element_type=F32)


def _rms(x, g):
    return x * lax.rsqrt(jnp.mean(x * x, axis=-1, keepdims=True) + EPS) * g


def _modnorm(x, g, shift, scale):
    return (_rms(x, g) * (1.0 + scale) + shift).astype(BF16)


ADA_TN = 768
ADA_SPLIT = 3


def _ada_body(cc_ref, *refs):
    w_refs, b_ref, o_ref = refs[:ADA_SPLIT], refs[ADA_SPLIT], refs[ADA_SPLIT + 1]
    s = _silu(cc_ref[...]).astype(BF16)
    for q, w_ref in enumerate(w_refs):
        cols = slice(q * ADA_TN, (q + 1) * ADA_TN)
        o_ref[0, :, cols] = _dot(s, w_ref[0].astype(BF16)) + b_ref[0, :, cols]


def _ada(cc, ada_w, ada_b):
    depth, _, n = ada_w.shape
    step = ADA_SPLIT * ADA_TN

    def w_spec(q):
        return pl.BlockSpec((1, D_MODEL, ADA_TN), lambda i, j: (i, 0, ADA_SPLIT * j + q))

    return pl.pallas_call(
        _ada_body,
        grid=(depth, n // step),
        in_specs=[pl.BlockSpec((8, D_MODEL), lambda i, j: (0, 0))]
        + [w_spec(q) for q in range(ADA_SPLIT)]
        + [pl.BlockSpec((1, 1, step), lambda i, j: (i, 0, j))],
        out_specs=pl.BlockSpec((1, 8, step), lambda i, j: (i, 0, j)),
        out_shape=jax.ShapeDtypeStruct((depth, 8, n), F32),
        compiler_params=_params("arbitrary", "arbitrary"),
        name="ada_mod",
    )(cc, *([ada_w] * ADA_SPLIT), ada_b.reshape(depth, 1, n))


def _mod_spec(row0, k):
    return pl.BlockSpec((1, 1, D_MODEL), lambda b, m: (row0 + b, 0, k))


def _const_row_spec(row):
    return pl.BlockSpec((1, 1, D_MODEL), lambda b, m: (row, 0, 0))


W_SLOTS = 2


def _weight_copies(win_hbm, wout_hbm, st_in, st_out, sem, w_idx, j, slot):
    lo = j * FF_CHUNK
    return (
        pltpu.make_async_copy(win_hbm.at[w_idx, :, pl.ds(lo, FF_CHUNK)], st_in.at[slot, 0], sem.at[slot, 0]),
        pltpu.make_async_copy(win_hbm.at[w_idx, :, pl.ds(D_FF + lo, FF_CHUNK)], st_in.at[slot, 1], sem.at[slot, 1]),
        pltpu.make_async_copy(wout_hbm.at[w_idx, pl.ds(lo, FF_CHUNK), :], st_out.at[slot], sem.at[slot, 2]),
    )


def _start_stream(copies):
    for j in range(W_SLOTS):
        for cp in copies(j, j):
            cp.start()


def _land_chunk(copies, j, win_ref, wout_ref, st_in, st_out):
    slot, lo = j % W_SLOTS, j * FF_CHUNK
    for cp in copies(j, slot):
        cp.wait()
    win_ref[:, lo:lo + FF_CHUNK] = st_in[slot, 0].astype(BF16)
    win_ref[:, D_FF + lo:D_FF + lo + FF_CHUNK] = st_in[slot, 1].astype(BF16)
    wout_ref[lo:lo + FF_CHUNK, :] = st_out[slot].astype(BF16)
    if j + W_SLOTS < D_FF // FF_CHUNK:
        for cp in copies(j + W_SLOTS, slot):
            cp.start()


def _ffn_compute(stream, x_ref, sh_ref, sc_ref, gt_ref, g_ref, win_hbm, wout_hbm, fg_ref, mix_refs,
                 o_ref, mid_ref, win_ref, wout_ref, st_in, st_out, sem, *, final, n_mix, mix_bias, w_idx):
    copies = functools.partial(_weight_copies, win_hbm, wout_hbm, st_in, st_out, sem, w_idx)
    if stream:
        _start_stream(copies)
    x = x_ref[0]
    if n_mix:
        acts, mg_ref, mw_ref = mix_refs[:n_mix], mix_refs[n_mix], mix_refs[n_mix + 1]
        y0, r = None, 0
        for a_ref in acts:
            k = a_ref.shape[-1]
            t = _dot(a_ref[0], mw_ref[r:r + k, :])
            y0 = t if y0 is None else y0 + t
            r += k
        if mix_bias:
            y0 = y0 + mix_refs[n_mix + 2][...]
        x = x + mg_ref[0] * y0
    xb = _modnorm(x, g_ref[0], sh_ref[0], sc_ref[0])
    for j in range(D_FF // FF_CHUNK):
        lo = j * FF_CHUNK
        if stream:
            _land_chunk(copies, j, win_ref, wout_ref, st_in, st_out)
        gate = _dot(xb, win_ref[:, lo:lo + FF_CHUNK])
        up = _dot(xb, win_ref[:, D_FF + lo:D_FF + lo + FF_CHUNK])
        mid_ref[:, lo:lo + FF_CHUNK] = (_silu(gate) * up).astype(BF16)
    y = _dot(mid_ref[...], wout_ref[...])
    h = x + (0.5 * gt_ref[0]) * y
    if final:
        h = _rms(h, fg_ref[0])
    o_ref[0] = h


def _preload_weights(win_hbm, wout_hbm, win_ref, wout_ref, st_in, st_out, sem, w_idx):
    copies = functools.partial(_weight_copies, win_hbm, wout_hbm, st_in, st_out, sem, w_idx)
    _start_stream(copies)
    for j in range(D_FF // FF_CHUNK):
        _land_chunk(copies, j, win_ref, wout_ref, st_in, st_out)


def _ffn_body(x_ref, sh_ref, sc_ref, gt_ref, g_ref, win_hbm, wout_hbm, fg_ref, *rest, single_step, **static):
    args = (x_ref, sh_ref, sc_ref, gt_ref, g_ref, win_hbm, wout_hbm, fg_ref, rest[:-7], *rest[-7:])
    is_first = (pl.program_id(0) == 0) & (pl.program_id(1) == 0)
    if single_step:
        _ffn_compute(True, *args, **static)
    elif static["n_mix"]:
        pl.when(is_first)(lambda: _preload_weights(win_hbm, wout_hbm, *rest[-5:], static["w_idx"]))
        _ffn_compute(False, *args, **static)
    else:
        pl.when(is_first)(lambda: _ffn_compute(True, *args, **static))
        pl.when(jnp.logical_not(is_first))(lambda: _ffn_compute(False, *args, **static))


def _ffn(h, mod, row0, k0, g_all, g_row, w_in, w_out, w_idx, final_g, final, mix=None):
    nb, n, _ = h.shape
    tm = min(TM, n)
    in_specs = [
        pl.BlockSpec((1, tm, D_MODEL), lambda b, m: (b, m, 0)),
        _mod_spec(row0, k0), _mod_spec(row0, k0 + 1), _mod_spec(row0, k0 + 2),
        _const_row_spec(g_row),
        pl.BlockSpec(memory_space=pl.ANY),
        pl.BlockSpec(memory_space=pl.ANY),
        _const_row_spec(0),
    ]
    args = [h, mod, mod, mod, g_all, w_in, w_out, final_g]
    n_mix, mix_bias = 0, False
    if mix is not None:
        acts, mw, mb, mk = mix
        n_mix, mix_bias = len(acts), mb is not None
        in_specs += [pl.BlockSpec((1, tm, a.shape[-1]), lambda b, m: (b, m, 0)) for a in acts]
        in_specs += [_mod_spec(row0, mk), _resident(mw.shape, lambda b, m: (0, 0))]
        args += [*acts, mod, mw]
        if mix_bias:
            in_specs.append(pl.BlockSpec((1, D_MODEL), lambda b, m: (0, 0)))
            args.append(mb)
    return pl.pallas_call(
        functools.partial(_ffn_body, final=final, n_mix=n_mix, mix_bias=mix_bias, w_idx=w_idx,
                          single_step=(nb * (n // tm) == 1)),
        grid=(nb, n // tm),
        in_specs=in_specs,
        out_specs=pl.BlockSpec((1, tm, D_MODEL), lambda b, m: (b, m, 0)),
        out_shape=jax.ShapeDtypeStruct(h.shape, F32),
        scratch_shapes=[
            pltpu.VMEM((tm, D_FF), BF16),
            pltpu.VMEM((D_MODEL, 2 * D_FF), BF16),
            pltpu.VMEM((D_FF, D_MODEL), BF16),
            pltpu.VMEM((W_SLOTS, 2, D_MODEL, FF_CHUNK), F32),
            pltpu.VMEM((W_SLOTS, FF_CHUNK, D_MODEL), F32),
            pltpu.SemaphoreType.DMA((W_SLOTS, 3)),
        ],
        compiler_params=pltpu.CompilerParams(dimension_semantics=("arbitrary", "arbitrary"),
                                             vmem_limit_bytes=FFN_VMEM_LIMIT),
        name="ffn_final" if final else "ffn",
    )(*args)


def _proj_body(x_ref, sh_ref, sc_ref, g_ref, w_ref, u_ref, qkv_ref, wbf_ref):
    @pl.when((pl.program_id(0) == 0) & (pl.program_id(1) == 0))
    def _():
        wbf_ref[...] = w_ref[...].astype(BF16)

    xb = _modnorm(x_ref[0], g_ref[0], sh_ref[0], sc_ref[0])
    u_ref[0] = _dot(xb, wbf_ref[:, :2 * D_CONV])
    qkv_ref[0] = _dot(xb, wbf_ref[:, 2 * D_CONV:]).astype(BF16)


def _proj(h, mod, row0, k0, g_all, g_row, w):
    nb, n, _ = h.shape
    return pl.pallas_call(
        _proj_body,
        grid=(nb, n // TM),
        in_specs=[
            pl.BlockSpec((1, TM, D_MODEL), lambda b, m: (b, m, 0)),
            _mod_spec(row0, k0), _mod_spec(row0, k0 + 1),
            _const_row_spec(g_row),
            _resident(w.shape, lambda b, m: (0, 0)),
        ],
        out_specs=[
            pl.BlockSpec((1, TM, 2 * D_CONV), lambda b, m: (b, m, 0)),
            pl.BlockSpec((1, TM, 3 * D_NA), lambda b, m: (b, m, 0)),
        ],
        out_shape=[
            jax.ShapeDtypeStruct((nb, n, 2 * D_CONV), F32),
            jax.ShapeDtypeStruct((nb, n, 3 * D_NA), BF16),
        ],
        scratch_shapes=[pltpu.VMEM(w.shape, BF16)],
        compiler_params=_params("arbitrary", "arbitrary"),
        name="ab_proj",
    )(h, mod, mod, g_all, w)


def _qkv_body(x_ref, sh_ref, sc_ref, g_ref, w_ref, qkv_ref):
    xb = _modnorm(x_ref[0], g_ref[0], sh_ref[0], sc_ref[0])
    qkv_ref[0] = _dot(xb, w_ref[:, 2 * D_CONV:].astype(BF16)).astype(BF16)


def _qkv_proj(h, mod, row0, k0, g_all, g_row, w):
    nb, n, _ = h.shape
    tm = min(TM, n)
    return pl.pallas_call(
        _qkv_body,
        grid=(nb, n // tm),
        in_specs=[
            pl.BlockSpec((1, tm, D_MODEL), lambda b, m: (b, m, 0)),
            _mod_spec(row0, k0), _mod_spec(row0, k0 + 1),
            _const_row_spec(g_row),
            _resident(w.shape, lambda b, m: (0, 0)),
        ],
        out_specs=pl.BlockSpec((1, tm, 3 * D_NA), lambda b, m: (b, m, 0)),
        out_shape=jax.ShapeDtypeStruct((nb, n, 3 * D_NA), BF16),
        compiler_params=_params("arbitrary", "arbitrary"),
        name="ctx_qkv",
    )(h, mod, mod, g_all, w)


def _glu(v):
    return v[:, :D_CONV] * (1.0 / (1.0 + jnp.exp(-v[:, D_CONV:])))


CONV_RC = 64
CONV_N = CONV_TL + 2 * CONV_HALO


def _conv_body(cur_ref, prev_ref, next_ref, w_ref, b_ref, lg_ref, lb_ref, o_ref, y_ref):
    t = pl.program_id(1)
    nt = pl.num_programs(1)
    y_ref[0, CONV_HALO:CONV_HALO + CONV_TL, :] = _glu(cur_ref[0])
    y_ref[0, 0:CONV_HALO, :] = jnp.where(t > 0, _glu(prev_ref[0]), 0.0)
    y_ref[0, CONV_HALO + CONV_TL:, :] = jnp.where(t < nt - 1, _glu(next_ref[0]), 0.0)
    for s in range(1, SUBLANES):
        y_ref[s, 0:CONV_N - SUBLANES, :] = y_ref[0, s:s + CONV_N - SUBLANES, :]
    off = CONV_HALO - CONV_WIDTH // 2
    for r in range(0, CONV_TL, CONV_RC):
        acc = jnp.zeros((CONV_RC // SUBLANES, SUBLANES, D_CONV), F32)
        for k in range(CONV_WIDTH):
            m8, s = divmod(off + k, SUBLANES)
            lo = r + SUBLANES * m8
            yk = y_ref[s, lo:lo + CONV_RC, :].reshape(CONV_RC // SUBLANES, SUBLANES, D_CONV)
            acc = acc + w_ref[k][None] * yk
        acc = acc.reshape(CONV_RC, D_CONV) + b_ref[...]
        mu = jnp.mean(acc, axis=-1, keepdims=True)
        cen = acc - mu
        var = jnp.mean(cen * cen, axis=-1, keepdims=True)
        z = cen * lax.rsqrt(var + EPS) * lg_ref[...] + lb_ref[...]
        o_ref[0, r:r + CONV_RC, :] = _silu(z).astype(BF16)


def _conv(u, w, b, ln_g, ln_b):
    nb, n, _ = u.shape
    nt = n // CONV_TL
    hb = CONV_TL // CONV_HALO
    last = n // CONV_HALO - 1
    row = lambda v: v.reshape(1, D_CONV)
    w_rep = jnp.broadcast_to(w[:, None, :], (CONV_WIDTH, SUBLANES, D_CONV))
    return pl.pallas_call(
        _conv_body,
        grid=(nb, nt),
        in_specs=[
            pl.BlockSpec((1, CONV_TL, 2 * D_CONV), lambda b_, t: (b_, t, 0)),
            pl.BlockSpec((1, CONV_HALO, 2 * D_CONV), lambda b_, t: (b_, jnp.maximum(t * hb - 1, 0), 0)),
            pl.BlockSpec((1, CONV_HALO, 2 * D_CONV), lambda b_, t: (b_, jnp.minimum((t + 1) * hb, last), 0)),
            pl.BlockSpec((CONV_WIDTH, SUBLANES, D_CONV), lambda b_, t: (0, 0, 0)),
            pl.BlockSpec((1, D_CONV), lambda b_, t: (0, 0)),
            pl.BlockSpec((1, D_CONV), lambda b_, t: (0, 0)),
            pl.BlockSpec((1, D_CONV), lambda b_, t: (0, 0)),
        ],
        out_specs=pl.BlockSpec((1, CONV_TL, D_CONV), lambda b_, t: (b_, t, 0)),
        out_shape=jax.ShapeDtypeStruct((nb, n, D_CONV), BF16),
        scratch_shapes=[pltpu.VMEM((SUBLANES, CONV_N, D_CONV), F32)],
        compiler_params=_params("arbitrary", "arbitrary"),
        name="conv_module",
    )(u, u, u, w_rep, row(b), row(ln_g), row(ln_b))


NA_HG = 4
NA_DR_PAD = 8
NA_T2 = 2 * NA_KH - 1 + 2 * NA_DR_PAD - 1


def _na_bias_table(rpb):
    qc = np.arange(GRID_W)
    cs = np.clip(qc - NA_KW // 2, 0, GRID_W - NA_KW)
    kc = np.arange(GRID_W)
    col_valid = (kc[None, :] >= cs[:, None]) & (kc[None, :] < cs[:, None] + NA_KW)
    dc = kc[None, :] - qc[:, None] + NA_KW - 1
    oh_c = np.zeros((2 * NA_KW - 1, GRID_W, GRID_W), np.float32)
    qi, ki = np.nonzero(col_valid)
    oh_c[dc[qi, ki], qi, ki] = 1.0
    t_col = jnp.einsum('hrd,dqk->hrqk', rpb, jnp.asarray(oh_c), precision=lax.Precision.HIGHEST)
    t_col = jnp.where(jnp.asarray(col_valid)[None, None], t_col, NEG)
    t_pad = jnp.pad(t_col, ((0, 0), (NA_DR_PAD, NA_DR_PAD), (0, 0), (0, 0)))
    return jnp.concatenate([t_pad[:, :NA_T2], t_pad[:, 1:NA_T2 + 1]], axis=-1)


def _na_window(blk):
    w0 = min(max(NA_ROWS * blk - NA_KH // 2, 0), GRID_W - NA_KROWS)
    a_lo = [min(max(NA_ROWS * blk + i - NA_KH // 2, 0), GRID_W - NA_KH) - w0 for i in range(NA_ROWS)]
    return w0, a_lo


def _na_probs(t2_ref, blk, j, s, sc):
    w0, a_lo = _na_window(blk)
    base = w0 - NA_ROWS * blk + NA_KH - 1 + NA_DR_PAD
    plane = lax.broadcasted_iota(jnp.int32, (1, LANES), 1)
    zero = jnp.zeros((GRID_W, LANES), BF16)
    p_rows, pc_rows = [], []
    for i in range(NA_ROWS):
        rows = slice(i * GRID_W, (i + 1) * GRID_W)
        pieces = {}
        for p in range(NA_KROWS // 2):
            first = a_lo[i] <= 2 * p < a_lo[i] + NA_KH
            second = a_lo[i] <= 2 * p + 1 < a_lo[i] + NA_KH
            if not (first or second):
                continue
            piece = s[rows, p * LANES:(p + 1) * LANES] + t2_ref[j, base + 2 * p - i]
            if not (first and second):
                piece = jnp.where((plane < GRID_W) if first else (plane >= GRID_W), piece, NEG)
            pieces[p] = piece
        sc_i = sc[rows]
        m = jnp.max(sc_i, axis=-1, keepdims=True)
        for piece in pieces.values():
            m = jnp.maximum(m, jnp.max(piece, axis=-1, keepdims=True))
        p_rows.append(jnp.concatenate(
            [jnp.exp(pieces[p] - m).astype(BF16) if p in pieces else zero for p in range(NA_KROWS // 2)], axis=-1))
        pc_rows.append(jnp.exp(sc_i - m).astype(BF16))
    return jnp.concatenate(p_rows, axis=0), jnp.concatenate(pc_rows, axis=0)


def _na_body(q_ref, k_ref, v_ref, kc_ref, vc_ref, t2_ref, o_ref):
    blk = pl.program_id(2)
    w0 = jnp.clip(NA_ROWS * blk - NA_KH // 2, 0, GRID_W - NA_KROWS)
    start = pl.multiple_of(w0 * GRID_W, 256)
    q2 = q_ref[0] * jnp.asarray(HEAD_DIM ** -0.5, BF16)
    kw = k_ref[0, pl.ds(start, NA_KB), :]
    vw = v_ref[0, pl.ds(start, NA_KB), :]
    kc = kc_ref[0]
    vc = vc_ref[0]
    lane = lax.broadcasted_iota(jnp.int32, (1, NA_HG * HEAD_DIM), 1)
    nt = (((1,), (1,)), ((), ()))
    one = jnp.ones((), BF16)

    def run(probs_fn):
        out = None
        for j in range(NA_HG):
            in_head = (lane >= j * HEAD_DIM) & (lane < (j + 1) * HEAD_DIM)
            qm = jnp.where(in_head, q2, jnp.zeros_like(q2))
            s = lax.dot_general(qm, kw, nt, preferred_element_type=F32)
            sc = lax.dot_general(qm, kc, nt, preferred_element_type=F32)
            p, pc = probs_fn(j, s, sc)
            o = _dot(p, jnp.where(in_head, vw, one)) + _dot(pc, jnp.where(in_head, vc, one))
            o = o * (1.0 / pltpu.roll(o, HEAD_DIM, axis=1))
            out = o if out is None else jnp.where(in_head, o, out)
        o_ref[0] = out.astype(BF16)

    last = GRID_W // NA_ROWS - 1
    pl.when(blk == 0)(lambda: run(functools.partial(_na_probs, t2_ref, 0)))
    pl.when((blk > 0) & (blk < last))(lambda: run(functools.partial(_na_probs, t2_ref, 1)))
    pl.when(blk == last)(lambda: run(functools.partial(_na_probs, t2_ref, last)))


def _natten(qkv, qkv_c, t2):
    nb, n, _ = qkv.shape
    nctx = qkv_c.shape[1]
    ng = NA_HEADS // NA_HG
    lanes = NA_HG * HEAD_DIM
    return pl.pallas_call(
        _na_body,
        grid=(ng, nb, n // NA_QB),
        in_specs=[
            pl.BlockSpec((1, NA_QB, lanes), lambda h, b, i: (b, i, h)),
            pl.BlockSpec((1, n, lanes), lambda h, b, i: (b, 0, ng + h)),
            pl.BlockSpec((1, n, lanes), lambda h, b, i: (b, 0, 2 * ng + h)),
            pl.BlockSpec((1, nctx, lanes), lambda h, b, i: (b, 0, ng + h)),
            pl.BlockSpec((1, nctx, lanes), lambda h, b, i: (b, 0, 2 * ng + h)),
            pl.BlockSpec((NA_HG, NA_T2, GRID_W, 2 * GRID_W), lambda h, b, i: (h, 0, 0, 0)),
        ],
        out_specs=pl.BlockSpec((1, NA_QB, lanes), lambda h, b, i: (b, i, h)),
        out_shape=jax.ShapeDtypeStruct((nb, n, D_NA), BF16),
        compiler_params=_params("arbitrary", "arbitrary", "arbitrary"),
        name="natten",
    )(qkv, qkv, qkv, qkv_c, qkv_c, t2)


SEQ = GRID_W * GRID_W
FN_R = SUBLANES
FN_M = SEQ // FN_R
FN_TA = 2048
FN_LANES = 512
FN_CH = 16


def _dft_tables(n):
    idx = np.arange(n, dtype=np.int64)
    ang = 2.0 * np.pi * ((idx[:, None] * idx[None, :]) % n).astype(np.float64) / n
    scale = 1.0 / np.sqrt(n)
    return (np.cos(ang) * scale).astype(np.float32), (np.sin(ang) * scale).astype(np.float32)


def _seq_tables():
    k2 = np.arange(FN_M, dtype=np.int64)
    n2 = np.arange(FN_M, dtype=np.int64)
    out = np.zeros((FN_R, 3, FN_M, FN_M), np.float32)
    for n1 in range(FN_R):
        num = (k2[:, None] * n2[None, :] * FN_R + n1 * k2[:, None]) % SEQ
        ang = 2.0 * np.pi * num.astype(np.float64) / SEQ
        c = np.cos(ang) / np.sqrt(SEQ)
        s = np.sin(ang) / np.sqrt(SEQ)
        out[n1, 0] = c
        out[n1, 1] = c - s
        out[n1, 2] = -(c + s)
    return out


def _fnet_a_body(*refs):
    nx = D_MODEL // LANES
    x_refs = refs[:nx]
    sh_ref, sc_ref, g_ref, cs_ref, a_ref, b_ref = refs[nx:]
    gw = D_MODEL // FNET_GROUPS
    for s in range(FN_R):
        xs = jnp.concatenate([x[0, pl.ds(s, FN_TA // FN_R, stride=FN_R), :] for x in x_refs], axis=-1)
        xb = _modnorm(xs, g_ref[0], sh_ref[0], sc_ref[0])
        for grp in range(FNET_GROUPS):
            ab = _dot(xb[:, grp * gw:(grp + 1) * gw], cs_ref[...])
            a_ref[0, s, :, grp * gw:(grp + 1) * gw] = ab[:, :gw].astype(BF16)
            b_ref[0, s, :, grp * gw:(grp + 1) * gw] = ab[:, gw:].astype(BF16)


def _fnet_a(h, mod, row0, k0, g_all, g_row, cs):
    nb, n, _ = h.shape
    gw = D_MODEL // FNET_GROUPS
    nx = D_MODEL // LANES
    out_spec = pl.BlockSpec((1, FN_R, FN_TA // FN_R, D_MODEL), lambda b, m: (b, 0, m, 0))
    return pl.pallas_call(
        _fnet_a_body,
        grid=(nb, n // FN_TA),
        in_specs=[pl.BlockSpec((1, FN_TA, LANES), functools.partial(lambda b, m, c: (b, m, c), c=c))
                  for c in range(nx)]
        + [_mod_spec(row0, k0), _mod_spec(row0, k0 + 1), _const_row_spec(g_row),
           _resident((gw, 2 * gw), lambda b, m: (0, 0))],
        out_specs=[out_spec, out_spec],
        out_shape=[jax.ShapeDtypeStruct((nb, FN_R, n // FN_R, D_MODEL), BF16)] * 2,
        compiler_params=_params("arbitrary", "arbitrary"),
        name="fnet_channel_dft",
    )(*([h] * nx), mod, mod, g_all, cs)


def _fnet_b_body(zr_ref, zi_ref, m_ref, o_ref, v_ref):
    for n1 in range(FN_R):
        zr, zi = zr_ref[0, n1], zi_ref[0, n1]
        k1 = _dot(m_ref[n1, 0], (zr.astype(F32) + zi.astype(F32)).astype(BF16))
        k3 = _dot(m_ref[n1, 1], zi)
        k2 = _dot(m_ref[n1, 2], zr)
        v_ref[n1, :FN_M, :] = k1 - k3
        v_ref[n1, FN_M:, :] = k1 + k2

    rt = np.float32(np.sqrt(0.5))

    def chunk(i, carry):
        r0 = pl.multiple_of(i * FN_CH, FN_CH)
        for lt in range(FN_LANES // LANES):
            ls = slice(lt * LANES, (lt + 1) * LANES)
            re = [v_ref[n, pl.ds(r0, FN_CH), ls] for n in range(FN_R)]
            im = [v_ref[n, pl.ds(FN_M + r0, FN_CH), ls] for n in range(FN_R)]
            e0 = (re[0] + re[4]) + (re[2] + re[6])
            e2 = (re[0] + re[4]) - (re[2] + re[6])
            e1 = (re[0] - re[4]) + (im[2] - im[6])
            e3 = (re[0] - re[4]) - (im[2] - im[6])
            t0r, t0i = re[1] + re[5], im[1] + im[5]
            t1r, t1i = re[1] - re[5], im[1] - im[5]
            t2r, t2i = re[3] + re[7], im[3] + im[7]
            t3r, t3i = re[3] - re[7], im[3] - im[7]
            p0 = t0r + t2r
            p2 = t0i - t2i
            al = t1r - t3r
            be = t1i + t3i
            p1 = (al + be) * rt
            p3 = (be - al) * rt
            ys = (e0 + p0, e1 + p1, e2 + p2, e3 + p3, e0 - p0, e1 - p1, e2 - p2, e3 - p3)
            for k1 in range(FN_R):
                o_ref[0, pl.ds(k1 * FN_M + r0, FN_CH), ls] = ys[k1].astype(BF16)
        return carry

    lax.fori_loop(0, FN_M // FN_CH, chunk, 0)


def _fnet_b(zr, zi, mtab):
    assert FN_R == 8
    nb = zr.shape[0]
    z_spec = pl.BlockSpec((1, FN_R, FN_M, FN_LANES), lambda b, l: (b, 0, 0, l))
    return pl.pallas_call(
        _fnet_b_body,
        grid=(nb, D_MODEL // FN_LANES),
        in_specs=[z_spec, z_spec, _resident((FN_R, 3, FN_M, FN_M), lambda b, l: (0, 0, 0, 0))],
        out_specs=pl.BlockSpec((1, SEQ, FN_LANES), lambda b, l: (b, 0, l)),
        out_shape=jax.ShapeDtypeStruct((nb, SEQ, D_MODEL), BF16),
        scratch_shapes=[pltpu.VMEM((FN_R, 2 * FN_M, FN_LANES), F32)],
        compiler_params=_params("arbitrary", "arbitrary"),
        name="fnet_seq_dft",
    )(zr, zi, mtab)


def kernel(x, c, ctx, c_ctx, ada_w, ada_b, norm_g, ffn_w_in, ffn_w_out, ab_w_in, conv_w, conv_b,
           conv_ln_g, conv_ln_b, na_rpb, ab_w_out, fnet_w, fnet_b, final_g):
    nb, n, d = x.shape
    depth = ada_w.shape[0]
    nctx = ctx.shape[1]
    assert (d, depth, nb) == (D_MODEL, 2, 4) and n == GRID_W * GRID_W

    cc = jnp.concatenate([c, c_ctx[None], jnp.zeros((8 - nb - 1, d), F32)], axis=0)
    mod = _ada(cc, ada_w, ada_b).reshape(depth * 8, 1, N_MOD * d)
    ctx_row = nb

    g_all = norm_g.reshape(depth * 3, 1, d)
    fg = final_g.reshape(1, 1, d)
    w_in = ffn_w_in.reshape(depth * 2, d, 2 * D_FF)
    w_out = ffn_w_out.reshape(depth * 2, D_FF, d)

    h = _ffn(x, mod, 0, 0, g_all, 0, w_in, w_out, 0, fg, False)
    hc = _ffn(ctx.reshape(1, nb * nctx, d), mod, ctx_row, 0, g_all, 0, w_in, w_out, 0, fg, False)
    assert ab_w_in.shape[0] == 1
    w_ab = ab_w_in.reshape(ab_w_in.shape[1:])
    u, qkv = _proj(h, mod, 0, 3, g_all, 1, w_ab)
    qkv_c = _qkv_proj(hc, mod, ctx_row, 3, g_all, 1, w_ab)
    conv_x = _conv(u, conv_w[0], conv_b[0], conv_ln_g[0], conv_ln_b[0])
    att_x = _natten(qkv, qkv_c.reshape(nb, nctx, 3 * D_NA), _na_bias_table(na_rpb[0]))
    h = _ffn(h, mod, 0, 6, g_all, 2, w_in, w_out, 1, fg, False,
             mix=((conv_x, att_x), ab_w_out[0].astype(BF16), None, 5))

    h = _ffn(h, mod, 8, 0, g_all, 3, w_in, w_out, 2, fg, False)
    gw = d // FNET_GROUPS
    cc_tab, sc_tab = _dft_tables(gw)
    cs = jnp.asarray(np.concatenate([cc_tab, -sc_tab], axis=1)).astype(BF16)
    zr, zi = _fnet_a(h, mod, 8, 3, g_all, 4, cs)
    f = _fnet_b(zr, zi, jnp.asarray(_seq_tables()).astype(BF16))
    return _ffn(h, mod, 8, 6, g_all, 5, w_in, w_out, 3, fg, True,
                mix=((f,), fnet_w[0].astype(BF16), fnet_b[0].reshape(1, d), 5))
```

```python
import functools

import numpy as np
import jax
import jax.numpy as jnp
from jax import lax
from jax.experimental import pallas as pl
from jax.experimental.pallas import tpu as pltpu

D_MODEL = 1024
GRID_W = 64
D_CONV = 512
D_NA = 512
NA_HEADS = 8
HEAD_DIM = 64
CONV_WIDTH = 31
NA_KH = 8
NA_KW = 16
FNET_GROUPS = 4
D_FF = 2816
N_MOD = 9
EPS = 1e-6

BF16 = jnp.bfloat16
F32 = jnp.float32

VMEM_LIMIT = 56 * 1024 * 1024
FFN_VMEM_LIMIT = 60 * 1024 * 1024
SUBLANES = 8
LANES = 128
MOD_ROWS = SUBLANES
TM = 1024
FF_CHUNK = 256
CONV_TL = 512
CONV_HALO = 16
NA_ROWS = 8
NA_QB = NA_ROWS * GRID_W
NA_KROWS = 16
NA_KB = NA_KROWS * GRID_W
NEG = -1e30


def _params(*sem):
    return pltpu.CompilerParams(dimension_semantics=sem, vmem_limit_bytes=VMEM_LIMIT)


def _resident(shape, index_map):
    return pl.BlockSpec(shape, index_map, pipeline_mode=pl.Buffered(1))


def _silu(x):
    return x * (1.0 / (1.0 + jnp.exp(-x)))


def _dot(a, b):
    return jnp.dot(a, b, preferred_element_type=F32)


def _rms(x, g):
    return x * lax.rsqrt(jnp.mean(x * x, axis=-1, keepdims=True) + EPS) * g


def _modnorm(x, g, shift, scale):
    return (_rms(x, g) * (1.0 + scale) + shift).astype(BF16)


ADA_TN = 768
ADA_SPLIT = 3


def _ada_body(cc_ref, *refs):
    w_refs, b_ref, o_ref = refs[:ADA_SPLIT], refs[ADA_SPLIT], refs[ADA_SPLIT + 1]
    s = _silu(cc_ref[...]).astype(BF16)
    for q, w_ref in enumerate(w_refs):
        cols = slice(q * ADA_TN, (q + 1) * ADA_TN)
        o_ref[0, :, cols] = _dot(s, w_ref[0].astype(BF16)) + b_ref[0, :, cols]


def _ada(cc, ada_w, ada_b):
    depth, _, n = ada_w.shape
    step = ADA_SPLIT * ADA_TN

    def w_spec(q):
        return pl.BlockSpec((1, D_MODEL, ADA_TN), lambda i, j: (i, 0, ADA_SPLIT * j + q))

    return pl.pallas_call(
        _ada_body,
        grid=(depth, n // step),
        in_specs=[pl.BlockSpec((MOD_ROWS, D_MODEL), lambda i, j: (0, 0))]
        + [w_spec(q) for q in range(ADA_SPLIT)]
        + [pl.BlockSpec((1, 1, step), lambda i, j: (i, 0, j))],
        out_specs=pl.BlockSpec((1, MOD_ROWS, step), lambda i, j: (i, 0, j)),
        out_shape=jax.ShapeDtypeStruct((depth, MOD_ROWS, n), F32),
        compiler_params=_params("arbitrary", "arbitrary"),
        name="ada_mod",
    )(cc, *([ada_w] * ADA_SPLIT), ada_b.reshape(depth, 1, n))


def _mod_spec(row0, k):
    return pl.BlockSpec((1, 1, D_MODEL), lambda b, m: (row0 + b, 0, k))


def _const_row_spec(row):
    return pl.BlockSpec((1, 1, D_MODEL), lambda b, m: (row, 0, 0))


W_SLOTS = 2


def _weight_copies(win_hbm, wout_hbm, st_in, st_out, sem, w_idx, j, slot):
    lo = j * FF_CHUNK
    return (
        pltpu.make_async_copy(win_hbm.at[w_idx, :, pl.ds(lo, FF_CHUNK)], st_in.at[slot, 0], sem.at[slot, 0]),
        pltpu.make_async_copy(win_hbm.at[w_idx, :, pl.ds(D_FF + lo, FF_CHUNK)], st_in.at[slot, 1], sem.at[slot, 1]),
        pltpu.make_async_copy(wout_hbm.at[w_idx, pl.ds(lo, FF_CHUNK), :], st_out.at[slot], sem.at[slot, 2]),
    )


def _start_stream(copies):
    for j in range(W_SLOTS):
        for cp in copies(j, j):
            cp.start()


def _land_chunk(copies, j, win_ref, wout_ref, st_in, st_out):
    slot, lo = j % W_SLOTS, j * FF_CHUNK
    for cp in copies(j, slot):
        cp.wait()
    win_ref[:, lo:lo + FF_CHUNK] = st_in[slot, 0].astype(BF16)
    win_ref[:, D_FF + lo:D_FF + lo + FF_CHUNK] = st_in[slot, 1].astype(BF16)
    wout_ref[lo:lo + FF_CHUNK, :] = st_out[slot].astype(BF16)
    if j + W_SLOTS < D_FF // FF_CHUNK:
        for cp in copies(j + W_SLOTS, slot):
            cp.start()


def _ffn_compute(stream, x_ref, sh_ref, sc_ref, gt_ref, g_ref, win_hbm, wout_hbm, fg_ref, mix_refs,
                 o_ref, mid_ref, win_ref, wout_ref, st_in, st_out, sem, *, final, n_mix, mix_bias, w_idx):
    copies = functools.partial(_weight_copies, win_hbm, wout_hbm, st_in, st_out, sem, w_idx)
    if stream:
        _start_stream(copies)
    x = x_ref[0]
    if n_mix:
        acts, mg_ref, mw_ref = mix_refs[:n_mix], mix_refs[n_mix], mix_refs[n_mix + 1]
        y0, r = None, 0
        for a_ref in acts:
            k = a_ref.shape[-1]
            t = _dot(a_ref[0], mw_ref[r:r + k, :])
            y0 = t if y0 is None else y0 + t
            r += k
        if mix_bias:
            y0 = y0 + mix_refs[n_mix + 2][...]
        x = x + mg_ref[0] * y0
    xb = _modnorm(x, g_ref[0], sh_ref[0], sc_ref[0])
    for j in range(D_FF // FF_CHUNK):
        lo = j * FF_CHUNK
        if stream:
            _land_chunk(copies, j, win_ref, wout_ref, st_in, st_out)
        gate = _dot(xb, win_ref[:, lo:lo + FF_CHUNK])
        up = _dot(xb, win_ref[:, D_FF + lo:D_FF + lo + FF_CHUNK])
        mid_ref[:, lo:lo + FF_CHUNK] = (_silu(gate) * up).astype(BF16)
    y = _dot(mid_ref[...], wout_ref[...])
    h = x + (0.5 * gt_ref[0]) * y
    if final:
        h = _rms(h, fg_ref[0])
    o_ref[0] = h


def _preload_weights(win_hbm, wout_hbm, win_ref, wout_ref, st_in, st_out, sem, w_idx):
    copies = functools.partial(_weight_copies, win_hbm, wout_hbm, st_in, st_out, sem, w_idx)
    _start_stream(copies)
    for j in range(D_FF // FF_CHUNK):
        _land_chunk(copies, j, win_ref, wout_ref, st_in, st_out)


def _ffn_body(x_ref, sh_ref, sc_ref, gt_ref, g_ref, win_hbm, wout_hbm, fg_ref, *rest, single_step, **static):
    args = (x_ref, sh_ref, sc_ref, gt_ref, g_ref, win_hbm, wout_hbm, fg_ref, rest[:-7], *rest[-7:])
    is_first = (pl.program_id(0) == 0) & (pl.program_id(1) == 0)
    if single_step:
        _ffn_compute(True, *args, **static)
    elif static["n_mix"]:
        pl.when(is_first)(lambda: _preload_weights(win_hbm, wout_hbm, *rest[-5:], static["w_idx"]))
        _ffn_compute(False, *args, **static)
    else:
        pl.when(is_first)(lambda: _ffn_compute(True, *args, **static))
        pl.when(jnp.logical_not(is_first))(lambda: _ffn_compute(False, *args, **static))


def _ffn(h, mod, row0, k0, g_all, g_row, w_in, w_out, w_idx, final_g, final, mix=None):
    nb, n, _ = h.shape
    tm = min(TM, n)
    in_specs = [
        pl.BlockSpec((1, tm, D_MODEL), lambda b, m: (b, m, 0)),
        _mod_spec(row0, k0), _mod_spec(row0, k0 + 1), _mod_spec(row0, k0 + 2),
        _const_row_spec(g_row),
        pl.BlockSpec(memory_space=pl.ANY),
        pl.BlockSpec(memory_space=pl.ANY),
        _const_row_spec(0),
    ]
    args = [h, mod, mod, mod, g_all, w_in, w_out, final_g]
    n_mix, mix_bias = 0, False
    if mix is not None:
        acts, mw, mb, mk = mix
        n_mix, mix_bias = len(acts), mb is not None
        in_specs += [pl.BlockSpec((1, tm, a.shape[-1]), lambda b, m: (b, m, 0)) for a in acts]
        in_specs += [_mod_spec(row0, mk), _resident(mw.shape, lambda b, m: (0, 0))]
        args += [*acts, mod, mw]
        if mix_bias:
            in_specs.append(pl.BlockSpec((1, D_MODEL), lambda b, m: (0, 0)))
            args.append(mb)
    return pl.pallas_call(
        functools.partial(_ffn_body, final=final, n_mix=n_mix, mix_bias=mix_bias, w_idx=w_idx,
                          single_step=(nb * (n // tm) == 1)),
        grid=(nb, n // tm),
        in_specs=in_specs,
        out_specs=pl.BlockSpec((1, tm, D_MODEL), lambda b, m: (b, m, 0)),
        out_shape=jax.ShapeDtypeStruct(h.shape, F32),
        scratch_shapes=[
            pltpu.VMEM((tm, D_FF), BF16),
            pltpu.VMEM((D_MODEL, 2 * D_FF), BF16),
            pltpu.VMEM((D_FF, D_MODEL), BF16),
            pltpu.VMEM((W_SLOTS, 2, D_MODEL, FF_CHUNK), F32),
            pltpu.VMEM((W_SLOTS, FF_CHUNK, D_MODEL), F32),
            pltpu.SemaphoreType.DMA((W_SLOTS, 3)),
        ],
        compiler_params=pltpu.CompilerParams(dimension_semantics=("arbitrary", "arbitrary"),
                                             vmem_limit_bytes=FFN_VMEM_LIMIT),
        name="ffn_final" if final else "ffn",
    )(*args)


def _proj_body(x_ref, sh_ref, sc_ref, g_ref, w_ref, u_ref, qkv_ref, wbf_ref):
    @pl.when((pl.program_id(0) == 0) & (pl.program_id(1) == 0))
    def _():
        wbf_ref[...] = w_ref[...].astype(BF16)

    xb = _modnorm(x_ref[0], g_ref[0], sh_ref[0], sc_ref[0])
    u_ref[0] = _dot(xb, wbf_ref[:, :2 * D_CONV])
    qkv_ref[0] = _dot(xb, wbf_ref[:, 2 * D_CONV:]).astype(BF16)


def _proj(h, mod, row0, k0, g_all, g_row, w):
    nb, n, _ = h.shape
    return pl.pallas_call(
        _proj_body,
        grid=(nb, n // TM),
        in_specs=[
            pl.BlockSpec((1, TM, D_MODEL), lambda b, m: (b, m, 0)),
            _mod_spec(row0, k0), _mod_spec(row0, k0 + 1),
            _const_row_spec(g_row),
            _resident(w.shape, lambda b, m: (0, 0)),
        ],
        out_specs=[
            pl.BlockSpec((1, TM, 2 * D_CONV), lambda b, m: (b, m, 0)),
            pl.BlockSpec((1, TM, 3 * D_NA), lambda b, m: (b, m, 0)),
        ],
        out_shape=[
            jax.ShapeDtypeStruct((nb, n, 2 * D_CONV), F32),
            jax.ShapeDtypeStruct((nb, n, 3 * D_NA), BF16),
        ],
        scratch_shapes=[pltpu.VMEM(w.shape, BF16)],
        compiler_params=_params("arbitrary", "arbitrary"),
        name="ab_proj",
    )(h, mod, mod, g_all, w)


def _qkv_body(x_ref, sh_ref, sc_ref, g_ref, w_ref, qkv_ref):
    xb = _modnorm(x_ref[0], g_ref[0], sh_ref[0], sc_ref[0])
    qkv_ref[0] = _dot(xb, w_ref[:, 2 * D_CONV:].astype(BF16)).astype(BF16)


def _qkv_proj(h, mod, row0, k0, g_all, g_row, w):
    nb, n, _ = h.shape
    tm = min(TM, n)
    return pl.pallas_call(
        _qkv_body,
        grid=(nb, n // tm),
        in_specs=[
            pl.BlockSpec((1, tm, D_MODEL), lambda b, m: (b, m, 0)),
            _mod_spec(row0, k0), _mod_spec(row0, k0 + 1),
            _const_row_spec(g_row),
            _resident(w.shape, lambda b, m: (0, 0)),
        ],
        out_specs=pl.BlockSpec((1, tm, 3 * D_NA), lambda b, m: (b, m, 0)),
        out_shape=jax.ShapeDtypeStruct((nb, n, 3 * D_NA), BF16),
        compiler_params=_params("arbitrary", "arbitrary"),
        name="ctx_qkv",
    )(h, mod, mod, g_all, w)


def _glu(v):
    return v[:, :D_CONV] * (1.0 / (1.0 + jnp.exp(-v[:, D_CONV:])))


CONV_RC = 64
CONV_N = CONV_TL + 2 * CONV_HALO


def _conv_body(cur_ref, prev_ref, next_ref, w_ref, b_ref, lg_ref, lb_ref, o_ref, y_ref):
    t = pl.program_id(1)
    nt = pl.num_programs(1)
    y_ref[0, CONV_HALO:CONV_HALO + CONV_TL, :] = _glu(cur_ref[0])
    y_ref[0, 0:CONV_HALO, :] = jnp.where(t > 0, _glu(prev_ref[0]), 0.0)
    y_ref[0, CONV_HALO + CONV_TL:, :] = jnp.where(t < nt - 1, _glu(next_ref[0]), 0.0)
    for s in range(1, SUBLANES):
        y_ref[s, 0:CONV_N - SUBLANES, :] = y_ref[0, s:s + CONV_N - SUBLANES, :]
    off = CONV_HALO - CONV_WIDTH // 2
    for r in range(0, CONV_TL, CONV_RC):
        acc = jnp.zeros((CONV_RC // SUBLANES, SUBLANES, D_CONV), F32)
        for k in range(CONV_WIDTH):
            m8, s = divmod(off + k, SUBLANES)
            lo = r + SUBLANES * m8
            yk = y_ref[s, lo:lo + CONV_RC, :].reshape(CONV_RC // SUBLANES, SUBLANES, D_CONV)
            acc = acc + w_ref[k][None] * yk
        acc = acc.reshape(CONV_RC, D_CONV) + b_ref[...]
        mu = jnp.mean(acc, axis=-1, keepdims=True)
        cen = acc - mu
        var = jnp.mean(cen * cen, axis=-1, keepdims=True)
        z = cen * lax.rsqrt(var + EPS) * lg_ref[...] + lb_ref[...]
        o_ref[0, r:r + CONV_RC, :] = _silu(z).astype(BF16)


def _conv(u, w, b, ln_g, ln_b):
    nb, n, _ = u.shape
    nt = n // CONV_TL
    hb = CONV_TL // CONV_HALO
    last = n // CONV_HALO - 1
    row = lambda v: v.reshape(1, D_CONV)
    w_rep = jnp.broadcast_to(w[:, None, :], (CONV_WIDTH, SUBLANES, D_CONV))
    return pl.pallas_call(
        _conv_body,
        grid=(nb, nt),
        in_specs=[
            pl.BlockSpec((1, CONV_TL, 2 * D_CONV), lambda b_, t: (b_, t, 0)),
            pl.BlockSpec((1, CONV_HALO, 2 * D_CONV), lambda b_, t: (b_, jnp.maximum(t * hb - 1, 0), 0)),
            pl.BlockSpec((1, CONV_HALO, 2 * D_CONV), lambda b_, t: (b_, jnp.minimum((t + 1) * hb, last), 0)),
            pl.BlockSpec((CONV_WIDTH, SUBLANES, D_CONV), lambda b_, t: (0, 0, 0)),
            pl.BlockSpec((1, D_CONV), lambda b_, t: (0, 0)),
            pl.BlockSpec((1, D_CONV), lambda b_, t: (0, 0)),
            pl.BlockSpec((1, D_CONV), lambda b_, t: (0, 0)),
        ],
        out_specs=pl.BlockSpec((1, CONV_TL, D_CONV), lambda b_, t: (b_, t, 0)),
        out_shape=jax.ShapeDtypeStruct((nb, n, D_CONV), BF16),
        scratch_shapes=[pltpu.VMEM((SUBLANES, CONV_N, D_CONV), F32)],
        compiler_params=_params("arbitrary", "arbitrary"),
        name="conv_module",
    )(u, u, u, w_rep, row(b), row(ln_g), row(ln_b))


NA_HG = 4
NA_DR_PAD = 8
NA_T2 = 2 * NA_KH - 1 + 2 * NA_DR_PAD - 1


def _na_bias_table(rpb):
    qc = np.arange(GRID_W)
    cs = np.clip(qc - NA_KW // 2, 0, GRID_W - NA_KW)
    kc = np.arange(GRID_W)
    col_valid = (kc[None, :] >= cs[:, None]) & (kc[None, :] < cs[:, None] + NA_KW)
    dc = kc[None, :] - qc[:, None] + NA_KW - 1
    oh_c = np.zeros((2 * NA_KW - 1, GRID_W, GRID_W), np.float32)
    qi, ki = np.nonzero(col_valid)
    oh_c[dc[qi, ki], qi, ki] = 1.0
    t_col = jnp.einsum('hrd,dqk->hrqk', rpb, jnp.asarray(oh_c), precision=lax.Precision.HIGHEST)
    t_col = jnp.where(jnp.asarray(col_valid)[None, None], t_col, NEG)
    t_pad = jnp.pad(t_col, ((0, 0), (NA_DR_PAD, NA_DR_PAD), (0, 0), (0, 0)))
    return jnp.concatenate([t_pad[:, :NA_T2], t_pad[:, 1:NA_T2 + 1]], axis=-1)


def _na_window(blk):
    w0 = min(max(NA_ROWS * blk - NA_KH // 2, 0), GRID_W - NA_KROWS)
    a_lo = [min(max(NA_ROWS * blk + i - NA_KH // 2, 0), GRID_W - NA_KH) - w0 for i in range(NA_ROWS)]
    return w0, a_lo


def _na_probs(t2_ref, blk, j, s, sc):
    w0, a_lo = _na_window(blk)
    base = w0 - NA_ROWS * blk + NA_KH - 1 + NA_DR_PAD
    plane = lax.broadcasted_iota(jnp.int32, (1, LANES), 1)
    zero = jnp.zeros((GRID_W, LANES), BF16)
    p_rows, pc_rows = [], []
    for i in range(NA_ROWS):
        rows = slice(i * GRID_W, (i + 1) * GRID_W)
        pieces = {}
        for p in range(NA_KROWS // 2):
            first = a_lo[i] <= 2 * p < a_lo[i] + NA_KH
            second = a_lo[i] <= 2 * p + 1 < a_lo[i] + NA_KH
            if not (first or second):
                continue
            piece = s[rows, p * LANES:(p + 1) * LANES] + t2_ref[j, base + 2 * p - i]
            if not (first and second):
                piece = jnp.where((plane < GRID_W) if first else (plane >= GRID_W), piece, NEG)
            pieces[p] = piece
        sc_i = sc[rows]
        m = jnp.max(sc_i, axis=-1, keepdims=True)
        for piece in pieces.values():
            m = jnp.maximum(m, jnp.max(piece, axis=-1, keepdims=True))
        p_rows.append(jnp.concatenate(
            [jnp.exp(pieces[p] - m).astype(BF16) if p in pieces else zero for p in range(NA_KROWS // 2)], axis=-1))
        pc_rows.append(jnp.exp(sc_i - m).astype(BF16))
    return jnp.concatenate(p_rows, axis=0), jnp.concatenate(pc_rows, axis=0)


def _na_body(q_ref, k_ref, v_ref, kc_ref, vc_ref, t2_ref, o_ref):
    blk = pl.program_id(2)
    w0 = jnp.clip(NA_ROWS * blk - NA_KH // 2, 0, GRID_W - NA_KROWS)
    start = pl.multiple_of(w0 * GRID_W, (NA_KH // 2) * GRID_W)
    q2 = q_ref[0] * jnp.asarray(HEAD_DIM ** -0.5, BF16)
    kw = k_ref[0, pl.ds(start, NA_KB), :]
    vw = v_ref[0, pl.ds(start, NA_KB), :]
    kc = kc_ref[0]
    vc = vc_ref[0]
    lane = lax.broadcasted_iota(jnp.int32, (1, NA_HG * HEAD_DIM), 1)
    nt = (((1,), (1,)), ((), ()))
    one = jnp.ones((), BF16)

    def run(probs_fn):
        out = None
        for j in range(NA_HG):
            in_head = (lane >= j * HEAD_DIM) & (lane < (j + 1) * HEAD_DIM)
            qm = jnp.where(in_head, q2, jnp.zeros_like(q2))
            s = lax.dot_general(qm, kw, nt, preferred_element_type=F32)
            sc = lax.dot_general(qm, kc, nt, preferred_element_type=F32)
            p, pc = probs_fn(j, s, sc)
            o = _dot(p, jnp.where(in_head, vw, one)) + _dot(pc, jnp.where(in_head, vc, one))
            o = o * (1.0 / pltpu.roll(o, HEAD_DIM, axis=1))
            out = o if out is None else jnp.where(in_head, o, out)
        o_ref[0] = out.astype(BF16)

    last = GRID_W // NA_ROWS - 1
    pl.when(blk == 0)(lambda: run(functools.partial(_na_probs, t2_ref, 0)))
    pl.when((blk > 0) & (blk < last))(lambda: run(functools.partial(_na_probs, t2_ref, 1)))
    pl.when(blk == last)(lambda: run(functools.partial(_na_probs, t2_ref, last)))


def _natten(qkv, qkv_c, t2):
    nb, n, _ = qkv.shape
    nctx = qkv_c.shape[1]
    ng = NA_HEADS // NA_HG
    lanes = NA_HG * HEAD_DIM
    return pl.pallas_call(
        _na_body,
        grid=(ng, nb, n // NA_QB),
        in_specs=[
            pl.BlockSpec((1, NA_QB, lanes), lambda h, b, i: (b, i, h)),
            pl.BlockSpec((1, n, lanes), lambda h, b, i: (b, 0, ng + h)),
            pl.BlockSpec((1, n, lanes), lambda h, b, i: (b, 0, 2 * ng + h)),
            pl.BlockSpec((1, nctx, lanes), lambda h, b, i: (b, 0, ng + h)),
            pl.BlockSpec((1, nctx, lanes), lambda h, b, i: (b, 0, 2 * ng + h)),
            pl.BlockSpec((NA_HG, NA_T2, GRID_W, 2 * GRID_W), lambda h, b, i: (h, 0, 0, 0)),
        ],
        out_specs=pl.BlockSpec((1, NA_QB, lanes), lambda h, b, i: (b, i, h)),
        out_shape=jax.ShapeDtypeStruct((nb, n, D_NA), BF16),
        compiler_params=_params("arbitrary", "arbitrary", "arbitrary"),
        name="natten",
    )(qkv, qkv, qkv, qkv_c, qkv_c, t2)


SEQ = GRID_W * GRID_W
FN_R = SUBLANES
FN_M = SEQ // FN_R
FN_TA = 2048
FN_LANES = 256
FN_CH = 16


def _dft_tables(n):
    idx = np.arange(n, dtype=np.int64)
    ang = 2.0 * np.pi * ((idx[:, None] * idx[None, :]) % n).astype(np.float64) / n
    scale = 1.0 / np.sqrt(n)
    return (np.cos(ang) * scale).astype(np.float32), (np.sin(ang) * scale).astype(np.float32)


def _seq_tables():
    k2 = np.arange(FN_M, dtype=np.int64)
    n2 = np.arange(FN_M, dtype=np.int64)
    out = np.zeros((FN_R, 3, FN_M, FN_M), np.float32)
    for n1 in range(FN_R):
        num = (k2[:, None] * n2[None, :] * FN_R + n1 * k2[:, None]) % SEQ
        ang = 2.0 * np.pi * num.astype(np.float64) / SEQ
        c = np.cos(ang) / np.sqrt(SEQ)
        s = np.sin(ang) / np.sqrt(SEQ)
        out[n1, 0] = c
        out[n1, 1] = c - s
        out[n1, 2] = -(c + s)
    return out


def _fnet_a_body(*refs):
    nx = D_MODEL // LANES
    x_refs = refs[:nx]
    sh_ref, sc_ref, g_ref, cs_ref, a_ref, b_ref = refs[nx:]
    gw = D_MODEL // FNET_GROUPS
    for s in range(FN_R):
        xs = jnp.concatenate([x[0, pl.ds(s, FN_TA // FN_R, stride=FN_R), :] for x in x_refs], axis=-1)
        xb = _modnorm(xs, g_ref[0], sh_ref[0], sc_ref[0])
        for grp in range(FNET_GROUPS):
            ab = _dot(xb[:, grp * gw:(grp + 1) * gw], cs_ref[...])
            a_ref[0, s, :, grp * gw:(grp + 1) * gw] = ab[:, :gw].astype(BF16)
            b_ref[0, s, :, grp * gw:(grp + 1) * gw] = ab[:, gw:].astype(BF16)


def _fnet_a(h, mod, row0, k0, g_all, g_row, cs):
    nb, n, _ = h.shape
    gw = D_MODEL // FNET_GROUPS
    nx = D_MODEL // LANES
    out_spec = pl.BlockSpec((1, FN_R, FN_TA // FN_R, D_MODEL), lambda b, m: (b, 0, m, 0))
    return pl.pallas_call(
        _fnet_a_body,
        grid=(nb, n // FN_TA),
        in_specs=[pl.BlockSpec((1, FN_TA, LANES), functools.partial(lambda b, m, c: (b, m, c), c=c))
                  for c in range(nx)]
        + [_mod_spec(row0, k0), _mod_spec(row0, k0 + 1), _const_row_spec(g_row),
           _resident((gw, 2 * gw), lambda b, m: (0, 0))],
        out_specs=[out_spec, out_spec],
        out_shape=[jax.ShapeDtypeStruct((nb, FN_R, n // FN_R, D_MODEL), BF16)] * 2,
        compiler_params=_params("arbitrary", "arbitrary"),
        name="fnet_channel_dft",
    )(*([h] * nx), mod, mod, g_all, cs)


def _fnet_b_body(zr_ref, zi_ref, m_ref, o_ref, v_ref):
    for n1 in range(FN_R):
        zr, zi = zr_ref[0, n1], zi_ref[0, n1]
        k1 = _dot(m_ref[n1, 0], (zr.astype(F32) + zi.astype(F32)).astype(BF16))
        k3 = _dot(m_ref[n1, 1], zi)
        k2 = _dot(m_ref[n1, 2], zr)
        v_ref[n1, :FN_M, :] = k1 - k3
        v_ref[n1, FN_M:, :] = k1 + k2

    rt = np.float32(np.sqrt(0.5))

    def chunk(i, carry):
        r0 = pl.multiple_of(i * FN_CH, FN_CH)
        for lt in range(FN_LANES // LANES):
            ls = slice(lt * LANES, (lt + 1) * LANES)
            re = [v_ref[n, pl.ds(r0, FN_CH), ls] for n in range(FN_R)]
            im = [v_ref[n, pl.ds(FN_M + r0, FN_CH), ls] for n in range(FN_R)]
            e0 = (re[0] + re[4]) + (re[2] + re[6])
            e2 = (re[0] + re[4]) - (re[2] + re[6])
            e1 = (re[0] - re[4]) + (im[2] - im[6])
            e3 = (re[0] - re[4]) - (im[2] - im[6])
            t0r, t0i = re[1] + re[5], im[1] + im[5]
            t1r, t1i = re[1] - re[5], im[1] - im[5]
            t2r, t2i = re[3] + re[7], im[3] + im[7]
            t3r, t3i = re[3] - re[7], im[3] - im[7]
            p0 = t0r + t2r
            p2 = t0i - t2i
            al = t1r - t3r
            be = t1i + t3i
            p1 = (al + be) * rt
            p3 = (be - al) * rt
            ys = (e0 + p0, e1 + p1, e2 + p2, e3 + p3, e0 - p0, e1 - p1, e2 - p2, e3 - p3)
            for k1 in range(FN_R):
                o_ref[0, pl.ds(k1 * FN_M + r0, FN_CH), ls] = ys[k1].astype(BF16)
        return carry

    lax.fori_loop(0, FN_M // FN_CH, chunk, 0)


def _fnet_b(zr, zi, mtab):
    assert FN_R == 8
    nb = zr.shape[0]
    z_spec = pl.BlockSpec((1, FN_R, FN_M, FN_LANES), lambda b, l: (b, 0, 0, l))
    return pl.pallas_call(
        _fnet_b_body,
        grid=(nb, D_MODEL // FN_LANES),
        in_specs=[z_spec, z_spec, _resident((FN_R, 3, FN_M, FN_M), lambda b, l: (0, 0, 0, 0))],
        out_specs=pl.BlockSpec((1, SEQ, FN_LANES), lambda b, l: (b, 0, l)),
        out_shape=jax.ShapeDtypeStruct((nb, SEQ, D_MODEL), BF16),
        scratch_shapes=[pltpu.VMEM((FN_R, 2 * FN_M, FN_LANES), F32)],
        compiler_params=_params("arbitrary", "arbitrary"),
        name="fnet_seq_dft",
    )(zr, zi, mtab)


def kernel(x, c, ctx, c_ctx, ada_w, ada_b, norm_g, ffn_w_in, ffn_w_out, ab_w_in, conv_w, conv_b,
           conv_ln_g, conv_ln_b, na_rpb, ab_w_out, fnet_w, fnet_b, final_g):
    nb, n, d = x.shape
    depth = ada_w.shape[0]
    nctx = ctx.shape[1]
    assert (d, depth, nb) == (D_MODEL, 2, 4) and n == GRID_W * GRID_W

    cc = jnp.concatenate([c, c_ctx[None], jnp.zeros((MOD_ROWS - nb - 1, d), F32)], axis=0)
    mod = _ada(cc, ada_w, ada_b).reshape(depth * MOD_ROWS, 1, N_MOD * d)
    ctx_row = nb
    lyr1 = MOD_ROWS

    g_all = norm_g.reshape(depth * 3, 1, d)
    fg = final_g.reshape(1, 1, d)
    w_in = ffn_w_in.reshape(depth * 2, d, 2 * D_FF)
    w_out = ffn_w_out.reshape(depth * 2, D_FF, d)

    h = _ffn(x, mod, 0, 0, g_all, 0, w_in, w_out, 0, fg, False)
    hc = _ffn(ctx.reshape(1, nb * nctx, d), mod, ctx_row, 0, g_all, 0, w_in, w_out, 0, fg, False)
    assert ab_w_in.shape[0] == 1
    w_ab = ab_w_in.reshape(ab_w_in.shape[1:])
    u, qkv = _proj(h, mod, 0, 3, g_all, 1, w_ab)
    qkv_c = _qkv_proj(hc, mod, ctx_row, 3, g_all, 1, w_ab)
    conv_x = _conv(u, conv_w[0], conv_b[0], conv_ln_g[0], conv_ln_b[0])
    att_x = _natten(qkv, qkv_c.reshape(nb, nctx, 3 * D_NA), _na_bias_table(na_rpb[0]))
    h = _ffn(h, mod, 0, 6, g_all, 2, w_in, w_out, 1, fg, False,
             mix=((conv_x, att_x), ab_w_out[0].astype(BF16), None, 5))

    h = _ffn(h, mod, lyr1, 0, g_all, 3, w_in, w_out, 2, fg, False)
    gw = d // FNET_GROUPS
    cc_tab, sc_tab = _dft_tables(gw)
    cs = jnp.asarray(np.concatenate([cc_tab, -sc_tab], axis=1)).astype(BF16)
    zr, zi = _fnet_a(h, mod, lyr1, 3, g_all, 4, cs)
    f = _fnet_b(zr, zi, jnp.asarray(_seq_tables()).astype(BF16))
    return _ffn(h, mod, lyr1, 6, g_all, 5, w_in, w_out, 3, fg, True,
                mix=((f,), fnet_w[0].astype(BF16), fnet_b[0].reshape(1, d), 5))
```

```python
import functools

import numpy as np
import jax
import jax.numpy as jnp
from jax import lax
from jax.experimental import pallas as pl
from jax.experimental.pallas import tpu as pltpu

D_MODEL = 1024
GRID_W = 64
D_CONV = 512
D_NA = 512
NA_HEADS = 8
HEAD_DIM = 64
CONV_WIDTH = 31
NA_KH = 8
NA_KW = 16
FNET_GROUPS = 4
D_FF = 2816
N_MOD = 9
EPS = 1e-6

BF16 = jnp.bfloat16
F32 = jnp.float32

VMEM_LIMIT = 56 * 1024 * 1024
FFN_VMEM_LIMIT = 60 * 1024 * 1024
SUBLANES = 8
LANES = 128
MOD_ROWS = SUBLANES
TM = 1024
FF_CHUNK = 256
CONV_TL = 512
CONV_HALO = 16
NA_ROWS = 8
NA_QB = NA_ROWS * GRID_W
NA_KROWS = 16
NA_KB = NA_KROWS * GRID_W
NEG = -1e30


def _params(*sem):
    return pltpu.CompilerParams(dimension_semantics=sem, vmem_limit_bytes=VMEM_LIMIT)


def _resident(shape, index_map):
    return pl.BlockSpec(shape, index_map, pipeline_mode=pl.Buffered(1))


def _silu(x):
    return x * (1.0 / (1.0 + jnp.exp(-x)))


def _dot(a, b):
    return jnp.dot(a, b, preferred_element_type=F32)


def _rms(x, g):
    return x * lax.rsqrt(jnp.mean(x * x, axis=-1, keepdims=True) + EPS) * g


def _modnorm(x, g, shift, scale):
    return (_rms(x, g) * (1.0 + scale) + shift).astype(BF16)


ADA_TN = 512
ADA_SPLIT = 6


def _ada_body(cc_ref, *refs):
    w_refs, b_ref, o_ref = refs[:ADA_SPLIT], refs[ADA_SPLIT], refs[ADA_SPLIT + 1]
    s = _silu(cc_ref[...]).astype(BF16)
    for q, w_ref in enumerate(w_refs):
        cols = slice(q * ADA_TN, (q + 1) * ADA_TN)
        o_ref[0, :, cols] = _dot(s, w_ref[0].astype(BF16)) + b_ref[0, :, cols]


def _ada(cc, ada_w, ada_b):
    depth, _, n = ada_w.shape
    step = ADA_SPLIT * ADA_TN

    def w_spec(q):
        return pl.BlockSpec((1, D_MODEL, ADA_TN), lambda i, j: (i, 0, ADA_SPLIT * j + q))

    return pl.pallas_call(
        _ada_body,
        grid=(depth, n // step),
        in_specs=[pl.BlockSpec((MOD_ROWS, D_MODEL), lambda i, j: (0, 0))]
        + [w_spec(q) for q in range(ADA_SPLIT)]
        + [pl.BlockSpec((1, 1, step), lambda i, j: (i, 0, j))],
        out_specs=pl.BlockSpec((1, MOD_ROWS, step), lambda i, j: (i, 0, j)),
        out_shape=jax.ShapeDtypeStruct((depth, MOD_ROWS, n), F32),
        compiler_params=_params("arbitrary", "arbitrary"),
        name="ada_mod",
    )(cc, *([ada_w] * ADA_SPLIT), ada_b.reshape(depth, 1, n))


def _mod_spec(row0, k):
    return pl.BlockSpec((1, 1, D_MODEL), lambda b, m: (row0 + b, 0, k))


def _const_row_spec(row):
    return pl.BlockSpec((1, 1, D_MODEL), lambda b, m: (row, 0, 0))


W_SLOTS = 2


def _weight_copies(win_hbm, wout_hbm, st_in, st_out, sem, w_idx, j, slot):
    lo = j * FF_CHUNK
    return (
        pltpu.make_async_copy(win_hbm.at[w_idx, :, pl.ds(lo, FF_CHUNK)], st_in.at[slot, 0], sem.at[slot, 0]),
        pltpu.make_async_copy(win_hbm.at[w_idx, :, pl.ds(D_FF + lo, FF_CHUNK)], st_in.at[slot, 1], sem.at[slot, 1]),
        pltpu.make_async_copy(wout_hbm.at[w_idx, pl.ds(lo, FF_CHUNK), :], st_out.at[slot], sem.at[slot, 2]),
    )


def _start_stream(copies):
    for j in range(W_SLOTS):
        for cp in copies(j, j):
            cp.start()


def _land_chunk(copies, j, win_ref, wout_ref, st_in, st_out):
    slot, lo = j % W_SLOTS, j * FF_CHUNK
    for cp in copies(j, slot):
        cp.wait()
    win_ref[:, lo:lo + FF_CHUNK] = st_in[slot, 0].astype(BF16)
    win_ref[:, D_FF + lo:D_FF + lo + FF_CHUNK] = st_in[slot, 1].astype(BF16)
    wout_ref[lo:lo + FF_CHUNK, :] = st_out[slot].astype(BF16)
    if j + W_SLOTS < D_FF // FF_CHUNK:
        for cp in copies(j + W_SLOTS, slot):
            cp.start()


def _ffn_compute(stream, x_ref, sh_ref, sc_ref, gt_ref, g_ref, win_hbm, wout_hbm, fg_ref, mix_refs,
                 o_ref, mid_ref, win_ref, wout_ref, st_in, st_out, sem, *, final, n_mix, mix_bias, w_idx):
    copies = functools.partial(_weight_copies, win_hbm, wout_hbm, st_in, st_out, sem, w_idx)
    if stream:
        _start_stream(copies)
    x = x_ref[0]
    if n_mix:
        acts, mg_ref, mw_ref = mix_refs[:n_mix], mix_refs[n_mix], mix_refs[n_mix + 1]
        y0, r = None, 0
        for a_ref in acts:
            k = a_ref.shape[-1]
            t = _dot(a_ref[0], mw_ref[r:r + k, :])
            y0 = t if y0 is None else y0 + t
            r += k
        if mix_bias:
            y0 = y0 + mix_refs[n_mix + 2][...]
        x = x + mg_ref[0] * y0
    xb = _modnorm(x, g_ref[0], sh_ref[0], sc_ref[0])
    for j in range(D_FF // FF_CHUNK):
        lo = j * FF_CHUNK
        if stream:
            _land_chunk(copies, j, win_ref, wout_ref, st_in, st_out)
        gate = _dot(xb, win_ref[:, lo:lo + FF_CHUNK])
        up = _dot(xb, win_ref[:, D_FF + lo:D_FF + lo + FF_CHUNK])
        mid_ref[:, lo:lo + FF_CHUNK] = (_silu(gate) * up).astype(BF16)
    y = _dot(mid_ref[...], wout_ref[...])
    h = x + (0.5 * gt_ref[0]) * y
    if final:
        h = _rms(h, fg_ref[0])
    o_ref[0] = h


def _preload_weights(win_hbm, wout_hbm, win_ref, wout_ref, st_in, st_out, sem, w_idx):
    copies = functools.partial(_weight_copies, win_hbm, wout_hbm, st_in, st_out, sem, w_idx)
    _start_stream(copies)
    for j in range(D_FF // FF_CHUNK):
        _land_chunk(copies, j, win_ref, wout_ref, st_in, st_out)


def _ffn_body(x_ref, sh_ref, sc_ref, gt_ref, g_ref, win_hbm, wout_hbm, fg_ref, *rest, single_step, **static):
    args = (x_ref, sh_ref, sc_ref, gt_ref, g_ref, win_hbm, wout_hbm, fg_ref, rest[:-7], *rest[-7:])
    is_first = (pl.program_id(0) == 0) & (pl.program_id(1) == 0)
    if single_step:
        _ffn_compute(True, *args, **static)
    elif static["n_mix"]:
        pl.when(is_first)(lambda: _preload_weights(win_hbm, wout_hbm, *rest[-5:], static["w_idx"]))
        _ffn_compute(False, *args, **static)
    else:
        pl.when(is_first)(lambda: _ffn_compute(True, *args, **static))
        pl.when(jnp.logical_not(is_first))(lambda: _ffn_compute(False, *args, **static))


def _ffn(h, mod, row0, k0, g_all, g_row, w_in, w_out, w_idx, final_g, final, mix=None):
    nb, n, _ = h.shape
    tm = min(TM, n)
    in_specs = [
        pl.BlockSpec((1, tm, D_MODEL), lambda b, m: (b, m, 0)),
        _mod_spec(row0, k0), _mod_spec(row0, k0 + 1), _mod_spec(row0, k0 + 2),
        _const_row_spec(g_row),
        pl.BlockSpec(memory_space=pl.ANY),
        pl.BlockSpec(memory_space=pl.ANY),
        _const_row_spec(0),
    ]
    args = [h, mod, mod, mod, g_all, w_in, w_out, final_g]
    n_mix, mix_bias = 0, False
    if mix is not None:
        acts, mw, mb, mk = mix
        n_mix, mix_bias = len(acts), mb is not None
        in_specs += [pl.BlockSpec((1, tm, a.shape[-1]), lambda b, m: (b, m, 0)) for a in acts]
        in_specs += [_mod_spec(row0, mk), _resident(mw.shape, lambda b, m: (0, 0))]
        args += [*acts, mod, mw]
        if mix_bias:
            in_specs.append(pl.BlockSpec((1, D_MODEL), lambda b, m: (0, 0)))
            args.append(mb)
    return pl.pallas_call(
        functools.partial(_ffn_body, final=final, n_mix=n_mix, mix_bias=mix_bias, w_idx=w_idx,
                          single_step=(nb * (n // tm) == 1)),
        grid=(nb, n // tm),
        in_specs=in_specs,
        out_specs=pl.BlockSpec((1, tm, D_MODEL), lambda b, m: (b, m, 0)),
        out_shape=jax.ShapeDtypeStruct(h.shape, F32),
        scratch_shapes=[
            pltpu.VMEM((tm, D_FF), BF16),
            pltpu.VMEM((D_MODEL, 2 * D_FF), BF16),
            pltpu.VMEM((D_FF, D_MODEL), BF16),
            pltpu.VMEM((W_SLOTS, 2, D_MODEL, FF_CHUNK), F32),
            pltpu.VMEM((W_SLOTS, FF_CHUNK, D_MODEL), F32),
            pltpu.SemaphoreType.DMA((W_SLOTS, 3)),
        ],
        compiler_params=pltpu.CompilerParams(dimension_semantics=("arbitrary", "arbitrary"),
                                             vmem_limit_bytes=FFN_VMEM_LIMIT),
        name="ffn_final" if final else "ffn",
    )(*args)


def _proj_body(x_ref, sh_ref, sc_ref, g_ref, w_ref, u_ref, qkv_ref, wbf_ref):
    @pl.when((pl.program_id(0) == 0) & (pl.program_id(1) == 0))
    def _():
        wbf_ref[...] = w_ref[...].astype(BF16)

    xb = _modnorm(x_ref[0], g_ref[0], sh_ref[0], sc_ref[0])
    u_ref[0] = _dot(xb, wbf_ref[:, :2 * D_CONV])
    qkv_ref[0] = _dot(xb, wbf_ref[:, 2 * D_CONV:]).astype(BF16)


def _proj(h, mod, row0, k0, g_all, g_row, w):
    nb, n, _ = h.shape
    return pl.pallas_call(
        _proj_body,
        grid=(nb, n // TM),
        in_specs=[
            pl.BlockSpec((1, TM, D_MODEL), lambda b, m: (b, m, 0)),
            _mod_spec(row0, k0), _mod_spec(row0, k0 + 1),
            _const_row_spec(g_row),
            _resident(w.shape, lambda b, m: (0, 0)),
        ],
        out_specs=[
            pl.BlockSpec((1, TM, 2 * D_CONV), lambda b, m: (b, m, 0)),
            pl.BlockSpec((1, TM, 3 * D_NA), lambda b, m: (b, m, 0)),
        ],
        out_shape=[
            jax.ShapeDtypeStruct((nb, n, 2 * D_CONV), F32),
            jax.ShapeDtypeStruct((nb, n, 3 * D_NA), BF16),
        ],
        scratch_shapes=[pltpu.VMEM(w.shape, BF16)],
        compiler_params=_params("arbitrary", "arbitrary"),
        name="ab_proj",
    )(h, mod, mod, g_all, w)


def _qkv_body(x_ref, sh_ref, sc_ref, g_ref, w_ref, qkv_ref):
    xb = _modnorm(x_ref[0], g_ref[0], sh_ref[0], sc_ref[0])
    qkv_ref[0] = _dot(xb, w_ref[:, 2 * D_CONV:].astype(BF16)).astype(BF16)


def _qkv_proj(h, mod, row0, k0, g_all, g_row, w):
    nb, n, _ = h.shape
    tm = min(TM, n)
    return pl.pallas_call(
        _qkv_body,
        grid=(nb, n // tm),
        in_specs=[
            pl.BlockSpec((1, tm, D_MODEL), lambda b, m: (b, m, 0)),
            _mod_spec(row0, k0), _mod_spec(row0, k0 + 1),
            _const_row_spec(g_row),
            _resident(w.shape, lambda b, m: (0, 0)),
        ],
        out_specs=pl.BlockSpec((1, tm, 3 * D_NA), lambda b, m: (b, m, 0)),
        out_shape=jax.ShapeDtypeStruct((nb, n, 3 * D_NA), BF16),
        compiler_params=_params("arbitrary", "arbitrary"),
        name="ctx_qkv",
    )(h, mod, mod, g_all, w)


def _glu(v):
    return v[:, :D_CONV] * (1.0 / (1.0 + jnp.exp(-v[:, D_CONV:])))


CONV_RC = 64
CONV_N = CONV_TL + 2 * CONV_HALO


def _conv_body(cur_ref, prev_ref, next_ref, w_ref, b_ref, lg_ref, lb_ref, o_ref, y_ref):
    t = pl.program_id(1)
    nt = pl.num_programs(1)
    y_ref[0, CONV_HALO:CONV_HALO + CONV_TL, :] = _glu(cur_ref[0])
    y_ref[0, 0:CONV_HALO, :] = jnp.where(t > 0, _glu(prev_ref[0]), 0.0)
    y_ref[0, CONV_HALO + CONV_TL:, :] = jnp.where(t < nt - 1, _glu(next_ref[0]), 0.0)
    for s in range(1, SUBLANES):
        y_ref[s, 0:CONV_N - SUBLANES, :] = y_ref[0, s:s + CONV_N - SUBLANES, :]
    off = CONV_HALO - CONV_WIDTH // 2
    for r in range(0, CONV_TL, CONV_RC):
        acc = jnp.zeros((CONV_RC // SUBLANES, SUBLANES, D_CONV), F32)
        for k in range(CONV_WIDTH):
            m8, s = divmod(off + k, SUBLANES)
            lo = r + SUBLANES * m8
            yk = y_ref[s, lo:lo + CONV_RC, :].reshape(CONV_RC // SUBLANES, SUBLANES, D_CONV)
            acc = acc + w_ref[k][None] * yk
        acc = acc.reshape(CONV_RC, D_CONV) + b_ref[...]
        mu = jnp.mean(acc, axis=-1, keepdims=True)
        cen = acc - mu
        var = jnp.mean(cen * cen, axis=-1, keepdims=True)
        z = cen * lax.rsqrt(var + EPS) * lg_ref[...] + lb_ref[...]
        o_ref[0, r:r + CONV_RC, :] = _silu(z).astype(BF16)


def _conv(u, w, b, ln_g, ln_b):
    nb, n, _ = u.shape
    nt = n // CONV_TL
    hb = CONV_TL // CONV_HALO
    last = n // CONV_HALO - 1
    row = lambda v: v.reshape(1, D_CONV)
    w_rep = jnp.broadcast_to(w[:, None, :], (CONV_WIDTH, SUBLANES, D_CONV))
    return pl.pallas_call(
        _conv_body,
        grid=(nb, nt),
        in_specs=[
            pl.BlockSpec((1, CONV_TL, 2 * D_CONV), lambda b_, t: (b_, t, 0)),
            pl.BlockSpec((1, CONV_HALO, 2 * D_CONV), lambda b_, t: (b_, jnp.maximum(t * hb - 1, 0), 0)),
            pl.BlockSpec((1, CONV_HALO, 2 * D_CONV), lambda b_, t: (b_, jnp.minimum((t + 1) * hb, last), 0)),
            pl.BlockSpec((CONV_WIDTH, SUBLANES, D_CONV), lambda b_, t: (0, 0, 0)),
            pl.BlockSpec((1, D_CONV), lambda b_, t: (0, 0)),
            pl.BlockSpec((1, D_CONV), lambda b_, t: (0, 0)),
            pl.BlockSpec((1, D_CONV), lambda b_, t: (0, 0)),
        ],
        out_specs=pl.BlockSpec((1, CONV_TL, D_CONV), lambda b_, t: (b_, t, 0)),
        out_shape=jax.ShapeDtypeStruct((nb, n, D_CONV), BF16),
        scratch_shapes=[pltpu.VMEM((SUBLANES, CONV_N, D_CONV), F32)],
        compiler_params=_params("arbitrary", "arbitrary"),
        name="conv_module",
    )(u, u, u, w_rep, row(b), row(ln_g), row(ln_b))


NA_HG = 4
NA_DR_PAD = 8
NA_T2 = 2 * NA_KH - 1 + 2 * NA_DR_PAD - 1


def _na_bias_table(rpb):
    qc = np.arange(GRID_W)
    cs = np.clip(qc - NA_KW // 2, 0, GRID_W - NA_KW)
    kc = np.arange(GRID_W)
    col_valid = (kc[None, :] >= cs[:, None]) & (kc[None, :] < cs[:, None] + NA_KW)
    dc = kc[None, :] - qc[:, None] + NA_KW - 1
    oh_c = np.zeros((2 * NA_KW - 1, GRID_W, GRID_W), np.float32)
    qi, ki = np.nonzero(col_valid)
    oh_c[dc[qi, ki], qi, ki] = 1.0
    t_col = jnp.einsum('hrd,dqk->hrqk', rpb, jnp.asarray(oh_c), precision=lax.Precision.HIGHEST)
    t_col = jnp.where(jnp.asarray(col_valid)[None, None], t_col, NEG)
    t_pad = jnp.pad(t_col, ((0, 0), (NA_DR_PAD, NA_DR_PAD), (0, 0), (0, 0)))
    return jnp.concatenate([t_pad[:, :NA_T2], t_pad[:, 1:NA_T2 + 1]], axis=-1)


def _na_window(blk):
    w0 = min(max(NA_ROWS * blk - NA_KH // 2, 0), GRID_W - NA_KROWS)
    a_lo = [min(max(NA_ROWS * blk + i - NA_KH // 2, 0), GRID_W - NA_KH) - w0 for i in range(NA_ROWS)]
    return w0, a_lo


def _na_probs(t2_ref, blk, j, s, sc):
    w0, a_lo = _na_window(blk)
    base = w0 - NA_ROWS * blk + NA_KH - 1 + NA_DR_PAD
    plane = lax.broadcasted_iota(jnp.int32, (1, LANES), 1)
    zero = jnp.zeros((GRID_W, LANES), BF16)
    p_rows, pc_rows = [], []
    for i in range(NA_ROWS):
        rows = slice(i * GRID_W, (i + 1) * GRID_W)
        pieces = {}
        for p in range(NA_KROWS // 2):
            first = a_lo[i] <= 2 * p < a_lo[i] + NA_KH
            second = a_lo[i] <= 2 * p + 1 < a_lo[i] + NA_KH
            if not (first or second):
                continue
            piece = s[rows, p * LANES:(p + 1) * LANES] + t2_ref[j, base + 2 * p - i]
            if not (first and second):
                piece = jnp.where((plane < GRID_W) if first else (plane >= GRID_W), piece, NEG)
            pieces[p] = piece
        sc_i = sc[rows]
        m = jnp.max(sc_i, axis=-1, keepdims=True)
        for piece in pieces.values():
            m = jnp.maximum(m, jnp.max(piece, axis=-1, keepdims=True))
        p_rows.append(jnp.concatenate(
            [jnp.exp(pieces[p] - m).astype(BF16) if p in pieces else zero for p in range(NA_KROWS // 2)], axis=-1))
        pc_rows.append(jnp.exp(sc_i - m).astype(BF16))
    return jnp.concatenate(p_rows, axis=0), jnp.concatenate(pc_rows, axis=0)


def _na_body(q_ref, k_ref, v_ref, kc_ref, vc_ref, t2_ref, o_ref):
    blk = pl.program_id(2)
    w0 = jnp.clip(NA_ROWS * blk - NA_KH // 2, 0, GRID_W - NA_KROWS)
    start = pl.multiple_of(w0 * GRID_W, (NA_KH // 2) * GRID_W)
    q2 = q_ref[0] * jnp.asarray(HEAD_DIM ** -0.5, BF16)
    kw = k_ref[0, pl.ds(start, NA_KB), :]
    vw = v_ref[0, pl.ds(start, NA_KB), :]
    kc = kc_ref[0]
    vc = vc_ref[0]
    lane = lax.broadcasted_iota(jnp.int32, (1, NA_HG * HEAD_DIM), 1)
    nt = (((1,), (1,)), ((), ()))
    one = jnp.ones((), BF16)

    def run(probs_fn):
        out = None
        for j in range(NA_HG):
            in_head = (lane >= j * HEAD_DIM) & (lane < (j + 1) * HEAD_DIM)
            qm = jnp.where(in_head, q2, jnp.zeros_like(q2))
            s = lax.dot_general(qm, kw, nt, preferred_element_type=F32)
            sc = lax.dot_general(qm, kc, nt, preferred_element_type=F32)
            p, pc = probs_fn(j, s, sc)
            o = _dot(p, jnp.where(in_head, vw, one)) + _dot(pc, jnp.where(in_head, vc, one))
            o = o * (1.0 / pltpu.roll(o, HEAD_DIM, axis=1))
            out = o if out is None else jnp.where(in_head, o, out)
        o_ref[0] = out.astype(BF16)

    last = GRID_W // NA_ROWS - 1
    pl.when(blk == 0)(lambda: run(functools.partial(_na_probs, t2_ref, 0)))
    pl.when((blk > 0) & (blk < last))(lambda: run(functools.partial(_na_probs, t2_ref, 1)))
    pl.when(blk == last)(lambda: run(functools.partial(_na_probs, t2_ref, last)))


def _natten(qkv, qkv_c, t2):
    nb, n, _ = qkv.shape
    nctx = qkv_c.shape[1]
    ng = NA_HEADS // NA_HG
    lanes = NA_HG * HEAD_DIM
    return pl.pallas_call(
        _na_body,
        grid=(ng, nb, n // NA_QB),
        in_specs=[
            pl.BlockSpec((1, NA_QB, lanes), lambda h, b, i: (b, i, h)),
            pl.BlockSpec((1, n, lanes), lambda h, b, i: (b, 0, ng + h)),
            pl.BlockSpec((1, n, lanes), lambda h, b, i: (b, 0, 2 * ng + h)),
            pl.BlockSpec((1, nctx, lanes), lambda h, b, i: (b, 0, ng + h)),
            pl.BlockSpec((1, nctx, lanes), lambda h, b, i: (b, 0, 2 * ng + h)),
            pl.BlockSpec((NA_HG, NA_T2, GRID_W, 2 * GRID_W), lambda h, b, i: (h, 0, 0, 0)),
        ],
        out_specs=pl.BlockSpec((1, NA_QB, lanes), lambda h, b, i: (b, i, h)),
        out_shape=jax.ShapeDtypeStruct((nb, n, D_NA), BF16),
        compiler_params=_params("arbitrary", "arbitrary", "arbitrary"),
        name="natten",
    )(qkv, qkv, qkv, qkv_c, qkv_c, t2)


SEQ = GRID_W * GRID_W
FN_R = SUBLANES
FN_M = SEQ // FN_R
FN_TA = 2048
FN_LANES = 256
FN_CH = 16


def _dft_tables(n):
    idx = np.arange(n, dtype=np.int64)
    ang = 2.0 * np.pi * ((idx[:, None] * idx[None, :]) % n).astype(np.float64) / n
    scale = 1.0 / np.sqrt(n)
    return (np.cos(ang) * scale).astype(np.float32), (np.sin(ang) * scale).astype(np.float32)


def _seq_tables():
    k2 = np.arange(FN_M, dtype=np.int64)
    n2 = np.arange(FN_M, dtype=np.int64)
    out = np.zeros((FN_R, 3, FN_M, FN_M), np.float32)
    for n1 in range(FN_R):
        num = (k2[:, None] * n2[None, :] * FN_R + n1 * k2[:, None]) % SEQ
        ang = 2.0 * np.pi * num.astype(np.float64) / SEQ
        c = np.cos(ang) / np.sqrt(SEQ)
        s = np.sin(ang) / np.sqrt(SEQ)
        out[n1, 0] = c
        out[n1, 1] = c - s
        out[n1, 2] = -(c + s)
    return out


def _fnet_a_body(*refs):
    nx = D_MODEL // LANES
    x_refs = refs[:nx]
    sh_ref, sc_ref, g_ref, cs_ref, a_ref, b_ref = refs[nx:]
    gw = D_MODEL // FNET_GROUPS
    for s in range(FN_R):
        xs = jnp.concatenate([x[0, pl.ds(s, FN_TA // FN_R, stride=FN_R), :] for x in x_refs], axis=-1)
        xb = _modnorm(xs, g_ref[0], sh_ref[0], sc_ref[0])
        for grp in range(FNET_GROUPS):
            ab = _dot(xb[:, grp * gw:(grp + 1) * gw], cs_ref[...])
            a_ref[0, s, :, grp * gw:(grp + 1) * gw] = ab[:, :gw].astype(BF16)
            b_ref[0, s, :, grp * gw:(grp + 1) * gw] = ab[:, gw:].astype(BF16)


def _fnet_a(h, mod, row0, k0, g_all, g_row, cs):
    nb, n, _ = h.shape
    gw = D_MODEL // FNET_GROUPS
    nx = D_MODEL // LANES
    out_spec = pl.BlockSpec((1, FN_R, FN_TA // FN_R, D_MODEL), lambda b, m: (b, 0, m, 0))
    return pl.pallas_call(
        _fnet_a_body,
        grid=(nb, n // FN_TA),
        in_specs=[pl.BlockSpec((1, FN_TA, LANES), functools.partial(lambda b, m, c: (b, m, c), c=c))
                  for c in range(nx)]
        + [_mod_spec(row0, k0), _mod_spec(row0, k0 + 1), _const_row_spec(g_row),
           _resident((gw, 2 * gw), lambda b, m: (0, 0))],
        out_specs=[out_spec, out_spec],
        out_shape=[jax.ShapeDtypeStruct((nb, FN_R, n // FN_R, D_MODEL), BF16)] * 2,
        compiler_params=_params("arbitrary", "arbitrary"),
        name="fnet_channel_dft",
    )(*([h] * nx), mod, mod, g_all, cs)


def _fnet_b_body(zr_ref, zi_ref, m_ref, o_ref, v_ref):
    for n1 in range(FN_R):
        zr, zi = zr_ref[0, n1], zi_ref[0, n1]
        k1 = _dot(m_ref[n1, 0], (zr.astype(F32) + zi.astype(F32)).astype(BF16))
        k3 = _dot(m_ref[n1, 1], zi)
        k2 = _dot(m_ref[n1, 2], zr)
        v_ref[n1, :FN_M, :] = k1 - k3
        v_ref[n1, FN_M:, :] = k1 + k2

    rt = np.float32(np.sqrt(0.5))

    def chunk(i, carry):
        r0 = pl.multiple_of(i * FN_CH, FN_CH)
        for lt in range(FN_LANES // LANES):
            ls = slice(lt * LANES, (lt + 1) * LANES)
            re = [v_ref[n, pl.ds(r0, FN_CH), ls] for n in range(FN_R)]
            im = [v_ref[n, pl.ds(FN_M + r0, FN_CH), ls] for n in range(FN_R)]
            e0 = (re[0] + re[4]) + (re[2] + re[6])
            e2 = (re[0] + re[4]) - (re[2] + re[6])
            e1 = (re[0] - re[4]) + (im[2] - im[6])
            e3 = (re[0] - re[4]) - (im[2] - im[6])
            t0r, t0i = re[1] + re[5], im[1] + im[5]
            t1r, t1i = re[1] - re[5], im[1] - im[5]
            t2r, t2i = re[3] + re[7], im[3] + im[7]
            t3r, t3i = re[3] - re[7], im[3] - im[7]
            p0 = t0r + t2r
            p2 = t0i - t2i
            al = t1r - t3r
            be = t1i + t3i
            p1 = (al + be) * rt
            p3 = (be - al) * rt
            ys = (e0 + p0, e1 + p1, e2 + p2, e3 + p3, e0 - p0, e1 - p1, e2 - p2, e3 - p3)
            for k1 in range(FN_R):
                o_ref[0, pl.ds(k1 * FN_M + r0, FN_CH), ls] = ys[k1].astype(BF16)
        return carry

    lax.fori_loop(0, FN_M // FN_CH, chunk, 0)


def _fnet_b(zr, zi, mtab):
    assert FN_R == 8
    nb = zr.shape[0]
    z_spec = pl.BlockSpec((1, FN_R, FN_M, FN_LANES), lambda b, l: (b, 0, 0, l))
    return pl.pallas_call(
        _fnet_b_body,
        grid=(nb, D_MODEL // FN_LANES),
        in_specs=[z_spec, z_spec, _resident((FN_R, 3, FN_M, FN_M), lambda b, l: (0, 0, 0, 0))],
        out_specs=pl.BlockSpec((1, SEQ, FN_LANES), lambda b, l: (b, 0, l)),
        out_shape=jax.ShapeDtypeStruct((nb, SEQ, D_MODEL), BF16),
        scratch_shapes=[pltpu.VMEM((FN_R, 2 * FN_M, FN_LANES), F32)],
        compiler_params=_params("arbitrary", "arbitrary"),
        name="fnet_seq_dft",
    )(zr, zi, mtab)


def kernel(x, c, ctx, c_ctx, ada_w, ada_b, norm_g, ffn_w_in, ffn_w_out, ab_w_in, conv_w, conv_b,
           conv_ln_g, conv_ln_b, na_rpb, ab_w_out, fnet_w, fnet_b, final_g):
    nb, n, d = x.shape
    depth = ada_w.shape[0]
    nctx = ctx.shape[1]
    assert (d, depth, nb) == (D_MODEL, 2, 4) and n == GRID_W * GRID_W

    cc = jnp.concatenate([c, c_ctx[None], jnp.zeros((MOD_ROWS - nb - 1, d), F32)], axis=0)
    mod = _ada(cc, ada_w, ada_b).reshape(depth * MOD_ROWS, 1, N_MOD * d)
    ctx_row = nb
    lyr1 = MOD_ROWS

    g_all = norm_g.reshape(depth * 3, 1, d)
    fg = final_g.reshape(1, 1, d)
    w_in = ffn_w_in.reshape(depth * 2, d, 2 * D_FF)
    w_out = ffn_w_out.reshape(depth * 2, D_FF, d)

    h = _ffn(x, mod, 0, 0, g_all, 0, w_in, w_out, 0, fg, False)
    hc = _ffn(ctx.reshape(1, nb * nctx, d), mod, ctx_row, 0, g_all, 0, w_in, w_out, 0, fg, False)
    assert ab_w_in.shape[0] == 1
    w_ab = ab_w_in.reshape(ab_w_in.shape[1:])
    u, qkv = _proj(h, mod, 0, 3, g_all, 1, w_ab)
    qkv_c = _qkv_proj(hc, mod, ctx_row, 3, g_all, 1, w_ab)
    conv_x = _conv(u, conv_w[0], conv_b[0], conv_ln_g[0], conv_ln_b[0])
    att_x = _natten(qkv, qkv_c.reshape(nb, nctx, 3 * D_NA), _na_bias_table(na_rpb[0]))
    h = _ffn(h, mod, 0, 6, g_all, 2, w_in, w_out, 1, fg, False,
             mix=((conv_x, att_x), ab_w_out[0].astype(BF16), None, 5))

    h = _ffn(h, mod, lyr1, 0, g_all, 3, w_in, w_out, 2, fg, False)
    gw = d // FNET_GROUPS
    cc_tab, sc_tab = _dft_tables(gw)
    cs = jnp.asarray(np.concatenate([cc_tab, -sc_tab], axis=1)).astype(BF16)
    zr, zi = _fnet_a(h, mod, lyr1, 3, g_all, 4, cs)
    f = _fnet_b(zr, zi, jnp.asarray(_seq_tables()).astype(BF16))
    return _ffn(h, mod, lyr1, 6, g_all, 5, w_in, w_out, 3, fg, True,
                mix=((f,), fnet_w[0].astype(BF16), fnet_b[0].reshape(1, d), 5))
```

```python
import functools

import numpy as np
import jax
import jax.numpy as jnp
from jax import lax
from jax.experimental import pallas as pl
from jax.experimental.pallas import tpu as pltpu

D_MODEL = 1024
GRID_W = 64
D_CONV = 512
D_NA = 512
NA_HEADS = 8
HEAD_DIM = 64
CONV_WIDTH = 31
NA_KH = 8
NA_KW = 16
FNET_GROUPS = 4
D_FF = 2816
N_MOD = 9
EPS = 1e-6

BF16 = jnp.bfloat16
F32 = jnp.float32

VMEM_LIMIT = 56 * 1024 * 1024
FFN_VMEM_LIMIT = 60 * 1024 * 1024
SUBLANES = 8
LANES = 128
MOD_ROWS = SUBLANES
TM = 1024
FF_CHUNK = 256
CONV_TL = 512
CONV_HALO = 16
NA_ROWS = 4
NA_QB = NA_ROWS * GRID_W
NA_KROWS = 12
NA_KB = NA_KROWS * GRID_W
NEG = -1e30


def _params(*sem):
    return pltpu.CompilerParams(dimension_semantics=sem, vmem_limit_bytes=VMEM_LIMIT)


def _resident(shape, index_map):
    return pl.BlockSpec(shape, index_map, pipeline_mode=pl.Buffered(1))


def _silu(x):
    return x * (1.0 / (1.0 + jnp.exp(-x)))


def _dot(a, b):
    return jnp.dot(a, b, preferred_element_type=F32)


def _rms(x, g):
    return x * lax.rsqrt(jnp.mean(x * x, axis=-1, keepdims=True) + EPS) * g


def _modnorm(x, g, shift, scale):
    return (_rms(x, g) * (1.0 + scale) + shift).astype(BF16)


ADA_TN = 768
ADA_SPLIT = 3


def _ada_body(cc_ref, *refs):
    w_refs, b_ref, o_ref = refs[:ADA_SPLIT], refs[ADA_SPLIT], refs[ADA_SPLIT + 1]
    s = _silu(cc_ref[...]).astype(BF16)
    for q, w_ref in enumerate(w_refs):
        cols = slice(q * ADA_TN, (q + 1) * ADA_TN)
        o_ref[0, :, cols] = _dot(s, w_ref[0].astype(BF16)) + b_ref[0, :, cols]


def _ada(cc, ada_w, ada_b):
    depth, _, n = ada_w.shape
    step = ADA_SPLIT * ADA_TN

    def w_spec(q):
        return pl.BlockSpec((1, D_MODEL, ADA_TN), lambda i, j: (i, 0, ADA_SPLIT * j + q))

    return pl.pallas_call(
        _ada_body,
        grid=(depth, n // step),
        in_specs=[pl.BlockSpec((MOD_ROWS, D_MODEL), lambda i, j: (0, 0))]
        + [w_spec(q) for q in range(ADA_SPLIT)]
        + [pl.BlockSpec((1, 1, step), lambda i, j: (i, 0, j))],
        out_specs=pl.BlockSpec((1, MOD_ROWS, step), lambda i, j: (i, 0, j)),
        out_shape=jax.ShapeDtypeStruct((depth, MOD_ROWS, n), F32),
        compiler_params=_params("arbitrary", "arbitrary"),
        name="ada_mod",
    )(cc, *([ada_w] * ADA_SPLIT), ada_b.reshape(depth, 1, n))


def _mod_spec(row0, k):
    return pl.BlockSpec((1, 1, D_MODEL), lambda b, m: (row0 + b, 0, k))


def _const_row_spec(row):
    return pl.BlockSpec((1, 1, D_MODEL), lambda b, m: (row, 0, 0))


W_SLOTS = 2


def _weight_copies(win_hbm, wout_hbm, st_in, st_out, sem, w_idx, j, slot):
    lo = j * FF_CHUNK
    return (
        pltpu.make_async_copy(win_hbm.at[w_idx, :, pl.ds(lo, FF_CHUNK)], st_in.at[slot, 0], sem.at[slot, 0]),
        pltpu.make_async_copy(win_hbm.at[w_idx, :, pl.ds(D_FF + lo, FF_CHUNK)], st_in.at[slot, 1], sem.at[slot, 1]),
        pltpu.make_async_copy(wout_hbm.at[w_idx, pl.ds(lo, FF_CHUNK), :], st_out.at[slot], sem.at[slot, 2]),
    )


def _start_stream(copies):
    for j in range(W_SLOTS):
        for cp in copies(j, j):
            cp.start()


def _land_chunk(copies, j, win_ref, wout_ref, st_in, st_out):
    slot, lo = j % W_SLOTS, j * FF_CHUNK
    for cp in copies(j, slot):
        cp.wait()
    win_ref[:, lo:lo + FF_CHUNK] = st_in[slot, 0].astype(BF16)
    win_ref[:, D_FF + lo:D_FF + lo + FF_CHUNK] = st_in[slot, 1].astype(BF16)
    wout_ref[lo:lo + FF_CHUNK, :] = st_out[slot].astype(BF16)
    if j + W_SLOTS < D_FF // FF_CHUNK:
        for cp in copies(j + W_SLOTS, slot):
            cp.start()


def _ffn_compute(stream, x_ref, sh_ref, sc_ref, gt_ref, g_ref, win_hbm, wout_hbm, fg_ref, mix_refs,
                 o_ref, mid_ref, win_ref, wout_ref, st_in, st_out, sem, *, final, n_mix, mix_bias, w_idx):
    copies = functools.partial(_weight_copies, win_hbm, wout_hbm, st_in, st_out, sem, w_idx)
    if stream:
        _start_stream(copies)
    x = x_ref[0]
    if n_mix:
        acts, mg_ref, mw_ref = mix_refs[:n_mix], mix_refs[n_mix], mix_refs[n_mix + 1]
        y0, r = None, 0
        for a_ref in acts:
            k = a_ref.shape[-1]
            t = _dot(a_ref[0], mw_ref[r:r + k, :])
            y0 = t if y0 is None else y0 + t
            r += k
        if mix_bias:
            y0 = y0 + mix_refs[n_mix + 2][...]
        x = x + mg_ref[0] * y0
    xb = _modnorm(x, g_ref[0], sh_ref[0], sc_ref[0])
    for j in range(D_FF // FF_CHUNK):
        lo = j * FF_CHUNK
        if stream:
            _land_chunk(copies, j, win_ref, wout_ref, st_in, st_out)
        gate = _dot(xb, win_ref[:, lo:lo + FF_CHUNK])
        up = _dot(xb, win_ref[:, D_FF + lo:D_FF + lo + FF_CHUNK])
        mid_ref[:, lo:lo + FF_CHUNK] = (_silu(gate) * up).astype(BF16)
    y = _dot(mid_ref[...], wout_ref[...])
    h = x + (0.5 * gt_ref[0]) * y
    if final:
        h = _rms(h, fg_ref[0])
    o_ref[0] = h


def _preload_weights(win_hbm, wout_hbm, win_ref, wout_ref, st_in, st_out, sem, w_idx):
    copies = functools.partial(_weight_copies, win_hbm, wout_hbm, st_in, st_out, sem, w_idx)
    _start_stream(copies)
    for j in range(D_FF // FF_CHUNK):
        _land_chunk(copies, j, win_ref, wout_ref, st_in, st_out)


def _ffn_body(x_ref, sh_ref, sc_ref, gt_ref, g_ref, win_hbm, wout_hbm, fg_ref, *rest, single_step, **static):
    args = (x_ref, sh_ref, sc_ref, gt_ref, g_ref, win_hbm, wout_hbm, fg_ref, rest[:-7], *rest[-7:])
    is_first = (pl.program_id(0) == 0) & (pl.program_id(1) == 0)
    if single_step:
        _ffn_compute(True, *args, **static)
    elif static["n_mix"]:
        pl.when(is_first)(lambda: _preload_weights(win_hbm, wout_hbm, *rest[-5:], static["w_idx"]))
        _ffn_compute(False, *args, **static)
    else:
        pl.when(is_first)(lambda: _ffn_compute(True, *args, **static))
        pl.when(jnp.logical_not(is_first))(lambda: _ffn_compute(False, *args, **static))


def _ffn(h, mod, row0, k0, g_all, g_row, w_in, w_out, w_idx, final_g, final, mix=None):
    nb, n, _ = h.shape
    tm = min(TM, n)
    in_specs = [
        pl.BlockSpec((1, tm, D_MODEL), lambda b, m: (b, m, 0)),
        _mod_spec(row0, k0), _mod_spec(row0, k0 + 1), _mod_spec(row0, k0 + 2),
        _const_row_spec(g_row),
        pl.BlockSpec(memory_space=pl.ANY),
        pl.BlockSpec(memory_space=pl.ANY),
        _const_row_spec(0),
    ]
    args = [h, mod, mod, mod, g_all, w_in, w_out, final_g]
    n_mix, mix_bias = 0, False
    if mix is not None:
        acts, mw, mb, mk = mix
        n_mix, mix_bias = len(acts), mb is not None
        in_specs += [pl.BlockSpec((1, tm, a.shape[-1]), lambda b, m: (b, m, 0)) for a in acts]
        in_specs += [_mod_spec(row0, mk), _resident(mw.shape, lambda b, m: (0, 0))]
        args += [*acts, mod, mw]
        if mix_bias:
            in_specs.append(pl.BlockSpec((1, D_MODEL), lambda b, m: (0, 0)))
            args.append(mb)
    return pl.pallas_call(
        functools.partial(_ffn_body, final=final, n_mix=n_mix, mix_bias=mix_bias, w_idx=w_idx,
                          single_step=(nb * (n // tm) == 1)),
        grid=(nb, n // tm),
        in_specs=in_specs,
        out_specs=pl.BlockSpec((1, tm, D_MODEL), lambda b, m: (b, m, 0)),
        out_shape=jax.ShapeDtypeStruct(h.shape, F32),
        scratch_shapes=[
            pltpu.VMEM((tm, D_FF), BF16),
            pltpu.VMEM((D_MODEL, 2 * D_FF), BF16),
            pltpu.VMEM((D_FF, D_MODEL), BF16),
            pltpu.VMEM((W_SLOTS, 2, D_MODEL, FF_CHUNK), F32),
            pltpu.VMEM((W_SLOTS, FF_CHUNK, D_MODEL), F32),
            pltpu.SemaphoreType.DMA((W_SLOTS, 3)),
        ],
        compiler_params=pltpu.CompilerParams(dimension_semantics=("arbitrary", "arbitrary"),
                                             vmem_limit_bytes=FFN_VMEM_LIMIT),
        name="ffn_final" if final else "ffn",
    )(*args)


def _proj_body(x_ref, sh_ref, sc_ref, g_ref, w_ref, u_ref, qkv_ref, wbf_ref):
    @pl.when((pl.program_id(0) == 0) & (pl.program_id(1) == 0))
    def _():
        wbf_ref[...] = w_ref[...].astype(BF16)

    xb = _modnorm(x_ref[0], g_ref[0], sh_ref[0], sc_ref[0])
    u_ref[0] = _dot(xb, wbf_ref[:, :2 * D_CONV])
    qkv_ref[0] = _dot(xb, wbf_ref[:, 2 * D_CONV:]).astype(BF16)


def _proj(h, mod, row0, k0, g_all, g_row, w):
    nb, n, _ = h.shape
    return pl.pallas_call(
        _proj_body,
        grid=(nb, n // TM),
        in_specs=[
            pl.BlockSpec((1, TM, D_MODEL), lambda b, m: (b, m, 0)),
            _mod_spec(row0, k0), _mod_spec(row0, k0 + 1),
            _const_row_spec(g_row),
            _resident(w.shape, lambda b, m: (0, 0)),
        ],
        out_specs=[
            pl.BlockSpec((1, TM, 2 * D_CONV), lambda b, m: (b, m, 0)),
            pl.BlockSpec((1, TM, 3 * D_NA), lambda b, m: (b, m, 0)),
        ],
        out_shape=[
            jax.ShapeDtypeStruct((nb, n, 2 * D_CONV), F32),
            jax.ShapeDtypeStruct((nb, n, 3 * D_NA), BF16),
        ],
        scratch_shapes=[pltpu.VMEM(w.shape, BF16)],
        compiler_params=_params("arbitrary", "arbitrary"),
        name="ab_proj",
    )(h, mod, mod, g_all, w)


def _qkv_body(x_ref, sh_ref, sc_ref, g_ref, w_ref, qkv_ref):
    xb = _modnorm(x_ref[0], g_ref[0], sh_ref[0], sc_ref[0])
    qkv_ref[0] = _dot(xb, w_ref[:, 2 * D_CONV:].astype(BF16)).astype(BF16)


def _qkv_proj(h, mod, row0, k0, g_all, g_row, w):
    nb, n, _ = h.shape
    tm = min(TM, n)
    return pl.pallas_call(
        _qkv_body,
        grid=(nb, n // tm),
        in_specs=[
            pl.BlockSpec((1, tm, D_MODEL), lambda b, m: (b, m, 0)),
            _mod_spec(row0, k0), _mod_spec(row0, k0 + 1),
            _const_row_spec(g_row),
            _resident(w.shape, lambda b, m: (0, 0)),
        ],
        out_specs=pl.BlockSpec((1, tm, 3 * D_NA), lambda b, m: (b, m, 0)),
        out_shape=jax.ShapeDtypeStruct((nb, n, 3 * D_NA), BF16),
        compiler_params=_params("arbitrary", "arbitrary"),
        name="ctx_qkv",
    )(h, mod, mod, g_all, w)


def _glu(v):
    return v[:, :D_CONV] * (1.0 / (1.0 + jnp.exp(-v[:, D_CONV:])))


CONV_RC = 64
CONV_N = CONV_TL + 2 * CONV_HALO


def _conv_body(cur_ref, prev_ref, next_ref, w_ref, b_ref, lg_ref, lb_ref, o_ref, y_ref):
    t = pl.program_id(1)
    nt = pl.num_programs(1)
    y_ref[0, CONV_HALO:CONV_HALO + CONV_TL, :] = _glu(cur_ref[0])
    y_ref[0, 0:CONV_HALO, :] = jnp.where(t > 0, _glu(prev_ref[0]), 0.0)
    y_ref[0, CONV_HALO + CONV_TL:, :] = jnp.where(t < nt - 1, _glu(next_ref[0]), 0.0)
    for s in range(1, SUBLANES):
        y_ref[s, 0:CONV_N - SUBLANES, :] = y_ref[0, s:s + CONV_N - SUBLANES, :]
    off = CONV_HALO - CONV_WIDTH // 2
    for r in range(0, CONV_TL, CONV_RC):
        acc = jnp.zeros((CONV_RC // SUBLANES, SUBLANES, D_CONV), F32)
        for k in range(CONV_WIDTH):
            m8, s = divmod(off + k, SUBLANES)
            lo = r + SUBLANES * m8
            yk = y_ref[s, lo:lo + CONV_RC, :].reshape(CONV_RC // SUBLANES, SUBLANES, D_CONV)
            acc = acc + w_ref[k][None] * yk
        acc = acc.reshape(CONV_RC, D_CONV) + b_ref[...]
        mu = jnp.mean(acc, axis=-1, keepdims=True)
        cen = acc - mu
        var = jnp.mean(cen * cen, axis=-1, keepdims=True)
        z = cen * lax.rsqrt(var + EPS) * lg_ref[...] + lb_ref[...]
        o_ref[0, r:r + CONV_RC, :] = _silu(z).astype(BF16)


def _conv(u, w, b, ln_g, ln_b):
    nb, n, _ = u.shape
    nt = n // CONV_TL
    hb = CONV_TL // CONV_HALO
    last = n // CONV_HALO - 1
    row = lambda v: v.reshape(1, D_CONV)
    w_rep = jnp.broadcast_to(w[:, None, :], (CONV_WIDTH, SUBLANES, D_CONV))
    return pl.pallas_call(
        _conv_body,
        grid=(nb, nt),
        in_specs=[
            pl.BlockSpec((1, CONV_TL, 2 * D_CONV), lambda b_, t: (b_, t, 0)),
            pl.BlockSpec((1, CONV_HALO, 2 * D_CONV), lambda b_, t: (b_, jnp.maximum(t * hb - 1, 0), 0)),
            pl.BlockSpec((1, CONV_HALO, 2 * D_CONV), lambda b_, t: (b_, jnp.minimum((t + 1) * hb, last), 0)),
            pl.BlockSpec((CONV_WIDTH, SUBLANES, D_CONV), lambda b_, t: (0, 0, 0)),
            pl.BlockSpec((1, D_CONV), lambda b_, t: (0, 0)),
            pl.BlockSpec((1, D_CONV), lambda b_, t: (0, 0)),
            pl.BlockSpec((1, D_CONV), lambda b_, t: (0, 0)),
        ],
        out_specs=pl.BlockSpec((1, CONV_TL, D_CONV), lambda b_, t: (b_, t, 0)),
        out_shape=jax.ShapeDtypeStruct((nb, n, D_CONV), BF16),
        scratch_shapes=[pltpu.VMEM((SUBLANES, CONV_N, D_CONV), F32)],
        compiler_params=_params("arbitrary", "arbitrary"),
        name="conv_module",
    )(u, u, u, w_rep, row(b), row(ln_g), row(ln_b))


NA_HG = 4
NA_DR_PAD = 8
NA_T2 = 2 * NA_KH - 1 + 2 * NA_DR_PAD - 1


def _na_bias_table(rpb):
    qc = np.arange(GRID_W)
    cs = np.clip(qc - NA_KW // 2, 0, GRID_W - NA_KW)
    kc = np.arange(GRID_W)
    col_valid = (kc[None, :] >= cs[:, None]) & (kc[None, :] < cs[:, None] + NA_KW)
    dc = kc[None, :] - qc[:, None] + NA_KW - 1
    oh_c = np.zeros((2 * NA_KW - 1, GRID_W, GRID_W), np.float32)
    qi, ki = np.nonzero(col_valid)
    oh_c[dc[qi, ki], qi, ki] = 1.0
    t_col = jnp.einsum('hrd,dqk->hrqk', rpb, jnp.asarray(oh_c), precision=lax.Precision.HIGHEST)
    t_col = jnp.where(jnp.asarray(col_valid)[None, None], t_col, NEG)
    t_pad = jnp.pad(t_col, ((0, 0), (NA_DR_PAD, NA_DR_PAD), (0, 0), (0, 0)))
    return jnp.concatenate([t_pad[:, :NA_T2], t_pad[:, 1:NA_T2 + 1]], axis=-1)


def _na_window(blk):
    w0 = min(max(NA_ROWS * blk - NA_KH // 2, 0), GRID_W - NA_KROWS)
    a_lo = [min(max(NA_ROWS * blk + i - NA_KH // 2, 0), GRID_W - NA_KH) - w0 for i in range(NA_ROWS)]
    return w0, a_lo


def _na_probs(t2_ref, blk, j, s, sc):
    w0, a_lo = _na_window(blk)
    base = w0 - NA_ROWS * blk + NA_KH - 1 + NA_DR_PAD
    plane = lax.broadcasted_iota(jnp.int32, (1, LANES), 1)
    zero = jnp.zeros((GRID_W, LANES), BF16)
    p_rows, pc_rows = [], []
    for i in range(NA_ROWS):
        rows = slice(i * GRID_W, (i + 1) * GRID_W)
        pieces = {}
        for p in range(NA_KROWS // 2):
            first = a_lo[i] <= 2 * p < a_lo[i] + NA_KH
            second = a_lo[i] <= 2 * p + 1 < a_lo[i] + NA_KH
            if not (first or second):
                continue
            piece = s[rows, p * LANES:(p + 1) * LANES] + t2_ref[j, base + 2 * p - i]
            if not (first and second):
                piece = jnp.where((plane < GRID_W) if first else (plane >= GRID_W), piece, NEG)
            pieces[p] = piece
        sc_i = sc[rows]
        m = jnp.max(sc_i, axis=-1, keepdims=True)
        for piece in pieces.values():
            m = jnp.maximum(m, jnp.max(piece, axis=-1, keepdims=True))
        p_rows.append(jnp.concatenate(
            [jnp.exp(pieces[p] - m).astype(BF16) if p in pieces else zero for p in range(NA_KROWS // 2)], axis=-1))
        pc_rows.append(jnp.exp(sc_i - m).astype(BF16))
    return jnp.concatenate(p_rows, axis=0), jnp.concatenate(pc_rows, axis=0)


def _na_body(q_ref, k_ref, v_ref, kc_ref, vc_ref, t2_ref, o_ref):
    blk = pl.program_id(2)
    w0 = jnp.clip(NA_ROWS * blk - NA_KH // 2, 0, GRID_W - NA_KROWS)
    start = pl.multiple_of(w0 * GRID_W, (NA_KH // 2) * GRID_W)
    q2 = q_ref[0] * jnp.asarray(HEAD_DIM ** -0.5, BF16)
    kw = k_ref[0, pl.ds(start, NA_KB), :]
    vw = v_ref[0, pl.ds(start, NA_KB), :]
    kc = kc_ref[0]
    vc = vc_ref[0]
    lane = lax.broadcasted_iota(jnp.int32, (1, NA_HG * HEAD_DIM), 1)
    nt = (((1,), (1,)), ((), ()))
    one = jnp.ones((), BF16)

    def run(probs_fn):
        out = None
        for j in range(NA_HG):
            in_head = (lane >= j * HEAD_DIM) & (lane < (j + 1) * HEAD_DIM)
            qm = jnp.where(in_head, q2, jnp.zeros_like(q2))
            s = lax.dot_general(qm, kw, nt, preferred_element_type=F32)
            sc = lax.dot_general(qm, kc, nt, preferred_element_type=F32)
            p, pc = probs_fn(j, s, sc)
            o = _dot(p, jnp.where(in_head, vw, one)) + _dot(pc, jnp.where(in_head, vc, one))
            o = o * (1.0 / pltpu.roll(o, HEAD_DIM, axis=1))
            out = o if out is None else jnp.where(in_head, o, out)
        o_ref[0] = out.astype(BF16)

    groups = {}
    for b in range(GRID_W // NA_ROWS):
        w0_b, a_lo_b = _na_window(b)
        groups.setdefault((w0_b - NA_ROWS * b, tuple(a_lo_b)), []).append(b)
    for blks in groups.values():
        assert blks == list(range(blks[0], blks[-1] + 1))
        pl.when((blk >= blks[0]) & (blk <= blks[-1]))(
            functools.partial(run, functools.partial(_na_probs, t2_ref, blks[0])))


def _natten(qkv, qkv_c, t2):
    nb, n, _ = qkv.shape
    nctx = qkv_c.shape[1]
    ng = NA_HEADS // NA_HG
    lanes = NA_HG * HEAD_DIM
    return pl.pallas_call(
        _na_body,
        grid=(ng, nb, n // NA_QB),
        in_specs=[
            pl.BlockSpec((1, NA_QB, lanes), lambda h, b, i: (b, i, h)),
            pl.BlockSpec((1, n, lanes), lambda h, b, i: (b, 0, ng + h)),
            pl.BlockSpec((1, n, lanes), lambda h, b, i: (b, 0, 2 * ng + h)),
            pl.BlockSpec((1, nctx, lanes), lambda h, b, i: (b, 0, ng + h)),
            pl.BlockSpec((1, nctx, lanes), lambda h, b, i: (b, 0, 2 * ng + h)),
            pl.BlockSpec((NA_HG, NA_T2, GRID_W, 2 * GRID_W), lambda h, b, i: (h, 0, 0, 0)),
        ],
        out_specs=pl.BlockSpec((1, NA_QB, lanes), lambda h, b, i: (b, i, h)),
        out_shape=jax.ShapeDtypeStruct((nb, n, D_NA), BF16),
        compiler_params=_params("arbitrary", "arbitrary", "arbitrary"),
        name="natten",
    )(qkv, qkv, qkv, qkv_c, qkv_c, t2)


SEQ = GRID_W * GRID_W
FN_R = SUBLANES
FN_M = SEQ // FN_R
FN_TA = 2048
FN_LANES = 256
FN_CH = 16


def _dft_tables(n):
    idx = np.arange(n, dtype=np.int64)
    ang = 2.0 * np.pi * ((idx[:, None] * idx[None, :]) % n).astype(np.float64) / n
    scale = 1.0 / np.sqrt(n)
    return (np.cos(ang) * scale).astype(np.float32), (np.sin(ang) * scale).astype(np.float32)


def _seq_tables():
    k2 = np.arange(FN_M, dtype=np.int64)
    n2 = np.arange(FN_M, dtype=np.int64)
    out = np.zeros((FN_R, 3, FN_M, FN_M), np.float32)
    for n1 in range(FN_R):
        num = (k2[:, None] * n2[None, :] * FN_R + n1 * k2[:, None]) % SEQ
        ang = 2.0 * np.pi * num.astype(np.float64) / SEQ
        c = np.cos(ang) / np.sqrt(SEQ)
        s = np.sin(ang) / np.sqrt(SEQ)
        out[n1, 0] = c
        out[n1, 1] = c - s
        out[n1, 2] = -(c + s)
    return out


def _fnet_a_body(*refs):
    nx = D_MODEL // LANES
    x_refs = refs[:nx]
    sh_ref, sc_ref, g_ref, cs_ref, a_ref, b_ref = refs[nx:]
    gw = D_MODEL // FNET_GROUPS
    for s in range(FN_R):
        xs = jnp.concatenate([x[0, pl.ds(s, FN_TA // FN_R, stride=FN_R), :] for x in x_refs], axis=-1)
        xb = _modnorm(xs, g_ref[0], sh_ref[0], sc_ref[0])
        for grp in range(FNET_GROUPS):
            ab = _dot(xb[:, grp * gw:(grp + 1) * gw], cs_ref[...])
            a_ref[0, s, :, grp * gw:(grp + 1) * gw] = ab[:, :gw].astype(BF16)
            b_ref[0, s, :, grp * gw:(grp + 1) * gw] = ab[:, gw:].astype(BF16)


def _fnet_a(h, mod, row0, k0, g_all, g_row, cs):
    nb, n, _ = h.shape
    gw = D_MODEL // FNET_GROUPS
    nx = D_MODEL // LANES
    out_spec = pl.BlockSpec((1, FN_R, FN_TA // FN_R, D_MODEL), lambda b, m: (b, 0, m, 0))
    return pl.pallas_call(
        _fnet_a_body,
        grid=(nb, n // FN_TA),
        in_specs=[pl.BlockSpec((1, FN_TA, LANES), functools.partial(lambda b, m, c: (b, m, c), c=c))
                  for c in range(nx)]
        + [_mod_spec(row0, k0), _mod_spec(row0, k0 + 1), _const_row_spec(g_row),
           _resident((gw, 2 * gw), lambda b, m: (0, 0))],
        out_specs=[out_spec, out_spec],
        out_shape=[jax.ShapeDtypeStruct((nb, FN_R, n // FN_R, D_MODEL), BF16)] * 2,
        compiler_params=_params("arbitrary", "arbitrary"),
        name="fnet_channel_dft",
    )(*([h] * nx), mod, mod, g_all, cs)


def _fnet_b_body(zr_ref, zi_ref, m_ref, o_ref, v_ref):
    for n1 in range(FN_R):
        zr, zi = zr_ref[0, n1], zi_ref[0, n1]
        k1 = _dot(m_ref[n1, 0], (zr.astype(F32) + zi.astype(F32)).astype(BF16))
        k3 = _dot(m_ref[n1, 1], zi)
        k2 = _dot(m_ref[n1, 2], zr)
        v_ref[n1, :FN_M, :] = k1 - k3
        v_ref[n1, FN_M:, :] = k1 + k2

    rt = np.float32(np.sqrt(0.5))

    def chunk(i, carry):
        r0 = pl.multiple_of(i * FN_CH, FN_CH)
        for lt in range(FN_LANES // LANES):
            ls = slice(lt * LANES, (lt + 1) * LANES)
            re = [v_ref[n, pl.ds(r0, FN_CH), ls] for n in range(FN_R)]
            im = [v_ref[n, pl.ds(FN_M + r0, FN_CH), ls] for n in range(FN_R)]
            e0 = (re[0] + re[4]) + (re[2] + re[6])
            e2 = (re[0] + re[4]) - (re[2] + re[6])
            e1 = (re[0] - re[4]) + (im[2] - im[6])
            e3 = (re[0] - re[4]) - (im[2] - im[6])
            t0r, t0i = re[1] + re[5], im[1] + im[5]
            t1r, t1i = re[1] - re[5], im[1] - im[5]
            t2r, t2i = re[3] + re[7], im[3] + im[7]
            t3r, t3i = re[3] - re[7], im[3] - im[7]
            p0 = t0r + t2r
            p2 = t0i - t2i
            al = t1r - t3r
            be = t1i + t3i
            p1 = (al + be) * rt
            p3 = (be - al) * rt
            ys = (e0 + p0, e1 + p1, e2 + p2, e3 + p3, e0 - p0, e1 - p1, e2 - p2, e3 - p3)
            for k1 in range(FN_R):
                o_ref[0, pl.ds(k1 * FN_M + r0, FN_CH), ls] = ys[k1].astype(BF16)
        return carry

    lax.fori_loop(0, FN_M // FN_CH, chunk, 0)


def _fnet_b(zr, zi, mtab):
    assert FN_R == 8
    nb = zr.shape[0]
    z_spec = pl.BlockSpec((1, FN_R, FN_M, FN_LANES), lambda b, l: (b, 0, 0, l))
    return pl.pallas_call(
        _fnet_b_body,
        grid=(nb, D_MODEL // FN_LANES),
        in_specs=[z_spec, z_spec, _resident((FN_R, 3, FN_M, FN_M), lambda b, l: (0, 0, 0, 0))],
        out_specs=pl.BlockSpec((1, SEQ, FN_LANES), lambda b, l: (b, 0, l)),
        out_shape=jax.ShapeDtypeStruct((nb, SEQ, D_MODEL), BF16),
        scratch_shapes=[pltpu.VMEM((FN_R, 2 * FN_M, FN_LANES), F32)],
        compiler_params=_params("arbitrary", "arbitrary"),
        name="fnet_seq_dft",
    )(zr, zi, mtab)


def kernel(x, c, ctx, c_ctx, ada_w, ada_b, norm_g, ffn_w_in, ffn_w_out, ab_w_in, conv_w, conv_b,
           conv_ln_g, conv_ln_b, na_rpb, ab_w_out, fnet_w, fnet_b, final_g):
    nb, n, d = x.shape
    depth = ada_w.shape[0]
    nctx = ctx.shape[1]
    assert (d, depth, nb) == (D_MODEL, 2, 4) and n == GRID_W * GRID_W

    cc = jnp.concatenate([c, c_ctx[None], jnp.zeros((MOD_ROWS - nb - 1, d), F32)], axis=0)
    mod = _ada(cc, ada_w, ada_b).reshape(depth * MOD_ROWS, 1, N_MOD * d)
    ctx_row = nb
    lyr1 = MOD_ROWS

    g_all = norm_g.reshape(depth * 3, 1, d)
    fg = final_g.reshape(1, 1, d)
    w_in = ffn_w_in.reshape(depth * 2, d, 2 * D_FF)
    w_out = ffn_w_out.reshape(depth * 2, D_FF, d)

    h = _ffn(x, mod, 0, 0, g_all, 0, w_in, w_out, 0, fg, False)
    hc = _ffn(ctx.reshape(1, nb * nctx, d), mod, ctx_row, 0, g_all, 0, w_in, w_out, 0, fg, False)
    assert ab_w_in.shape[0] == 1
    w_ab = ab_w_in.reshape(ab_w_in.shape[1:])
    u, qkv = _proj(h, mod, 0, 3, g_all, 1, w_ab)
    qkv_c = _qkv_proj(hc, mod, ctx_row, 3, g_all, 1, w_ab)
    conv_x = _conv(u, conv_w[0], conv_b[0], conv_ln_g[0], conv_ln_b[0])
    att_x = _natten(qkv, qkv_c.reshape(nb, nctx, 3 * D_NA), _na_bias_table(na_rpb[0]))
    h = _ffn(h, mod, 0, 6, g_all, 2, w_in, w_out, 1, fg, False,
             mix=((conv_x, att_x), ab_w_out[0].astype(BF16), None, 5))

    h = _ffn(h, mod, lyr1, 0, g_all, 3, w_in, w_out, 2, fg, False)
    gw = d // FNET_GROUPS
    cc_tab, sc_tab = _dft_tables(gw)
    cs = jnp.asarray(np.concatenate([cc_tab, -sc_tab], axis=1)).astype(BF16)
    zr, zi = _fnet_a(h, mod, lyr1, 3, g_all, 4, cs)
    f = _fnet_b(zr, zi, jnp.asarray(_seq_tables()).astype(BF16))
    return _ffn(h, mod, lyr1, 6, g_all, 5, w_in, w_out, 3, fg, True,
                mix=((f,), fnet_w[0].astype(BF16), fnet_b[0].reshape(1, d), 5))
```
